```python
import math
import jax, jax.numpy as jnp
from jax import lax
import numpy as np

D_MODEL = 1024
BATCH = 16
SEQ = 2048
DEPTH = 4
DEC_BATCH = 1
DEC_SEQ = 16384
PAST_LEN = 128

EPS = 1e-6
N_EVEN = (DEPTH + 1) // 2
N_ODD = DEPTH // 2

A_HEADS = 8
A_KV_HEADS = 2
A_HEAD_DIM = 64
A_WINDOW = 128
A_BLOCK = 128
A_Q_DIM = A_HEADS * A_HEAD_DIM
A_KV_DIM = A_KV_HEADS * A_HEAD_DIM
T5_BUCKETS = 32
T5_MAX_DIST = 128

SSD_HEADS = 16
SSD_HEAD_DIM = 64
SSD_INNER = SSD_HEADS * SSD_HEAD_DIM
SSD_GROUPS = 2
SSD_STATE = 64
SSD_CONV = 5
SSD_CHUNK = 64
SSD_XBC = SSD_INNER + 2 * SSD_GROUPS * SSD_STATE

EV_SIZES = [A_Q_DIM, A_KV_DIM, A_KV_DIM, SSD_INNER, SSD_XBC, 2 * SSD_HEADS]
EV_IN = sum(EV_SIZES)
EV_SPLITS = np.cumsum(EV_SIZES)[:-1].tolist()
EV_MIX = A_Q_DIM + SSD_INNER
XBC_SPLITS = [SSD_INNER, SSD_INNER + SSD_GROUPS * SSD_STATE]

HG_HEADS = 8
HG_EXPAND = 128
HG_HEAD_V = D_MODEL // HG_HEADS
HG_FDIM = HG_HEADS * HG_EXPAND
HG_CHUNK = 32
OD_SIZES = [HG_FDIM, HG_FDIM, HG_FDIM, D_MODEL, D_MODEL]
OD_IN = sum(OD_SIZES)
OD_SPLITS = np.cumsum(OD_SIZES)[:-1].tolist()

D_FF = -(-8 * D_MODEL // (3 * 256)) * 256

kernel_name = "hybrid_bidir_swa_ssd_hgrn2_encoder"


def rmsnorm(x, w):
    xf = x.astype(jnp.float32)
    y = xf * lax.rsqrt(jnp.mean(xf * xf, axis=-1, keepdims=True) + EPS)
    return (y * w.astype(jnp.float32)).astype(x.dtype)


def _flip(t):
    return jnp.flip(t, axis=1)


def t5_bucket(rel):
    half = T5_BUCKETS // 2
    max_exact = half // 2
    n = np.abs(rel)
    large = max_exact + (np.log(np.maximum(n, 1) / max_exact)
                         / np.log(T5_MAX_DIST / max_exact) * (half - max_exact)).astype(np.int32)
    large = np.minimum(large, half - 1)
    return ((rel > 0).astype(np.int32) * half + np.where(n < max_exact, n, large)).astype(np.int32)


def windowed_gqa(q, k, v, sink, t5_bias):
    b, S = q.shape[:2]
    nb = S // A_BLOCK
    rep = A_HEADS // A_KV_HEADS
    pad = ((0, 0), (A_BLOCK, A_BLOCK), (0, 0), (0, 0))

    def bands(t):
        tb = jnp.pad(t, pad).reshape(b, nb + 2, A_BLOCK, A_KV_HEADS, A_HEAD_DIM)
        return jnp.concatenate([tb[:, :-2], tb[:, 1:-1], tb[:, 2:]], axis=2)

    kb, vb = bands(k), bands(v)
    qb = q.reshape(b, nb, A_BLOCK, A_KV_HEADS, rep, A_HEAD_DIM)
    s = jnp.einsum('bnqgrd,bnkgd->bngrqk', qb, kb,
                   preferred_element_type=jnp.float32) * (A_HEAD_DIM ** -0.5)
    qi = np.arange(A_BLOCK)[:, None]
    kj = np.arange(3 * A_BLOCK)[None, :] - A_BLOCK
    rel = kj - qi
    bias = jnp.transpose(t5_bias[t5_bucket(rel)], (2, 0, 1)).astype(jnp.float32)
    bias = bias.reshape(A_KV_HEADS, rep, A_BLOCK, 3 * A_BLOCK)
    kpos = np.arange(nb)[:, None, None] * A_BLOCK + kj[None]
    valid = (np.abs(rel)[None] <= A_WINDOW) & (kpos >= 0) & (kpos < S)
    s = jnp.where(valid[None, :, None, None], s + bias, -jnp.inf)
    sk = sink.astype(jnp.float32).reshape(1, 1, A_KV_HEADS, rep, 1, 1)
    m = jnp.maximum(jnp.max(s, axis=-1, keepdims=True), sk)
    p = jnp.exp(s - m)
    denom = jnp.sum(p, axis=-1, keepdims=True) + jnp.exp(sk - m)
    o = jnp.einsum('bngrqk,bnkgd->bnqgrd', (p / denom).astype(v.dtype), vb)
    return o.reshape(b, S, A_Q_DIM)


def dwconv_centred(x, w, bias):
    y = lax.conv_general_dilated(x, w[:, None, :].astype(x.dtype), window_strides=(1,),
                                 padding=((SSD_CONV // 2, SSD_CONV // 2),),
                                 dimension_numbers=('NWC', 'WIO', 'NWC'),
                                 feature_group_count=x.shape[-1])
    return y + bias.astype(x.dtype)


def ssd_scan(x, dt, a, bm, cm):
    b, S = x.shape[:2]
    L = SSD_CHUNK
    nc = S // L
    r = SSD_HEADS // SSD_GROUPS
    xc = (x.astype(jnp.float32) * dt[..., None]).reshape(b, nc, L, SSD_GROUPS, r, SSD_HEAD_DIM)
    cs = jnp.cumsum((dt * a).reshape(b, nc, L, SSD_GROUPS, r), axis=2)
    bc = bm.astype(jnp.float32).reshape(b, nc, L, SSD_GROUPS, SSD_STATE)
    cc = cm.astype(jnp.float32).reshape(b, nc, L, SSD_GROUPS, SSD_STATE)
    cst = jnp.moveaxis(cs, 2, -1)
    causal = np.tril(np.ones((L, L), dtype=bool))
    decay = jnp.exp(jnp.where(causal, cst[..., :, None] - cst[..., None, :], -jnp.inf))
    cb = jnp.einsum('bclgn,bcsgn->bcgls', cc, bc)
    y_diag = jnp.einsum('bcgrls,bcsgrp->bclgrp', cb[:, :, :, None] * decay, xc)
    xw = xc * jnp.exp(cs[:, :, -1:] - cs)[..., None]
    out_decay = jnp.exp(cs)
    chunk_decay = jnp.exp(cs[:, :, -1])

    def step(h, inp):
        b_c, x_c, c_c, od_c, dc = inp
        y = jnp.einsum('blgn,bgrpn->blgrp', c_c, h) * od_c[..., None]
        h = dc[..., None, None] * h + jnp.einsum('blgn,blgrp->bgrpn', b_c, x_c)
        return h, y

    h0 = jnp.zeros((b, SSD_GROUPS, r, SSD_HEAD_DIM, SSD_STATE), jnp.float32)
    seq = tuple(jnp.moveaxis(t, 1, 0) for t in (bc, xw, cc, out_decay, chunk_decay))
    _, y_off = lax.scan(step, h0, seq)
    y = y_diag + jnp.moveaxis(y_off, 0, 1)
    return y.reshape(b, S, SSD_HEADS, SSD_HEAD_DIM)


def ssd_mixer(z, xbc, dt_raw, conv_w, conv_b, a_log, dt_bias, d_skip, norm_w):
    b, S = z.shape[:2]
    xbc = jax.nn.silu(dwconv_centred(xbc, conv_w, conv_b))
    xs, bm, cm = jnp.split(xbc, XBC_SPLITS, axis=-1)
    xs = xs.reshape(b, S, SSD_HEADS, SSD_HEAD_DIM)
    bm = bm.reshape(b, S, SSD_GROUPS, SSD_STATE)
    cm = cm.reshape(b, S, SSD_GROUPS, SSD_STATE)
    dt = jax.nn.softplus(dt_raw.astype(jnp.float32).reshape(b, S, 2, SSD_HEADS)
                         + dt_bias.astype(jnp.float32))
    a = -jnp.exp(a_log.astype(jnp.float32))
    y = ssd_scan(xs, dt[:, :, 0], a[0], bm, cm)
    y = y + _flip(ssd_scan(_flip(xs), _flip(dt[:, :, 1]), a[1], _flip(bm), _flip(cm)))
    y = y + d_skip.astype(jnp.float32)[:, None] * xs.astype(jnp.float32)
    y = y.reshape(b, S, SSD_INNER) * jax.nn.silu(z.astype(jnp.float32))
    yg = y.reshape(b, S, SSD_GROUPS, SSD_INNER // SSD_GROUPS)
    yg = yg * lax.rsqrt(jnp.mean(yg * yg, axis=-1, keepdims=True) + EPS)
    return (yg.reshape(b, S, SSD_INNER) * norm_w.astype(jnp.float32)).astype(z.dtype)


def even_mixer(h, w_in, sink, t5_bias, conv_w, conv_b, a_log, dt_bias, d_skip, norm_w, w_out):
    b, S, _ = h.shape
    q, k, v, z, xbc, dt_raw = jnp.split(h @ w_in, EV_SPLITS, axis=-1)
    a_out = windowed_gqa(q.reshape(b, S, A_HEADS, A_HEAD_DIM),
                         k.reshape(b, S, A_KV_HEADS, A_HEAD_DIM),
                         v.reshape(b, S, A_KV_HEADS, A_HEAD_DIM), sink, t5_bias)
    b_out = ssd_mixer(z, xbc, dt_raw, conv_w, conv_b, a_log, dt_bias, d_skip, norm_w)
    return jnp.concatenate([a_out, b_out.astype(a_out.dtype)], axis=-1) @ w_out


def hgrn2_scan(q, k, v, logf):
    b, S = q.shape[:2]
    L = HG_CHUNK
    nc = S // L

    def chunks(t):
        return t.reshape(b, nc, L, *t.shape[2:])

    qc, kc, vc, gc = chunks(q), chunks(k), chunks(v), chunks(logf)
    G = jnp.cumsum(gc, axis=2)
    q_in = qc * jnp.exp(G)
    att = jnp.einsum('bclhk,bcshk->bchls', q_in, kc * jnp.exp(-G))
    att = jnp.where(np.tril(np.ones((L, L), dtype=bool)), att, 0.0)
    o_intra = jnp.einsum('bchls,bcshv->bclhv', att, vc)
    k_st = kc * jnp.exp(G[:, :, -1:] - G)
    chunk_decay = jnp.exp(G[:, :, -1])

    def step(st, inp):
        q_c, k_c, v_c, dc = inp
        o = jnp.einsum('blhk,bhkv->blhv', q_c, st)
        st = dc[..., None] * st + jnp.einsum('blhk,blhv->bhkv', k_c, v_c)
        return st, o

    s0 = jnp.zeros((b, HG_HEADS, HG_EXPAND, HG_HEAD_V), jnp.float32)
    seq = tuple(jnp.moveaxis(t, 1, 0) for t in (q_in, k_st, vc, chunk_decay))
    _, o_inter = lax.scan(step, s0, seq)
    return (o_intra + jnp.moveaxis(o_inter, 0, 1)).reshape(b, S, HG_HEADS, HG_HEAD_V)


def odd_mixer(h, w_in, lb, norm_w, w_out):
    b, S, _ = h.shape
    q, f_fwd, f_bwd, i, g = jnp.split(h @ w_in, OD_SPLITS, axis=-1)
    q = jax.nn.silu(q.astype(jnp.float32)).reshape(b, S, HG_HEADS, HG_EXPAND)
    i = i.astype(jnp.float32).reshape(b, S, HG_HEADS, HG_HEAD_V)
    lb = lb.reshape(HG_HEADS, HG_EXPAND)

    def gates(fraw):
        fr = fraw.astype(jnp.float32).reshape(b, S, HG_HEADS, HG_EXPAND)
        f = lb + (1.0 - lb) * jax.nn.sigmoid(fr)
        return (1.0 - lb) * jax.nn.sigmoid(-fr), jnp.log(f)

    k_f, lf_f = gates(f_fwd)
    k_b, lf_b = gates(f_bwd)
    o = hgrn2_scan(q, k_f, i, lf_f)
    o = o + _flip(hgrn2_scan(_flip(q), _flip(k_b), _flip(i), _flip(lf_b)))
    o = o * lax.rsqrt(jnp.mean(o * o, axis=-1, keepdims=True) + EPS) * norm_w.astype(jnp.float32)
    o = o.reshape(b, S, D_MODEL) * jax.nn.silu(g.astype(jnp.float32))
    return o.astype(h.dtype) @ w_out


def swiglu(h, wg, wu, wd):
    return (jax.nn.silu(h @ wg) * (h @ wu)) @ wd


def trunk(x, norm_gains, t5_bias, ev_w_in, attn_sink, ssd_conv_w, ssd_conv_b, ssd_a_log,
          ssd_dt_bias, ssd_d, ssd_norm_w, ev_w_out, od_w_in, hg_lower_bounds, hg_norm_w,
          od_w_out, ffn_w_gate, ffn_w_up, ffn_w_down):
    lb_soft = jax.nn.softmax(hg_lower_bounds.astype(jnp.float32), axis=0)
    lb_all = jnp.cumsum(lb_soft, axis=0) - lb_soft[0]
    for l in range(DEPTH):
        g = norm_gains[l]
        hn = rmsnorm(x, g[0])
        j = l // 2
        if l % 2 == 0:
            m = even_mixer(hn, ev_w_in[j], attn_sink[j], t5_bias, ssd_conv_w[j], ssd_conv_b[j],
                           ssd_a_log[j], ssd_dt_bias[j], ssd_d[j], ssd_norm_w[j], ev_w_out[j])
        else:
            m = odd_mixer(hn, od_w_in[j], lb_all[j], hg_norm_w[j], od_w_out[j])
        x = x + rmsnorm(m, g[1])
        f = swiglu(rmsnorm(x, g[2]), ffn_w_gate[l], ffn_w_up[l], ffn_w_down[l])
        x = x + rmsnorm(f, g[3])
    return x


def setup_inputs(seed: int = 0) -> dict:
    key = jax.random.key(seed)
    ks = jax.random.split(key, 20)

    def nrm(k, shape, scale):
        return scale * jax.random.normal(k, shape, jnp.float32)

    dt0 = jnp.exp(jax.random.uniform(ks[9], (N_EVEN, 2, SSD_HEADS), jnp.float32,
                                     minval=math.log(1e-3), maxval=math.log(1e-1)))
    return {
        "x_prompt": nrm(ks[0], (BATCH, SEQ, D_MODEL), 1.0),
        "x_sample": nrm(ks[1], (DEC_BATCH, DEC_SEQ, D_MODEL), 1.0),
        "norm_gains": 1.0 + nrm(ks[2], (DEPTH, 4, D_MODEL), 0.02),
        "t5_bias": nrm(ks[3], (T5_BUCKETS, A_HEADS), 0.5),
        "ev_w_in": nrm(ks[4], (N_EVEN, D_MODEL, EV_IN), D_MODEL ** -0.5),
        "attn_sink": nrm(ks[5], (N_EVEN, A_HEADS), 0.5),
        "ssd_conv_w": nrm(ks[6], (N_EVEN, SSD_CONV, SSD_XBC), SSD_CONV ** -0.5),
        "ssd_conv_b": nrm(ks[7], (N_EVEN, SSD_XBC), 0.01),
        "ssd_a_log": jnp.log(jax.random.uniform(ks[8], (N_EVEN, 2, SSD_HEADS), jnp.float32, 1.0, 16.0)),
        "ssd_dt_bias": dt0 + jnp.log(-jnp.expm1(-dt0)),
        "ssd_d": 1.0 + nrm(ks[10], (N_EVEN, SSD_HEADS), 0.1),
        "ssd_norm_w": 1.0 + nrm(ks[11], (N_EVEN, SSD_INNER), 0.02),
        "ev_w_out": nrm(ks[12], (N_EVEN, EV_MIX, D_MODEL), EV_MIX ** -0.5),
        "od_w_in": nrm(ks[13], (N_ODD, D_MODEL, OD_IN), D_MODEL ** -0.5),
        "hg_lower_bounds": nrm(ks[14], (N_ODD, HG_FDIM), 0.5),
        "hg_norm_w": 1.0 + nrm(ks[15], (N_ODD, HG_HEAD_V), 0.02),
        "od_w_out": nrm(ks[16], (N_ODD, D_MODEL, D_MODEL), D_MODEL ** -0.5),
        "ffn_w_gate": nrm(ks[17], (DEPTH, D_MODEL, D_FF), D_MODEL ** -0.5),
        "ffn_w_up": nrm(ks[18], (DEPTH, D_MODEL, D_FF), D_MODEL ** -0.5),
        "ffn_w_down": nrm(ks[19], (DEPTH, D_FF, D_MODEL), D_FF ** -0.5),
    }


def reference(x_prompt, x_sample, norm_gains, t5_bias, ev_w_in, attn_sink, ssd_conv_w, ssd_conv_b,
              ssd_a_log, ssd_dt_bias, ssd_d, ssd_norm_w, ev_w_out, od_w_in, hg_lower_bounds,
              hg_norm_w, od_w_out, ffn_w_gate, ffn_w_up, ffn_w_down):
    y_prompt = trunk(x_prompt, norm_gains, t5_bias, ev_w_in, attn_sink, ssd_conv_w, ssd_conv_b,
                     ssd_a_log, ssd_dt_bias, ssd_d, ssd_norm_w, ev_w_out, od_w_in, hg_lower_bounds,
                     hg_norm_w, od_w_out, ffn_w_gate, ffn_w_up, ffn_w_down)
    y_sample = trunk(x_sample, norm_gains, t5_bias, ev_w_in, attn_sink, ssd_conv_w, ssd_conv_b,
                     ssd_a_log, ssd_dt_bias, ssd_d, ssd_norm_w, ev_w_out, od_w_in, hg_lower_bounds,
                     hg_norm_w, od_w_out, ffn_w_gate, ffn_w_up, ffn_w_down)
    return (y_prompt, y_sample)
```

```python
import functools
import math

import numpy as np
import jax
import jax.numpy as jnp
from jax import lax
from jax.experimental import pallas as pl
from jax.experimental.pallas import tpu as pltpu

F32 = jnp.float32
BF16 = jnp.bfloat16

D_MODEL = 1024
DEPTH = 4
EPS = 1e-6
NEG = -1e30

A_HEADS = 8
A_KV_HEADS = 2
A_HEAD_DIM = 64
A_BLOCK = 128
A_Q_DIM = A_HEADS * A_HEAD_DIM
A_KV_DIM = A_KV_HEADS * A_HEAD_DIM
T5_BUCKETS = 32
T5_MAX_DIST = 128

SSD_HEADS = 16
SSD_HEAD_DIM = 64
SSD_INNER = SSD_HEADS * SSD_HEAD_DIM
SSD_GROUPS = 2
SSD_STATE = 64
SSD_CONV = 5
SSD_XBC = SSD_INNER + 2 * SSD_GROUPS * SSD_STATE
SSD_L = 128
HALO = 8

HG_HEADS = 8
HG_K = 128
HG_V = 128
HG_C = 128
HG_SUB = 32
HG_NB = HG_C // HG_SUB

D_FF = 2816
FF_CHUNK = 1408

TM = 512
TS = 512
VMEM_LIMIT_BYTES = 56 * 1024 * 1024


def _cparams(*sem):
    return pltpu.CompilerParams(dimension_semantics=sem, vmem_limit_bytes=VMEM_LIMIT_BYTES)


def _const_spec(shape):
    nd = len(shape)
    return pl.BlockSpec(shape, lambda *_: (0,) * nd, pipeline_mode=pl.Buffered(1))


def _rms(x, g):
    return x * lax.rsqrt(jnp.mean(x * x, axis=-1, keepdims=True) + EPS) * g


def _silu(x):
    return x * (1.0 / (1.0 + jnp.exp(-x)))


def _dot(a, b):
    return jnp.dot(a, b, preferred_element_type=F32)


def _dot_nt(a, b):
    return lax.dot_general(a, b, (((1,), (1,)), ((), ())), preferred_element_type=F32)


def _dot_tn(a, b):
    return lax.dot_general(a, b, (((0,), (0,)), ((), ())), preferred_element_type=F32)


def _split_hi_lo(x):
    hi = x.astype(BF16)
    lo = (x - hi.astype(F32)).astype(BF16)
    return hi, lo


def _in_even_body(x_ref, g_ref, wqkv_ref, wz_ref, wxbc_ref, wdt_ref, wdtT_ref,
                  qkv_ref, z_ref, xbc_ref, dt_ref, dtT_ref):
    xn = _rms(x_ref[...], g_ref[...]).astype(BF16)
    qkv_ref[...] = _dot(xn, wqkv_ref[...]).astype(BF16)
    z_ref[...] = _dot(xn, wz_ref[...])
    xbc_ref[...] = _dot(xn, wxbc_ref[...])
    dt_ref[...] = _dot(xn, wdt_ref[...])
    dtT_ref[...] = _dot_nt(wdtT_ref[...], xn)


def in_proj_even(x, g, wqkv, wz, wxbc, wdt, wdtT):
    T = x.shape[0]
    nh2 = 2 * SSD_HEADS
    row = lambda c: pl.BlockSpec((TM, c), lambda i: (i, 0))
    return pl.pallas_call(
        _in_even_body,
        grid=(T // TM,),
        in_specs=[row(D_MODEL), _const_spec((1, D_MODEL)), _const_spec(wqkv.shape), _const_spec(wz.shape),
                  _const_spec(wxbc.shape), _const_spec(wdt.shape), _const_spec(wdtT.shape)],
        out_specs=[row(A_Q_DIM + 2 * A_KV_DIM), row(SSD_INNER), row(SSD_XBC), row(nh2),
                   pl.BlockSpec((nh2, TM), lambda i: (0, i))],
        out_shape=[jax.ShapeDtypeStruct((T, A_Q_DIM + 2 * A_KV_DIM), BF16),
                   jax.ShapeDtypeStruct((T, SSD_INNER), F32),
                   jax.ShapeDtypeStruct((T, SSD_XBC), F32),
                   jax.ShapeDtypeStruct((T, nh2), F32),
                   jax.ShapeDtypeStruct((nh2, T), F32)],
        compiler_params=_cparams("parallel"),
    )(x, g, wqkv, wz, wxbc, wdt, wdtT)


def _in_odd_body(x_ref, g_ref, w_ref, o_ref):
    xn = _rms(x_ref[...], g_ref[...]).astype(BF16)
    n = w_ref.shape[1]
    for c in range(0, n, D_MODEL):
        o_ref[:, c:c + D_MODEL] = _dot(xn, w_ref[:, c:c + D_MODEL])


def in_proj_odd(x, g, w):
    T = x.shape[0]
    n = w.shape[1]
    return pl.pallas_call(
        _in_odd_body,
        grid=(T // TM,),
        in_specs=[pl.BlockSpec((TM, D_MODEL), lambda i: (i, 0)), _const_spec((1, D_MODEL)), _const_spec(w.shape)],
        out_specs=pl.BlockSpec((TM, n), lambda i: (i, 0)),
        out_shape=jax.ShapeDtypeStruct((T, n), F32),
        compiler_params=_cparams("parallel"),
    )(x, g, w)


def _out_ffn_body(n_mix, *refs):
    mix_refs = refs[:n_mix]
    wout_refs = refs[n_mix:2 * n_mix]
    x_ref, g_ref, wg_ref, wu_ref, wd_ref, o_ref = refs[2 * n_mix:]
    m = _dot(mix_refs[0][...], wout_refs[0][...])
    for a_ref, w_ref in zip(mix_refs[1:], wout_refs[1:]):
        m = m + _dot(a_ref[...], w_ref[...])
    x1 = x_ref[...] + _rms(m, g_ref[0:1, :])
    hn = _rms(x1, g_ref[1:2, :]).astype(BF16)
    f = None
    for c in range(0, D_FF, FF_CHUNK):
        gate = _dot(hn, wg_ref[:, c:c + FF_CHUNK])
        up = _dot(hn, wu_ref[:, c:c + FF_CHUNK])
        h = (_silu(gate) * up).astype(BF16)
        part = _dot(h, wd_ref[c:c + FF_CHUNK, :])
        f = part if f is None else f + part
    o_ref[...] = x1 + _rms(f, g_ref[2:3, :])


def out_proj_ffn(mixes, wouts, x, g3, wg, wu, wd):
    T = x.shape[0]
    n_mix = len(mixes)
    row = lambda c: pl.BlockSpec((TM, c), lambda i: (i, 0))
    in_specs = ([row(a.shape[1]) for a in mixes] + [_const_spec(w.shape) for w in wouts]
                + [row(D_MODEL), _const_spec(g3.shape), _const_spec(wg.shape), _const_spec(wu.shape),
                   _const_spec(wd.shape)])
    return pl.pallas_call(
        functools.partial(_out_ffn_body, n_mix),
        grid=(T // TM,),
        in_specs=in_specs,
        out_specs=row(D_MODEL),
        out_shape=jax.ShapeDtypeStruct((T, D_MODEL), F32),
        compiler_params=_cparams("parallel"),
    )(*mixes, *wouts, x, g3, wg, wu, wd)


def _attn_body(q_ref, kp_ref, kc_ref, kn_ref, vp_ref, vc_ref, vn_ref, bias_ref, sink_ref, o_ref):
    i = pl.program_id(1)
    n = pl.num_programs(1)
    nsub = TS // A_BLOCK
    kall = jnp.concatenate([kp_ref[...], kc_ref[...], kn_ref[...]], axis=0)
    vall = jnp.concatenate([vp_ref[...], vc_ref[...], vn_ref[...]], axis=0)
    col = lax.broadcasted_iota(jnp.int32, (1, 3 * A_BLOCK), 1)
    rep = A_HEADS // A_KV_HEADS
    for sb in range(nsub):
        r0 = sb * A_BLOCK
        kwin = kall[r0:r0 + 3 * A_BLOCK]
        vwin = vall[r0:r0 + 3 * A_BLOCK]
        edge = None
        if sb == 0:
            edge = jnp.where(col < jnp.where(i == 0, A_BLOCK, 0), NEG, 0.0)
        if sb == nsub - 1:
            e2 = jnp.where(col >= jnp.where(i == n - 1, 2 * A_BLOCK, 3 * A_BLOCK), NEG, 0.0)
            edge = e2 if edge is None else edge + e2
        outs = []
        for h in range(A_HEADS):
            g = h // rep
            qh = q_ref[r0:r0 + A_BLOCK, h * A_HEAD_DIM:(h + 1) * A_HEAD_DIM]
            s = _dot_nt(qh, kwin[:, g * A_HEAD_DIM:(g + 1) * A_HEAD_DIM]) * (A_HEAD_DIM ** -0.5) + bias_ref[h]
            if edge is not None:
                s = s + edge
            sk = sink_ref[0:1, h:h + 1]
            m = jnp.maximum(jnp.max(s, axis=-1, keepdims=True), sk)
            p = jnp.exp(s - m)
            denom = jnp.sum(p, axis=-1, keepdims=True) + jnp.exp(sk - m)
            o = _dot(p.astype(BF16), vwin[:, g * A_HEAD_DIM:(g + 1) * A_HEAD_DIM])
            outs.append(o / denom)
        o_ref[r0:r0 + A_BLOCK, :] = jnp.concatenate(outs, axis=1).astype(o_ref.dtype)


def windowed_attention(qkv, bias, sink, B, S):
    T = B * S
    nS = S // TS
    sub = TS // A_BLOCK
    nblk = T // A_BLOCK
    kcol = A_Q_DIM // A_KV_DIM
    vcol = kcol + 1
    cur = lambda c: pl.BlockSpec((TS, A_KV_DIM), lambda b, i: (b * nS + i, c))
    prev = lambda c: pl.BlockSpec((A_BLOCK, A_KV_DIM),
                                  lambda b, i: (jnp.maximum((b * nS + i) * sub - 1, 0), c))
    nxt = lambda c: pl.BlockSpec((A_BLOCK, A_KV_DIM),
                                 lambda b, i: (jnp.minimum((b * nS + i + 1) * sub, nblk - 1), c))
    return pl.pallas_call(
        _attn_body,
        grid=(B, nS),
        in_specs=[pl.BlockSpec((TS, A_Q_DIM), lambda b, i: (b * nS + i, 0)),
                  prev(kcol), cur(kcol), nxt(kcol), prev(vcol), cur(vcol), nxt(vcol),
                  _const_spec(bias.shape), _const_spec(sink.shape)],
        out_specs=pl.BlockSpec((TS, A_Q_DIM), lambda b, i: (b * nS + i, 0)),
        out_shape=jax.ShapeDtypeStruct((T, A_Q_DIM), BF16),
        compiler_params=_cparams("parallel", "parallel"),
    )(qkv, qkv, qkv, qkv, qkv, qkv, qkv, bias, sink)


def _ssd_body(backward, *refs):
    if backward:
        (xc_ref, xp_ref, xn_ref, dt_ref, dtT_ref, cw_ref, cb_ref, alog_ref, alogT_ref, dtb_ref, dtbT_ref,
         y_ref, xpad, xact, hst) = refs
    else:
        (xc_ref, xp_ref, xn_ref, dt_ref, dtT_ref, cw_ref, cb_ref, alog_ref, alogT_ref, dtb_ref, dtbT_ref,
         yb_ref, z_ref, dsk_ref, nw_ref, y_ref, xpad, xact, hst) = refs
    i = pl.program_id(1)
    n = pl.num_programs(1)
    L = SSD_L
    nchunk = TS // L
    P = SSD_HEAD_DIM
    d0 = SSD_HEADS if backward else 0
    pad = SSD_CONV // 2

    @pl.when(i == 0)
    def _():
        hst[...] = jnp.zeros_like(hst)

    t = (n - 1 - i) if backward else i
    xpad[0:HALO, :] = jnp.where(t == 0, 0.0, xp_ref[...])
    xpad[HALO:HALO + TS, :] = xc_ref[...]
    xpad[HALO + TS:2 * HALO + TS, :] = jnp.where(t == n - 1, 0.0, xn_ref[...])
    acc = cb_ref[...] + cw_ref[0:1, :] * xpad[HALO - pad:HALO - pad + TS, :]
    for j in range(1, SSD_CONV):
        acc = acc + cw_ref[j:j + 1, :] * xpad[HALO - pad + j:HALO - pad + j + TS, :]
    xact[...] = _silu(acc)

    a_row = -jnp.exp(alog_ref[...])
    a_col = -jnp.exp(alogT_ref[...])
    r_i = lax.broadcasted_iota(jnp.int32, (L, L), 0)
    c_i = lax.broadcasted_iota(jnp.int32, (L, L), 1)
    keep = (c_i >= r_i) if backward else (c_i <= r_i)
    tri = jnp.where(keep, 1.0, 0.0).astype(BF16)
    triT = jnp.where(c_i <= r_i if backward else c_i >= r_i, 1.0, 0.0).astype(BF16)
    last = 0 if backward else L - 1

    def chunk(ci, carry):
        c = (nchunk - 1 - ci) if backward else ci
        r0 = pl.multiple_of(c * L, L)
        dt = jax.nn.softplus(dt_ref[pl.ds(r0, L), :] + dtb_ref[...])
        dtT = jax.nn.softplus(dtT_ref[:, pl.ds(r0, L)] + dtbT_ref[...])
        w_hi, w_lo = _split_hi_lo(dt * a_row)
        cs = _dot(tri, w_hi) + _dot(tri, w_lo)
        wT_hi, wT_lo = _split_hi_lo(dtT * a_col)
        csT = _dot(wT_hi, triT) + _dot(wT_lo, triT)
        ecs = jnp.exp(cs)
        tot = cs[last:last + 1, :]
        etot = jnp.exp(tot)
        scale = dt * jnp.exp(tot - cs)
        xs = xact[pl.ds(r0, L), 0:SSD_INNER]
        bm = xact[pl.ds(r0, L), SSD_INNER:SSD_INNER + SSD_GROUPS * SSD_STATE].astype(BF16)
        cm = xact[pl.ds(r0, L), SSD_INNER + SSD_GROUPS * SSD_STATE:SSD_XBC].astype(BF16)
        hpg = SSD_HEADS // SSD_GROUPS
        ys = []
        for g in range(SSD_GROUPS):
            bg = bm[:, g * SSD_STATE:(g + 1) * SSD_STATE]
            cg = cm[:, g * SSD_STATE:(g + 1) * SSD_STATE]
            cb = _dot_nt(cg, bg)
            for hh in range(hpg):
                h = g * hpg + hh
                k = d0 + h
                xh = xs[:, h * P:(h + 1) * P]
                seg = cs[:, k:k + 1] - csT[k:k + 1, :]
                dm = jnp.exp(jnp.where(keep, seg, -jnp.inf)) * dtT[k:k + 1, :]
                y = _dot((cb * dm).astype(BF16), xh.astype(BF16))
                hprev = hst[h]
                y = y + _dot(cg, hprev.astype(BF16)) * ecs[:, k:k + 1]
                xw = (xh * scale[:, k:k + 1]).astype(BF16)
                hst[h] = etot[:, k:k + 1] * hprev + _dot_tn(bg, xw)
                ys.append(y)
        y = jnp.concatenate(ys, axis=1)
        if backward:
            y_ref[pl.ds(r0, L), :] = y
        else:
            y = y + yb_ref[pl.ds(r0, L), :] + dsk_ref[...] * xs
            y = y * _silu(z_ref[pl.ds(r0, L), :])
            gw = SSD_INNER // SSD_GROUPS
            outs = []
            for g in range(SSD_GROUPS):
                yg = y[:, g * gw:(g + 1) * gw]
                outs.append(yg * lax.rsqrt(jnp.mean(yg * yg, axis=-1, keepdims=True) + EPS))
            y_ref[pl.ds(r0, L), :] = (jnp.concatenate(outs, axis=1) * nw_ref[...]).astype(y_ref.dtype)
        return carry

    lax.fori_loop(0, nchunk, chunk, 0)


def ssd_pass(backward, xbc, dt, dtT, cw, cb, alog, alogT, dtb, dtbT, B, S, extra=()):
    T = B * S
    nS = S // TS
    per = TS // HALO
    nh2 = 2 * SSD_HEADS

    def blk(b, i):
        return b * nS + ((nS - 1 - i) if backward else i)

    row = lambda c: pl.BlockSpec((TS, c), lambda b, i: (blk(b, i), 0))
    in_specs = [row(SSD_XBC),
                pl.BlockSpec((HALO, SSD_XBC), lambda b, i: (jnp.maximum(blk(b, i) * per - 1, 0), 0)),
                pl.BlockSpec((HALO, SSD_XBC), lambda b, i: (jnp.minimum((blk(b, i) + 1) * per, T // HALO - 1), 0)),
                row(nh2),
                pl.BlockSpec((nh2, TS), lambda b, i: (0, blk(b, i))),
                _const_spec(cw.shape), _const_spec(cb.shape), _const_spec(alog.shape), _const_spec(alogT.shape),
                _const_spec(dtb.shape), _const_spec(dtbT.shape)]
    args = [xbc, xbc, xbc, dt, dtT, cw, cb, alog, alogT, dtb, dtbT]
    if not backward:
        yb, z, dsk, nw = extra
        in_specs += [row(SSD_INNER), row(SSD_INNER), _const_spec(dsk.shape), _const_spec(nw.shape)]
        args += [yb, z, dsk, nw]
    return pl.pallas_call(
        functools.partial(_ssd_body, backward),
        grid=(B, nS),
        in_specs=in_specs,
        out_specs=row(SSD_INNER),
        out_shape=jax.ShapeDtypeStruct((T, SSD_INNER), F32 if backward else BF16),
        scratch_shapes=[pltpu.VMEM((TS + 2 * HALO, SSD_XBC), F32), pltpu.VMEM((TS, SSD_XBC), F32),
                        pltpu.VMEM((SSD_HEADS, SSD_STATE, SSD_HEAD_DIM), F32)],
        compiler_params=_cparams("parallel", "arbitrary"),
    )(*args)


def _hgrn_constants(backward):
    C, NB = HG_C, HG_NB
    t = np.arange(C)[:, None]
    u = np.arange(C)[None, :]
    if backward:
        t, u = C - 1 - t, C - 1 - u
    bt, bu = t // HG_SUB, u // HG_SUB
    mats = [(bt == bu) & (u <= t),
            (bt == bu) & (u > t),
            bu < bt,
            bu > bt]
    for d in range(2, NB):
        mats.append((bu < bt) & (bu >= bt - (d - 1)))
    masks = [(bt == bu) & (u <= t)] + [bt - bu == d for d in range(1, NB)]
    return (np.concatenate(mats, axis=0).astype(np.float32), np.stack(masks).astype(np.float32))


def _hgrn_body(backward, *refs):
    if backward:
        (q_ref, f_ref, v_ref, lb_ref, mst_ref, msk_ref, o_ref, st) = refs
    else:
        (q_ref, f_ref, v_ref, lb_ref, mst_ref, msk_ref, ob_ref, g_ref, nw_ref, o_ref, st) = refs
    i = pl.program_id(1)
    C, NB = HG_C, HG_NB
    nchunk = TS // C
    last = 0 if backward else C - 1

    @pl.when(i == 0)
    def _():
        st[...] = jnp.zeros_like(st)

    def chunk(ci, carry):
        c = (nchunk - 1 - ci) if backward else ci
        r0 = pl.multiple_of(c * C, C)
        mstack = mst_ref[...]
        for h in range(HG_HEADS):
            hs = slice(h * HG_K, (h + 1) * HG_K)
            lb = lb_ref[:, hs]
            fr = f_ref[pl.ds(r0, C), hs]
            e = jnp.exp(-jnp.abs(fr))
            r = 1.0 / (1.0 + e)
            sg = jnp.where(fr >= 0, r, e * r)
            sgn = jnp.where(fr >= 0, e * r, r)
            logf = jnp.log(lb + (1.0 - lb) * sg)
            k = (1.0 - lb) * sgn
            q = _silu(q_ref[pl.ds(r0, C), hs])
            v = v_ref[pl.ds(r0, C), hs].astype(BF16)
            hi, lo = _split_hi_lo(logf)
            sums = _dot(mstack, jnp.concatenate([hi, lo], axis=1))
            sums = sums[:, :HG_K] + sums[:, HG_K:]
            gl, hl, gpre, gpost = (sums[j * C:(j + 1) * C] for j in range(4))
            qt = q * jnp.exp(gl)
            kt = (k * jnp.exp(-gl)).astype(BF16)
            kh = k * jnp.exp(hl)
            qin = (qt * jnp.exp(gpre)).astype(BF16)
            kst = (kh * jnp.exp(gpost)).astype(BF16)
            kh = kh.astype(BF16)
            qs = [qt.astype(BF16)] + [(qt * jnp.exp(sums[(2 + d) * C:(3 + d) * C])).astype(BF16)
                                      for d in range(2, NB)]
            att = _dot_nt(qs[0], kt) * msk_ref[0]
            off = _dot_nt(jnp.concatenate(qs, axis=0), kh)
            for d in range(1, NB):
                att = att + off[(d - 1) * C:d * C] * msk_ref[d]
            sT = st[h]
            o = _dot(att.astype(BF16), v) + _dot_nt(qin, sT.astype(BF16))
            tot = gl[last:last + 1, :] + gpre[last:last + 1, :]
            st[h] = sT * jnp.exp(tot) + _dot_tn(v, kst)
            if backward:
                o_ref[pl.ds(r0, C), hs] = o
            else:
                o = o + ob_ref[pl.ds(r0, C), hs]
                o = o * lax.rsqrt(jnp.mean(o * o, axis=-1, keepdims=True) + EPS) * nw_ref[...]
                o_ref[pl.ds(r0, C), hs] = (o * _silu(g_ref[pl.ds(r0, C), hs])).astype(o_ref.dtype)
        return carry

    lax.fori_loop(0, nchunk, chunk, 0)


def hgrn_pass(backward, proj, lb, B, S, extra=()):
    T = B * S
    nS = S // TS
    mstack, masks = _hgrn_constants(backward)
    mstack = jnp.asarray(mstack, BF16)
    masks = jnp.asarray(masks, F32)

    def blk(b, i):
        return b * nS + ((nS - 1 - i) if backward else i)

    col = lambda c: pl.BlockSpec((TS, D_MODEL), lambda b, i: (blk(b, i), c))
    in_specs = [col(0), col(2 if backward else 1), col(3), _const_spec(lb.shape), _const_spec(mstack.shape),
                _const_spec(masks.shape)]
    args = [proj, proj, proj, lb, mstack, masks]
    if not backward:
        ob, nw = extra
        in_specs += [col(0), col(4), _const_spec(nw.shape)]
        args += [ob, proj, nw]
    return pl.pallas_call(
        functools.partial(_hgrn_body, backward),
        grid=(B, nS),
        in_specs=in_specs,
        out_specs=pl.BlockSpec((TS, D_MODEL), lambda b, i: (blk(b, i), 0)),
        out_shape=jax.ShapeDtypeStruct((T, D_MODEL), F32 if backward else BF16),
        scratch_shapes=[pltpu.VMEM((HG_HEADS, HG_V, HG_K), F32)],
        compiler_params=_cparams("parallel", "arbitrary"),
    )(*args)


def _t5_bucket(rel):
    half = T5_BUCKETS // 2
    max_exact = half // 2
    n = np.abs(rel)
    large = max_exact + (np.log(np.maximum(n, 1) / max_exact)
                         / np.log(T5_MAX_DIST / max_exact) * (half - max_exact)).astype(np.int32)
    large = np.minimum(large, half - 1)
    return ((rel > 0).astype(np.int32) * half + np.where(n < max_exact, n, large)).astype(np.int32)


def _attention_bias(t5_bias):
    qi = np.arange(A_BLOCK)[:, None]
    kj = np.arange(3 * A_BLOCK)[None, :] - A_BLOCK
    rel = kj - qi
    bias = jnp.transpose(t5_bias[_t5_bucket(rel)], (2, 0, 1)).astype(F32)
    return jnp.where(np.abs(rel)[None] <= A_BLOCK, bias, NEG)


def _trunk(x, B, S, p):
    for l in range(DEPTH):
        j = l // 2
        g = p["gains"][l]
        if l % 2 == 0:
            e = p["even"][j]
            qkv, z, xbc, dt, dtT = in_proj_even(x, g[0:1], e["wqkv"], e["wz"], e["wxbc"], e["wdt"], e["wdtT"])
            a_out = windowed_attention(qkv, p["attn_bias"], e["sink"], B, S)
            common = (xbc, dt, dtT, e["conv_w"], e["conv_b"], e["a_log"], e["a_logT"], e["dt_bias"], e["dt_biasT"])
            yb = ssd_pass(True, *common, B, S)
            b_out = ssd_pass(False, *common, B, S, extra=(yb, z, e["d_skip"], e["norm_w"]))
            mixes, wouts = [a_out, b_out], [e["wout_a"], e["wout_b"]]
        else:
            o = p["odd"][j]
            proj = in_proj_odd(x, g[0:1], o["w_in"])
            ob = hgrn_pass(True, proj, o["lb"], B, S)
            og = hgrn_pass(False, proj, o["lb"], B, S, extra=(ob, o["norm_w"]))
            mixes, wouts = [og], [o["w_out"]]
        f = p["ffn"][l]
        x = out_proj_ffn(mixes, wouts, x, g[1:4], f["wg"], f["wu"], f["wd"])
    return x


def kernel(x_prompt, x_sample, norm_gains, t5_bias, ev_w_in, attn_sink, ssd_conv_w, ssd_conv_b, ssd_a_log,
           ssd_dt_bias, ssd_d, ssd_norm_w, ev_w_out, od_w_in, hg_lower_bounds, hg_norm_w, od_w_out,
           ffn_w_gate, ffn_w_up, ffn_w_down):
    nh2 = 2 * SSD_HEADS
    c_qkv = A_Q_DIM + 2 * A_KV_DIM
    c_z = c_qkv + SSD_INNER
    c_xbc = c_z + SSD_XBC
    lb_soft = jax.nn.softmax(hg_lower_bounds.astype(F32), axis=0)
    lb_all = jnp.cumsum(lb_soft, axis=0) - lb_soft[0]
    p = {"gains": norm_gains.astype(F32), "attn_bias": _attention_bias(t5_bias), "even": [], "odd": [], "ffn": []}
    for j in range(ev_w_in.shape[0]):
        w = ev_w_in[j]
        p["even"].append(dict(
            wqkv=w[:, :c_qkv].astype(BF16), wz=w[:, c_qkv:c_z].astype(BF16), wxbc=w[:, c_z:c_xbc].astype(BF16),
            wdt=w[:, c_xbc:].astype(BF16), wdtT=w[:, c_xbc:].T.astype(BF16),
            sink=attn_sink[j].astype(F32).reshape(1, A_HEADS),
            conv_w=ssd_conv_w[j].astype(F32), conv_b=ssd_conv_b[j].astype(F32).reshape(1, SSD_XBC),
            a_log=ssd_a_log[j].astype(F32).reshape(1, nh2), a_logT=ssd_a_log[j].astype(F32).reshape(nh2, 1),
            dt_bias=ssd_dt_bias[j].astype(F32).reshape(1, nh2), dt_biasT=ssd_dt_bias[j].astype(F32).reshape(nh2, 1),
            d_skip=jnp.repeat(ssd_d[j].astype(F32), SSD_HEAD_DIM).reshape(1, SSD_INNER),
            norm_w=ssd_norm_w[j].astype(F32).reshape(1, SSD_INNER),
            wout_a=ev_w_out[j, :A_Q_DIM].astype(BF16), wout_b=ev_w_out[j, A_Q_DIM:].astype(BF16)))
    for j in range(od_w_in.shape[0]):
        p["odd"].append(dict(
            w_in=od_w_in[j].astype(BF16), lb=lb_all[j].reshape(1, HG_HEADS * HG_K),
            norm_w=hg_norm_w[j].astype(F32).reshape(1, HG_V), w_out=od_w_out[j].astype(BF16)))
    for l in range(DEPTH):
        p["ffn"].append(dict(wg=ffn_w_gate[l].astype(BF16), wu=ffn_w_up[l].astype(BF16),
                             wd=ffn_w_down[l].astype(BF16)))
    outs = []
    for x in (x_prompt, x_sample):
        B, S, _ = x.shape
        outs.append(_trunk(x.reshape(B * S, D_MODEL), B, S, p).reshape(B, S, D_MODEL))
    return tuple(outs)
```

```python
import functools
import math

import numpy as np
import jax
import jax.numpy as jnp
from jax import lax
from jax.experimental import pallas as pl
from jax.experimental.pallas import tpu as pltpu

F32 = jnp.float32
BF16 = jnp.bfloat16

D_MODEL = 1024
DEPTH = 4
EPS = 1e-6
NEG = -1e30

A_HEADS = 8
A_KV_HEADS = 2
A_HEAD_DIM = 64
A_BLOCK = 128
A_Q_DIM = A_HEADS * A_HEAD_DIM
A_KV_DIM = A_KV_HEADS * A_HEAD_DIM
T5_BUCKETS = 32
T5_MAX_DIST = 128

SSD_HEADS = 16
SSD_HEAD_DIM = 64
SSD_INNER = SSD_HEADS * SSD_HEAD_DIM
SSD_GROUPS = 2
SSD_STATE = 64
SSD_CONV = 5
SSD_XBC = SSD_INNER + 2 * SSD_GROUPS * SSD_STATE
SSD_L = 128
HALO = 8

HG_HEADS = 8
HG_K = 128
HG_V = 128
HG_C = 128
HG_SUB = 32
HG_NB = HG_C // HG_SUB

D_FF = 2816
FF_CHUNK = 1408

TM = 512
TS = 512
VMEM_LIMIT_BYTES = 56 * 1024 * 1024


def _cparams(*sem):
    return pltpu.CompilerParams(dimension_semantics=sem, vmem_limit_bytes=VMEM_LIMIT_BYTES)


def _const_spec(shape):
    nd = len(shape)
    return pl.BlockSpec(shape, lambda *_: (0,) * nd, pipeline_mode=pl.Buffered(1))


def _rms(x, g):
    return x * lax.rsqrt(jnp.mean(x * x, axis=-1, keepdims=True) + EPS) * g


def _silu(x):
    return x * (1.0 / (1.0 + jnp.exp(-x)))


def _dot(a, b):
    return jnp.dot(a, b, preferred_element_type=F32)


def _dot_nt(a, b):
    return lax.dot_general(a, b, (((1,), (1,)), ((), ())), preferred_element_type=F32)


def _dot_tn(a, b):
    return lax.dot_general(a, b, (((0,), (0,)), ((), ())), preferred_element_type=F32)


def _split_hi_lo(x):
    hi = x.astype(BF16)
    lo = (x - hi.astype(F32)).astype(BF16)
    return hi, lo


def _in_even_body(x_ref, g_ref, wqkv_ref, wz_ref, wxbc_ref, wdt_ref, wdtT_ref,
                  qkv_ref, z_ref, xbc_ref, dt_ref, dtT_ref):
    xn = _rms(x_ref[...], g_ref[...]).astype(BF16)
    qkv_ref[...] = _dot(xn, wqkv_ref[...]).astype(BF16)
    z_ref[...] = _dot(xn, wz_ref[...])
    xbc_ref[...] = _dot(xn, wxbc_ref[...])
    dt_ref[...] = _dot(xn, wdt_ref[...])
    dtT_ref[...] = _dot_nt(wdtT_ref[...], xn)


def in_proj_even(x, g, wqkv, wz, wxbc, wdt, wdtT):
    T = x.shape[0]
    nh2 = 2 * SSD_HEADS
    row = lambda c: pl.BlockSpec((TM, c), lambda i: (i, 0))
    return pl.pallas_call(
        _in_even_body,
        grid=(T // TM,),
        in_specs=[row(D_MODEL), _const_spec((1, D_MODEL)), _const_spec(wqkv.shape), _const_spec(wz.shape),
                  _const_spec(wxbc.shape), _const_spec(wdt.shape), _const_spec(wdtT.shape)],
        out_specs=[row(A_Q_DIM + 2 * A_KV_DIM), row(SSD_INNER), row(SSD_XBC), row(nh2),
                   pl.BlockSpec((nh2, TM), lambda i: (0, i))],
        out_shape=[jax.ShapeDtypeStruct((T, A_Q_DIM + 2 * A_KV_DIM), BF16),
                   jax.ShapeDtypeStruct((T, SSD_INNER), F32),
                   jax.ShapeDtypeStruct((T, SSD_XBC), F32),
                   jax.ShapeDtypeStruct((T, nh2), F32),
                   jax.ShapeDtypeStruct((nh2, T), F32)],
        compiler_params=_cparams("parallel"),
        name="in_proj_even",
    )(x, g, wqkv, wz, wxbc, wdt, wdtT)


def _in_odd_body(x_ref, g_ref, w_ref, o_ref):
    xn = _rms(x_ref[...], g_ref[...]).astype(BF16)
    n = w_ref.shape[1]
    for c in range(0, n, D_MODEL):
        o_ref[:, c:c + D_MODEL] = _dot(xn, w_ref[:, c:c + D_MODEL])


def in_proj_odd(x, g, w):
    T = x.shape[0]
    n = w.shape[1]
    return pl.pallas_call(
        _in_odd_body,
        grid=(T // TM,),
        in_specs=[pl.BlockSpec((TM, D_MODEL), lambda i: (i, 0)), _const_spec((1, D_MODEL)), _const_spec(w.shape)],
        out_specs=pl.BlockSpec((TM, n), lambda i: (i, 0)),
        out_shape=jax.ShapeDtypeStruct((T, n), F32),
        compiler_params=_cparams("parallel"),
        name="in_proj_odd",
    )(x, g, w)


def _out_ffn_body(n_mix, *refs):
    mix_refs = refs[:n_mix]
    wout_refs = refs[n_mix:2 * n_mix]
    x_ref, g_ref, wg_ref, wu_ref, wd_ref, o_ref = refs[2 * n_mix:]
    m = _dot(mix_refs[0][...], wout_refs[0][...])
    for a_ref, w_ref in zip(mix_refs[1:], wout_refs[1:]):
        m = m + _dot(a_ref[...], w_ref[...])
    x1 = x_ref[...] + _rms(m, g_ref[0:1, :])
    hn = _rms(x1, g_ref[1:2, :]).astype(BF16)
    f = None
    for c in range(0, D_FF, FF_CHUNK):
        gate = _dot(hn, wg_ref[:, c:c + FF_CHUNK])
        up = _dot(hn, wu_ref[:, c:c + FF_CHUNK])
        h = (_silu(gate) * up).astype(BF16)
        part = _dot(h, wd_ref[c:c + FF_CHUNK, :])
        f = part if f is None else f + part
    o_ref[...] = x1 + _rms(f, g_ref[2:3, :])


def out_proj_ffn(mixes, wouts, x, g3, wg, wu, wd):
    T = x.shape[0]
    n_mix = len(mixes)
    row = lambda c: pl.BlockSpec((TM, c), lambda i: (i, 0))
    in_specs = ([row(a.shape[1]) for a in mixes] + [_const_spec(w.shape) for w in wouts]
                + [row(D_MODEL), _const_spec(g3.shape), _const_spec(wg.shape), _const_spec(wu.shape),
                   _const_spec(wd.shape)])
    return pl.pallas_call(
        functools.partial(_out_ffn_body, n_mix),
        grid=(T // TM,),
        in_specs=in_specs,
        out_specs=row(D_MODEL),
        out_shape=jax.ShapeDtypeStruct((T, D_MODEL), F32),
        compiler_params=_cparams("parallel"),
        name="out_proj_ffn",
    )(*mixes, *wouts, x, g3, wg, wu, wd)


def _attn_body(q_ref, kp_ref, kc_ref, kn_ref, vp_ref, vc_ref, vn_ref, bias_ref, sink_ref, o_ref):
    i = pl.program_id(1)
    n = pl.num_programs(1)
    nsub = TS // A_BLOCK
    dh = A_HEAD_DIM
    kall = jnp.concatenate([kp_ref[...], kc_ref[...], kn_ref[...]], axis=0)
    vall = jnp.concatenate([vp_ref[...], vc_ref[...], vn_ref[...]], axis=0)
    lane = lax.broadcasted_iota(jnp.int32, (1, 2 * dh), 1)
    ones_col = jnp.where(lane == dh, 1.0, 0.0).astype(BF16)
    vext = []
    for g in range(A_KV_HEADS):
        vg = vall[:, g * dh:(g + 1) * dh]
        vext.append(jnp.concatenate([vg, jnp.zeros_like(vg)], axis=1) + ones_col)
    col = lax.broadcasted_iota(jnp.int32, (1, 3 * A_BLOCK), 1)
    rep = A_HEADS // A_KV_HEADS
    heads = range(A_HEADS)
    for sb in range(nsub):
        r0 = sb * A_BLOCK
        edge = None
        if sb == 0:
            edge = jnp.where(col < jnp.where(i == 0, A_BLOCK, 0), NEG, 0.0)
        if sb == nsub - 1:
            e2 = jnp.where(col >= jnp.where(i == n - 1, 2 * A_BLOCK, 3 * A_BLOCK), NEG, 0.0)
            edge = e2 if edge is None else edge + e2
        ss = [_dot_nt(q_ref[r0:r0 + A_BLOCK, h * dh:(h + 1) * dh],
                      kall[r0:r0 + 3 * A_BLOCK, (h // rep) * dh:(h // rep + 1) * dh]) for h in heads]
        ps, ms = [], []
        for h in heads:
            s = ss[h] * (dh ** -0.5) + bias_ref[h]
            if edge is not None:
                s = s + edge
            m = jnp.maximum(jnp.max(s, axis=-1, keepdims=True), sink_ref[0:1, h:h + 1])
            ps.append(jnp.exp(s - m).astype(BF16))
            ms.append(m)
        pv = [_dot(ps[h], vext[h // rep][r0:r0 + 3 * A_BLOCK]) for h in heads]
        outs = []
        for h in heads:
            denom = pv[h][:, dh:dh + 1] + jnp.exp(sink_ref[0:1, h:h + 1] - ms[h])
            outs.append(pv[h][:, :dh] / denom)
        o_ref[r0:r0 + A_BLOCK, :] = jnp.concatenate(outs, axis=1).astype(o_ref.dtype)


def windowed_attention(qkv, bias, sink, B, S):
    T = B * S
    nS = S // TS
    sub = TS // A_BLOCK
    nblk = T // A_BLOCK
    kcol = A_Q_DIM // A_KV_DIM
    vcol = kcol + 1
    cur = lambda c: pl.BlockSpec((TS, A_KV_DIM), lambda b, i: (b * nS + i, c))
    prev = lambda c: pl.BlockSpec((A_BLOCK, A_KV_DIM),
                                  lambda b, i: (jnp.maximum((b * nS + i) * sub - 1, 0), c))
    nxt = lambda c: pl.BlockSpec((A_BLOCK, A_KV_DIM),
                                 lambda b, i: (jnp.minimum((b * nS + i + 1) * sub, nblk - 1), c))
    return pl.pallas_call(
        _attn_body,
        grid=(B, nS),
        in_specs=[pl.BlockSpec((TS, A_Q_DIM), lambda b, i: (b * nS + i, 0)),
                  prev(kcol), cur(kcol), nxt(kcol), prev(vcol), cur(vcol), nxt(vcol),
                  _const_spec(bias.shape), _const_spec(sink.shape)],
        out_specs=pl.BlockSpec((TS, A_Q_DIM), lambda b, i: (b * nS + i, 0)),
        out_shape=jax.ShapeDtypeStruct((T, A_Q_DIM), BF16),
        compiler_params=_cparams("parallel", "parallel"),
        name="windowed_attention",
    )(qkv, qkv, qkv, qkv, qkv, qkv, qkv, bias, sink)


def _split3_tiled(x):
    h = x.shape[1]
    t = jnp.concatenate([x, x, x], axis=1)
    hi = t.astype(BF16).astype(F32)
    r1 = t - hi
    mid = r1.astype(BF16).astype(F32)
    lane = lax.broadcasted_iota(jnp.int32, t.shape, 1)
    return jnp.where(lane < h, hi, jnp.where(lane < 2 * h, mid, r1 - mid)).astype(BF16)


def _ssd_constants():
    H, P, L = SSD_HEADS, SSD_HEAD_DIM, SSD_L
    selb = np.zeros((3 * H, H * L), np.float32)
    selx = np.zeros((3 * H, H * P), np.float32)
    for k in range(H):
        for part in range(3):
            selb[part * H + k, k * L:(k + 1) * L] = 1.0
            selx[part * H + k, k * P:(k + 1) * P] = 1.0
    return selb, selx


def _ssd_body(backward, *refs):
    if backward:
        (xc_ref, xp_ref, xn_ref, dt_ref, dtT_ref, cw_ref, cb_ref, alog_ref, alogT_ref, dtb_ref, dtbT_ref,
         selb_ref, selx_ref, y_ref, xa_ref, xpad, hst) = refs
    else:
        (xa_ref, dt_ref, dtT_ref, alog_ref, alogT_ref, dtb_ref, dtbT_ref, selb_ref, selx_ref,
         yb_ref, z_ref, dsk_ref, nw_ref, y_ref, hst) = refs
    i = pl.program_id(1)
    n = pl.num_programs(1)
    L = SSD_L
    H = SSD_HEADS
    P = SSD_HEAD_DIM
    nchunk = TS // L
    d0 = H if backward else 0
    gw = SSD_INNER // SSD_GROUPS

    @pl.when(i == 0)
    def _():
        hst[...] = jnp.zeros_like(hst)

    if backward:
        t = n - 1 - i
        rows = TS + 2 * HALO
        xpad[0:HALO, :] = jnp.where(t == 0, 0.0, xp_ref[...])
        xpad[HALO:HALO + TS, :] = xc_ref[...]
        xpad[HALO + TS:rows, :] = jnp.where(t == n - 1, 0.0, xn_ref[...])
        xp = xpad[...]
        acc = cb_ref[...] + cw_ref[SSD_CONV // 2:SSD_CONV // 2 + 1, :] * xp[HALO:HALO + TS]
        for j in range(SSD_CONV):
            if j != SSD_CONV // 2:
                sh = pltpu.roll(xp, (SSD_CONV // 2 - j) % rows, 0)
                acc = acc + cw_ref[j:j + 1, :] * sh[HALO:HALO + TS]
        xa_ref[...] = _silu(acc).astype(xa_ref.dtype)

    a_row = -jnp.exp(alog_ref[...])[:, d0:d0 + H]
    a_col = -jnp.exp(alogT_ref[...])[d0:d0 + H, :]
    r_i = lax.broadcasted_iota(jnp.int32, (L, L), 0)
    c_i = lax.broadcasted_iota(jnp.int32, (L, L), 1)
    keep = (c_i >= r_i) if backward else (c_i <= r_i)
    tri = jnp.where(keep, 1.0, 0.0).astype(BF16)
    triT = jnp.where(c_i <= r_i if backward else c_i >= r_i, 1.0, 0.0).astype(BF16)
    last = 0 if backward else L - 1
    lane = lax.broadcasted_iota(jnp.int32, (1, 2 * P), 1)

    def chunk(ci, carry):
        c = (nchunk - 1 - ci) if backward else ci
        r0 = pl.multiple_of(c * L, L)
        dt = jax.nn.softplus(dt_ref[pl.ds(r0, L), :] + dtb_ref[...])[:, d0:d0 + H]
        dtT = jax.nn.softplus(dtT_ref[:, pl.ds(r0, L)] + dtbT_ref[...])[d0:d0 + H, :]
        w_hi, w_lo = _split_hi_lo(dt * a_row)
        cs = _dot(tri, w_hi) + _dot(tri, w_lo)
        wT_hi, wT_lo = _split_hi_lo(dtT * a_col)
        csT = _dot(wT_hi, triT) + _dot(wT_lo, triT)
        ecs = jnp.exp(cs)
        tot = cs[last:last + 1, :]
        scale = dt * jnp.exp(tot - cs)
        csb = _dot(_split3_tiled(cs), selb_ref[...])
        ex = _dot(_split3_tiled(ecs), selx_ref[...])
        sx = _dot(_split3_tiled(scale), selx_ref[...])
        etx = ex[last:last + 1, :]
        xs = xa_ref[pl.ds(r0, L), 0:SSD_INNER]
        bm = xa_ref[pl.ds(r0, L), SSD_INNER:SSD_INNER + SSD_GROUPS * SSD_STATE]
        cm = xa_ref[pl.ds(r0, L), SSD_INNER + SSD_GROUPS * SSD_STATE:SSD_XBC]
        xw = (xs.astype(F32) * sx).astype(BF16)
        cbs, yoffs = [], []
        for g in range(SSD_GROUPS):
            bg = bm[:, g * SSD_STATE:(g + 1) * SSD_STATE]
            cg = cm[:, g * SSD_STATE:(g + 1) * SSD_STATE]
            cbs.append(_dot_nt(cg, bg))
            hg = hst[g]
            yoffs.append(_dot(cg, hg.astype(BF16)) * ex[:, g * gw:(g + 1) * gw])
            hst[g] = etx[:, g * gw:(g + 1) * gw] * hg + _dot_tn(bg, xw[:, g * gw:(g + 1) * gw])
        ms = []
        for k in range(H):
            seg = csb[:, k * L:(k + 1) * L] - csT[k:k + 1, :]
            dm = jnp.exp(jnp.where(keep, seg, NEG)) * dtT[k:k + 1, :]
            ms.append((cbs[k // (H // SSD_GROUPS)] * dm).astype(BF16))
        ys = []
        for j in range(H // 2):
            xp2 = xs[:, 2 * j * P:(2 * j + 2) * P]
            zero = jnp.zeros_like(xp2)
            rhs = jnp.concatenate([jnp.where(lane < P, xp2, zero), jnp.where(lane >= P, xp2, zero)], axis=0)
            ys.append(_dot(jnp.concatenate([ms[2 * j], ms[2 * j + 1]], axis=1), rhs))
        y = jnp.concatenate(ys, axis=1) + jnp.concatenate(yoffs, axis=1)
        if backward:
            y_ref[pl.ds(r0, L), :] = y
        else:
            y = y + yb_ref[pl.ds(r0, L), :] + dsk_ref[...] * xs.astype(F32)
            y = y * _silu(z_ref[pl.ds(r0, L), :])
            outs = []
            for g in range(SSD_GROUPS):
                yg = y[:, g * gw:(g + 1) * gw]
                outs.append(yg * lax.rsqrt(jnp.mean(yg * yg, axis=-1, keepdims=True) + EPS))
            y_ref[pl.ds(r0, L), :] = (jnp.concatenate(outs, axis=1) * nw_ref[...]).astype(y_ref.dtype)
        return carry

    lax.fori_loop(0, nchunk, chunk, 0)


def ssd_pass(backward, src, dt, dtT, alog, alogT, dtb, dtbT, B, S, extra):
    T = B * S
    nS = S // TS
    per = TS // HALO
    nh2 = 2 * SSD_HEADS
    selb, selx = (jnp.asarray(m, BF16) for m in _ssd_constants())

    def blk(b, i):
        return b * nS + ((nS - 1 - i) if backward else i)

    row = lambda c: pl.BlockSpec((TS, c), lambda b, i: (blk(b, i), 0))
    dt_specs = [row(nh2), pl.BlockSpec((nh2, TS), lambda b, i: (0, blk(b, i)))]
    par_specs = [_const_spec(alog.shape), _const_spec(alogT.shape), _const_spec(dtb.shape), _const_spec(dtbT.shape),
                 _const_spec(selb.shape), _const_spec(selx.shape)]
    state = pltpu.VMEM((SSD_GROUPS, SSD_STATE, SSD_INNER // SSD_GROUPS), F32)
    if backward:
        cw, cb = extra
        in_specs = ([row(SSD_XBC),
                     pl.BlockSpec((HALO, SSD_XBC), lambda b, i: (jnp.maximum(blk(b, i) * per - 1, 0), 0)),
                     pl.BlockSpec((HALO, SSD_XBC),
                                  lambda b, i: (jnp.minimum((blk(b, i) + 1) * per, T // HALO - 1), 0))]
                    + dt_specs + [_const_spec(cw.shape), _const_spec(cb.shape)] + par_specs)
        args = [src, src, src, dt, dtT, cw, cb, alog, alogT, dtb, dtbT, selb, selx]
        out_specs = [row(SSD_INNER), row(SSD_XBC)]
        out_shape = [jax.ShapeDtypeStruct((T, SSD_INNER), F32), jax.ShapeDtypeStruct((T, SSD_XBC), BF16)]
        scratch = [pltpu.VMEM((TS + 2 * HALO, SSD_XBC), F32), state]
    else:
        yb, z, dsk, nw = extra
        in_specs = ([row(SSD_XBC)] + dt_specs + par_specs
                    + [row(SSD_INNER), row(SSD_INNER), _const_spec(dsk.shape), _const_spec(nw.shape)])
        args = [src, dt, dtT, alog, alogT, dtb, dtbT, selb, selx, yb, z, dsk, nw]
        out_specs = row(SSD_INNER)
        out_shape = jax.ShapeDtypeStruct((T, SSD_INNER), BF16)
        scratch = [state]
    return pl.pallas_call(
        functools.partial(_ssd_body, backward),
        grid=(B, nS),
        in_specs=in_specs,
        out_specs=out_specs,
        out_shape=out_shape,
        scratch_shapes=scratch,
        compiler_params=_cparams("parallel", "arbitrary"),
        name="ssd_bwd" if backward else "ssd_fwd",
    )(*args)


def _hgrn_constants(backward):
    C, NB = HG_C, HG_NB
    t = np.arange(C)[:, None]
    u = np.arange(C)[None, :]
    if backward:
        t, u = C - 1 - t, C - 1 - u
    bt, bu = t // HG_SUB, u // HG_SUB
    cum = (bt == bu) & (u <= t)
    masks = [cum] + [bt - bu == d for d in range(1, NB)]
    return cum.astype(np.float32), np.stack(masks).astype(np.float32)


def _rows_to_tile(rows):
    return jnp.concatenate([jnp.broadcast_to(r, (HG_SUB, r.shape[1])) for r in rows], axis=0)


def _hgrn_body(backward, *refs):
    if backward:
        (q_ref, f_ref, v_ref, lb_ref, cum_ref, msk_ref, o_ref, st) = refs
    else:
        (q_ref, f_ref, v_ref, lb_ref, cum_ref, msk_ref, ob_ref, g_ref, nw_ref, o_ref, st) = refs
    i = pl.program_id(1)
    C, NB = HG_C, HG_NB
    nchunk = TS // C
    order = list(range(NB - 1, -1, -1)) if backward else list(range(NB))
    end_row = 0 if backward else HG_SUB - 1

    @pl.when(i == 0)
    def _():
        st[...] = jnp.zeros_like(st)

    def chunk(ci, carry):
        c = (nchunk - 1 - ci) if backward else ci
        r0 = pl.multiple_of(c * C, C)
        cum = cum_ref[...]
        heads = range(HG_HEADS)
        hsl = [slice(h * HG_K, (h + 1) * HG_K) for h in heads]
        ks, qv, vv, hl = [], [], [], []
        for h in heads:
            lb = lb_ref[:, hsl[h]]
            fr = f_ref[pl.ds(r0, C), hsl[h]]
            e = jnp.exp(-jnp.abs(fr))
            r = 1.0 / (1.0 + e)
            sg = jnp.where(fr >= 0, r, e * r)
            sgn = jnp.where(fr >= 0, e * r, r)
            logf = jnp.log(lb + (1.0 - lb) * sg)
            ks.append((1.0 - lb) * sgn)
            qv.append(_silu(q_ref[pl.ds(r0, C), hsl[h]]))
            vv.append(v_ref[pl.ds(r0, C), hsl[h]].astype(BF16))
            hl.append(jnp.concatenate(_split_hi_lo(logf), axis=1))
        gls = [_dot(cum, hl[h]) for h in heads]
        qs_all, kt_all, kh_all, qin_all, kst_all, dec_all = [], [], [], [], [], []
        for h in heads:
            gl = gls[h][:, :HG_K] + gls[h][:, HG_K:]
            tot = [gl[j * HG_SUB + end_row:j * HG_SUB + end_row + 1, :] for j in range(NB)]
            zero = jnp.zeros_like(tot[0])
            pre, post, near = [None] * NB, [None] * NB, [[None] * NB for _ in range(NB)]
            for p, b in enumerate(order):
                pre[b] = sum((tot[order[pp]] for pp in range(p)), zero)
                post[b] = sum((tot[order[pp]] for pp in range(p + 1, NB)), zero)
                for d in range(2, NB):
                    near[d][b] = sum((tot[order[pp]] for pp in range(max(p - d + 1, 0), p)), zero)
            qt = qv[h] * jnp.exp(gl)
            kt = ks[h] * jnp.exp(-gl)
            kh = kt * _rows_to_tile([jnp.exp(x) for x in tot])
            qin_all.append((qt * _rows_to_tile([jnp.exp(x) for x in pre])).astype(BF16))
            kst_all.append((kh * _rows_to_tile([jnp.exp(x) for x in post])).astype(BF16))
            qs_all.append(jnp.concatenate(
                [qt.astype(BF16)] + [(qt * _rows_to_tile([jnp.exp(x) for x in near[d]])).astype(BF16)
                                     for d in range(2, NB)], axis=0))
            kt_all.append(kt.astype(BF16))
            kh_all.append(kh.astype(BF16))
            dec_all.append(jnp.exp(sum(tot, zero)))
        att0 = [_dot_nt(qs_all[h][:C], kt_all[h]) for h in heads]
        offs = [_dot_nt(qs_all[h], kh_all[h]) for h in heads]
        atts = []
        for h in heads:
            att = att0[h] * msk_ref[0]
            for d in range(1, NB):
                att = att + offs[h][(d - 1) * C:d * C] * msk_ref[d]
            atts.append(att.astype(BF16))
        sTs = [st[h] for h in heads]
        outs = [_dot(atts[h], vv[h]) + _dot_nt(qin_all[h], sTs[h].astype(BF16)) for h in heads]
        upd = [_dot_tn(vv[h], kst_all[h]) for h in heads]
        for h in heads:
            st[h] = sTs[h] * dec_all[h] + upd[h]
            o = outs[h]
            if backward:
                o_ref[pl.ds(r0, C), hsl[h]] = o
            else:
                o = o + ob_ref[pl.ds(r0, C), hsl[h]]
                o = o * lax.rsqrt(jnp.mean(o * o, axis=-1, keepdims=True) + EPS) * nw_ref[...]
                o_ref[pl.ds(r0, C), hsl[h]] = (o * _silu(g_ref[pl.ds(r0, C), hsl[h]])).astype(o_ref.dtype)
        return carry

    lax.fori_loop(0, nchunk, chunk, 0)


def hgrn_pass(backward, proj, lb, B, S, extra=()):
    T = B * S
    nS = S // TS
    cum, masks = _hgrn_constants(backward)
    cum = jnp.asarray(cum, BF16)
    masks = jnp.asarray(masks, F32)

    def blk(b, i):
        return b * nS + ((nS - 1 - i) if backward else i)

    col = lambda c: pl.BlockSpec((TS, D_MODEL), lambda b, i: (blk(b, i), c))
    in_specs = [col(0), col(2 if backward else 1), col(3), _const_spec(lb.shape), _const_spec(cum.shape),
                _const_spec(masks.shape)]
    args = [proj, proj, proj, lb, cum, masks]
    if not backward:
        ob, nw = extra
        in_specs += [col(0), col(4), _const_spec(nw.shape)]
        args += [ob, proj, nw]
    return pl.pallas_call(
        functools.partial(_hgrn_body, backward),
        grid=(B, nS),
        in_specs=in_specs,
        out_specs=pl.BlockSpec((TS, D_MODEL), lambda b, i: (blk(b, i), 0)),
        out_shape=jax.ShapeDtypeStruct((T, D_MODEL), F32 if backward else BF16),
        scratch_shapes=[pltpu.VMEM((HG_HEADS, HG_V, HG_K), F32)],
        compiler_params=_cparams("parallel", "arbitrary"),
        name="hgrn_bwd" if backward else "hgrn_fwd",
    )(*args)


def _t5_bucket(rel):
    half = T5_BUCKETS // 2
    max_exact = half // 2
    n = np.abs(rel)
    large = max_exact + (np.log(np.maximum(n, 1) / max_exact)
                         / np.log(T5_MAX_DIST / max_exact) * (half - max_exact)).astype(np.int32)
    large = np.minimum(large, half - 1)
    return ((rel > 0).astype(np.int32) * half + np.where(n < max_exact, n, large)).astype(np.int32)


def _attention_bias(t5_bias):
    qi = np.arange(A_BLOCK)[:, None]
    kj = np.arange(3 * A_BLOCK)[None, :] - A_BLOCK
    rel = kj - qi
    bias = jnp.transpose(t5_bias[_t5_bucket(rel)], (2, 0, 1)).astype(F32)
    return jnp.where(np.abs(rel)[None] <= A_BLOCK, bias, NEG)


def _trunk(x, B, S, p):
    for l in range(DEPTH):
        j = l // 2
        g = p["gains"][l]
        if l % 2 == 0:
            e = p["even"][j]
            qkv, z, xbc, dt, dtT = in_proj_even(x, g[0:1], e["wqkv"], e["wz"], e["wxbc"], e["wdt"], e["wdtT"])
            a_out = windowed_attention(qkv, p["attn_bias"], e["sink"], B, S)
            common = (dt, dtT, e["a_log"], e["a_logT"], e["dt_bias"], e["dt_biasT"], B, S)
            yb, xact = ssd_pass(True, xbc, *common, extra=(e["conv_w"], e["conv_b"]))
            b_out = ssd_pass(False, xact, *common, extra=(yb, z, e["d_skip"], e["norm_w"]))
            mixes, wouts = [a_out, b_out], [e["wout_a"], e["wout_b"]]
        else:
            o = p["odd"][j]
            proj = in_proj_odd(x, g[0:1], o["w_in"])
            ob = hgrn_pass(True, proj, o["lb"], B, S)
            og = hgrn_pass(False, proj, o["lb"], B, S, extra=(ob, o["norm_w"]))
            mixes, wouts = [og], [o["w_out"]]
        f = p["ffn"][l]
        x = out_proj_ffn(mixes, wouts, x, g[1:4], f["wg"], f["wu"], f["wd"])
    return x


def kernel(x_prompt, x_sample, norm_gains, t5_bias, ev_w_in, attn_sink, ssd_conv_w, ssd_conv_b, ssd_a_log,
           ssd_dt_bias, ssd_d, ssd_norm_w, ev_w_out, od_w_in, hg_lower_bounds, hg_norm_w, od_w_out,
           ffn_w_gate, ffn_w_up, ffn_w_down):
    nh2 = 2 * SSD_HEADS
    c_qkv = A_Q_DIM + 2 * A_KV_DIM
    c_z = c_qkv + SSD_INNER
    c_xbc = c_z + SSD_XBC
    lb_soft = jax.nn.softmax(hg_lower_bounds.astype(F32), axis=0)
    lb_all = jnp.cumsum(lb_soft, axis=0) - lb_soft[0]
    p = {"gains": norm_gains.astype(F32), "attn_bias": _attention_bias(t5_bias), "even": [], "odd": [], "ffn": []}
    for j in range(ev_w_in.shape[0]):
        w = ev_w_in[j]
        p["even"].append(dict(
            wqkv=w[:, :c_qkv].astype(BF16), wz=w[:, c_qkv:c_z].astype(BF16), wxbc=w[:, c_z:c_xbc].astype(BF16),
            wdt=w[:, c_xbc:].astype(BF16), wdtT=w[:, c_xbc:].T.astype(BF16),
            sink=attn_sink[j].astype(F32).reshape(1, A_HEADS),
            conv_w=ssd_conv_w[j].astype(F32), conv_b=ssd_conv_b[j].astype(F32).reshape(1, SSD_XBC),
            a_log=ssd_a_log[j].astype(F32).reshape(1, nh2), a_logT=ssd_a_log[j].astype(F32).reshape(nh2, 1),
            dt_bias=ssd_dt_bias[j].astype(F32).reshape(1, nh2), dt_biasT=ssd_dt_bias[j].astype(F32).reshape(nh2, 1),
            d_skip=jnp.repeat(ssd_d[j].astype(F32), SSD_HEAD_DIM).reshape(1, SSD_INNER),
            norm_w=ssd_norm_w[j].astype(F32).reshape(1, SSD_INNER),
            wout_a=ev_w_out[j, :A_Q_DIM].astype(BF16), wout_b=ev_w_out[j, A_Q_DIM:].astype(BF16)))
    for j in range(od_w_in.shape[0]):
        p["odd"].append(dict(
            w_in=od_w_in[j].astype(BF16), lb=lb_all[j].reshape(1, HG_HEADS * HG_K),
            norm_w=hg_norm_w[j].astype(F32).reshape(1, HG_V), w_out=od_w_out[j].astype(BF16)))
    for l in range(DEPTH):
        p["ffn"].append(dict(wg=ffn_w_gate[l].astype(BF16), wu=ffn_w_up[l].astype(BF16),
                             wd=ffn_w_down[l].astype(BF16)))
    outs = []
    for x in (x_prompt, x_sample):
        B, S, _ = x.shape
        outs.append(_trunk(x.reshape(B * S, D_MODEL), B, S, p).reshape(B, S, D_MODEL))
    return tuple(outs)
```

```python
import functools
import math

import numpy as np
import jax
import jax.numpy as jnp
from jax import lax
from jax.experimental import pallas as pl
from jax.experimental.pallas import tpu as pltpu

F32 = jnp.float32
BF16 = jnp.bfloat16

D_MODEL = 1024
DEPTH = 4
EPS = 1e-6
NEG = -1e30

A_HEADS = 8
A_KV_HEADS = 2
A_HEAD_DIM = 64
A_BLOCK = 128
A_Q_DIM = A_HEADS * A_HEAD_DIM
A_KV_DIM = A_KV_HEADS * A_HEAD_DIM
T5_BUCKETS = 32
T5_MAX_DIST = 128

SSD_HEADS = 16
SSD_HEAD_DIM = 64
SSD_INNER = SSD_HEADS * SSD_HEAD_DIM
SSD_GROUPS = 2
SSD_STATE = 64
SSD_CONV = 5
SSD_XBC = SSD_INNER + 2 * SSD_GROUPS * SSD_STATE
SSD_L = 128
HALO = 8

HG_HEADS = 8
HG_K = 128
HG_V = 128
HG_C = 128
HG_SUB = 32
HG_NB = HG_C // HG_SUB

D_FF = 2816
MXU_N = 256
FF_CHUNKS = ((0, 6 * MXU_N), (6 * MXU_N, D_FF))

TM = 512
TS = 512
VMEM_LIMIT_BYTES = 56 * 1024 * 1024


def _cparams(*sem):
    return pltpu.CompilerParams(dimension_semantics=sem, vmem_limit_bytes=VMEM_LIMIT_BYTES)


def _const_spec(shape):
    nd = len(shape)
    return pl.BlockSpec(shape, lambda *_: (0,) * nd, pipeline_mode=pl.Buffered(1))


def _rms(x, g):
    return x * lax.rsqrt(jnp.mean(x * x, axis=-1, keepdims=True) + EPS) * g


def _silu(x):
    return x * (1.0 / (1.0 + jnp.exp(-x)))


def _dot(a, b):
    return jnp.dot(a, b, preferred_element_type=F32)


def _dot_nt(a, b):
    return lax.dot_general(a, b, (((1,), (1,)), ((), ())), preferred_element_type=F32)


def _dot_tn(a, b):
    return lax.dot_general(a, b, (((0,), (0,)), ((), ())), preferred_element_type=F32)


def _split_hi_lo(x):
    hi = x.astype(BF16)
    lo = (x - hi.astype(F32)).astype(BF16)
    return hi, lo


def _in_even_body(x_ref, g_ref, wqkv_ref, wz_ref, wxbc_ref, wdt_ref, qkv_ref, z_ref, xbc_ref, dt_ref):
    xn = _rms(x_ref[...], g_ref[...]).astype(BF16)
    qkv_ref[...] = _dot(xn, wqkv_ref[...]).astype(BF16)
    z_ref[...] = _dot(xn, wz_ref[...])
    xbc_ref[...] = _dot(xn, wxbc_ref[...])
    dt_ref[...] = _dot(xn, wdt_ref[...])


def in_proj_even(x, g, wqkv, wz, wxbc, wdt):
    T = x.shape[0]
    nh2 = 2 * SSD_HEADS
    row = lambda c: pl.BlockSpec((TM, c), lambda i: (i, 0))
    return pl.pallas_call(
        _in_even_body,
        grid=(T // TM,),
        in_specs=[row(D_MODEL), _const_spec((1, D_MODEL)), _const_spec(wqkv.shape), _const_spec(wz.shape),
                  _const_spec(wxbc.shape), _const_spec(wdt.shape)],
        out_specs=[row(A_Q_DIM + 2 * A_KV_DIM), row(SSD_INNER), row(SSD_XBC), row(nh2)],
        out_shape=[jax.ShapeDtypeStruct((T, A_Q_DIM + 2 * A_KV_DIM), BF16),
                   jax.ShapeDtypeStruct((T, SSD_INNER), F32),
                   jax.ShapeDtypeStruct((T, SSD_XBC), F32),
                   jax.ShapeDtypeStruct((T, nh2), F32)],
        compiler_params=_cparams("parallel"),
        name="in_proj_even",
    )(x, g, wqkv, wz, wxbc, wdt)


def _in_odd_body(x_ref, g_ref, w_ref, o_ref):
    xn = _rms(x_ref[...], g_ref[...]).astype(BF16)
    n = w_ref.shape[1]
    for c in range(0, n, D_MODEL):
        o_ref[:, c:c + D_MODEL] = _dot(xn, w_ref[:, c:c + D_MODEL])


def in_proj_odd(x, g, w):
    T = x.shape[0]
    n = w.shape[1]
    return pl.pallas_call(
        _in_odd_body,
        grid=(T // TM,),
        in_specs=[pl.BlockSpec((TM, D_MODEL), lambda i: (i, 0)), _const_spec((1, D_MODEL)), _const_spec(w.shape)],
        out_specs=pl.BlockSpec((TM, n), lambda i: (i, 0)),
        out_shape=jax.ShapeDtypeStruct((T, n), F32),
        compiler_params=_cparams("parallel"),
        name="in_proj_odd",
    )(x, g, w)


def _out_ffn_body(n_mix, *refs):
    mix_refs = refs[:n_mix]
    wout_refs = refs[n_mix:2 * n_mix]
    x_ref, g_ref, wg_ref, wu_ref, wd_ref, o_ref = refs[2 * n_mix:]
    m = _dot(mix_refs[0][...], wout_refs[0][...])
    for a_ref, w_ref in zip(mix_refs[1:], wout_refs[1:]):
        m = m + _dot(a_ref[...], w_ref[...])
    x1 = x_ref[...] + _rms(m, g_ref[0:1, :])
    hn = _rms(x1, g_ref[1:2, :]).astype(BF16)
    f = None
    for c0, c1 in FF_CHUNKS:
        gate = _dot(hn, wg_ref[:, c0:c1])
        up = _dot(hn, wu_ref[:, c0:c1])
        h = (_silu(gate) * up).astype(BF16)
        part = _dot(h, wd_ref[c0:c1, :])
        f = part if f is None else f + part
    o_ref[...] = x1 + _rms(f, g_ref[2:3, :])


def out_proj_ffn(mixes, wouts, x, g3, wg, wu, wd):
    T = x.shape[0]
    n_mix = len(mixes)
    row = lambda c: pl.BlockSpec((TM, c), lambda i: (i, 0))
    in_specs = ([row(a.shape[1]) for a in mixes] + [_const_spec(w.shape) for w in wouts]
                + [row(D_MODEL), _const_spec(g3.shape), _const_spec(wg.shape), _const_spec(wu.shape),
                   _const_spec(wd.shape)])
    return pl.pallas_call(
        functools.partial(_out_ffn_body, n_mix),
        grid=(T // TM,),
        in_specs=in_specs,
        out_specs=row(D_MODEL),
        out_shape=jax.ShapeDtypeStruct((T, D_MODEL), F32),
        compiler_params=_cparams("parallel"),
        name="out_proj_ffn",
    )(*mixes, *wouts, x, g3, wg, wu, wd)


def _attn_body(q_ref, kp_ref, kc_ref, kn_ref, vp_ref, vc_ref, vn_ref, bias_ref, sink_ref, o_ref, kbuf, vtbuf):
    i = pl.program_id(1)
    n = pl.num_programs(1)
    nsub = TS // A_BLOCK
    dh = A_HEAD_DIM
    nk = TS + 2 * A_BLOCK
    row = lax.broadcasted_iota(jnp.int32, (2 * dh, nk), 0)
    for g in range(A_KV_HEADS):
        vs = []
        for lo, k_ref, v_ref in ((0, kp_ref, vp_ref), (A_BLOCK, kc_ref, vc_ref), (A_BLOCK + TS, kn_ref, vn_ref)):
            kbuf[g, lo:lo + k_ref.shape[0], :] = k_ref[:, g * dh:(g + 1) * dh]
            vs.append(v_ref[...])
        vt = jnp.concatenate(vs, axis=0).astype(F32).T[g * dh:(g + 1) * dh]
        vt = jnp.concatenate([vt, jnp.zeros_like(vt)], axis=0)
        vtbuf[g] = jnp.where(row == dh, 1.0, vt).astype(BF16)
    key = lax.broadcasted_iota(jnp.int32, (3 * A_BLOCK, A_BLOCK), 0)
    rep = A_HEADS // A_KV_HEADS
    heads = range(A_HEADS)

    def scores(sb):
        r0 = sb * A_BLOCK
        return [_dot_nt(kbuf[h // rep, r0:r0 + 3 * A_BLOCK, :], q_ref[r0:r0 + A_BLOCK, h * dh:(h + 1) * dh])
                for h in heads]

    def softmax_numerators(sb, ss):
        edge = None
        if sb == 0:
            edge = jnp.where(key < jnp.where(i == 0, A_BLOCK, 0), NEG, 0.0)
        if sb == nsub - 1:
            e2 = jnp.where(key >= jnp.where(i == n - 1, 2 * A_BLOCK, 3 * A_BLOCK), NEG, 0.0)
            edge = e2 if edge is None else edge + e2
        ps, ms = [], []
        for h in heads:
            s = ss[h] * (dh ** -0.5) + bias_ref[h]
            if edge is not None:
                s = s + edge
            m = jnp.maximum(jnp.max(s, axis=0, keepdims=True), sink_ref[0:1, h:h + 1])
            ps.append(jnp.exp(s - m).astype(BF16))
            ms.append(m)
        return ps, ms

    def weighted_values(sb, ps):
        r0 = sb * A_BLOCK
        return [_dot(vtbuf[h // rep, :, r0:r0 + 3 * A_BLOCK], ps[h]) for h in heads]

    def finish(sb, pv, ms):
        r0 = sb * A_BLOCK
        outs = []
        for h in heads:
            denom = pv[h][dh:dh + 1, :] + jnp.exp(sink_ref[0:1, h:h + 1] - ms[h])
            outs.append(pv[h][:dh, :] / denom)
        o_ref[r0:r0 + A_BLOCK, :] = jnp.concatenate(outs, axis=0).T.astype(o_ref.dtype)

    ss, pm, pv = {0: scores(0)}, {}, {}
    for sb in range(nsub + 1):
        if sb + 1 < nsub:
            ss[sb + 1] = scores(sb + 1)
        if sb < nsub:
            pm[sb] = softmax_numerators(sb, ss.pop(sb))
            pv[sb] = weighted_values(sb, pm[sb][0])
        if sb >= 1:
            finish(sb - 1, pv.pop(sb - 1), pm.pop(sb - 1)[1])


def windowed_attention(qkv, bias, sink, B, S):
    T = B * S
    nS = S // TS
    sub = TS // A_BLOCK
    nblk = T // A_BLOCK
    kcol = A_Q_DIM // A_KV_DIM
    vcol = kcol + 1
    cur = lambda c: pl.BlockSpec((TS, A_KV_DIM), lambda b, i: (b * nS + i, c))
    prev = lambda c: pl.BlockSpec((A_BLOCK, A_KV_DIM),
                                  lambda b, i: (jnp.maximum((b * nS + i) * sub - 1, 0), c))
    nxt = lambda c: pl.BlockSpec((A_BLOCK, A_KV_DIM),
                                 lambda b, i: (jnp.minimum((b * nS + i + 1) * sub, nblk - 1), c))
    return pl.pallas_call(
        _attn_body,
        grid=(B, nS),
        in_specs=[pl.BlockSpec((TS, A_Q_DIM), lambda b, i: (b * nS + i, 0)),
                  prev(kcol), cur(kcol), nxt(kcol), prev(vcol), cur(vcol), nxt(vcol),
                  _const_spec(bias.shape), _const_spec(sink.shape)],
        out_specs=pl.BlockSpec((TS, A_Q_DIM), lambda b, i: (b * nS + i, 0)),
        out_shape=jax.ShapeDtypeStruct((T, A_Q_DIM), BF16),
        scratch_shapes=[pltpu.VMEM((A_KV_HEADS, TS + 2 * A_BLOCK, A_HEAD_DIM), BF16),
                        pltpu.VMEM((A_KV_HEADS, 2 * A_HEAD_DIM, TS + 2 * A_BLOCK), BF16)],
        compiler_params=_cparams("parallel", "parallel"),
        name="windowed_attention",
    )(qkv, qkv, qkv, qkv, qkv, qkv, qkv, bias, sink)


def _split3_tiled(x):
    h = x.shape[1]
    t = jnp.concatenate([x, x, x], axis=1)
    hi = t.astype(BF16).astype(F32)
    r1 = t - hi
    mid = r1.astype(BF16).astype(F32)
    lane = lax.broadcasted_iota(jnp.int32, t.shape, 1)
    return jnp.where(lane < h, hi, jnp.where(lane < 2 * h, mid, r1 - mid)).astype(BF16)


def _ssd_constants():
    H, P, L = SSD_HEADS, SSD_HEAD_DIM, SSD_L
    selb = np.zeros((3 * H, H * L), np.float32)
    selx = np.zeros((3 * H, H * P), np.float32)
    for k in range(H):
        for part in range(3):
            selb[part * H + k, k * L:(k + 1) * L] = 1.0
            selx[part * H + k, k * P:(k + 1) * P] = 1.0
    return selb, selx


def _ssd_body(backward, *refs):
    if backward:
        (xc_ref, xp_ref, xn_ref, dt_ref, cw_ref, cb_ref, alog_ref, alogT_ref, dtb_ref,
         selb_ref, selx_ref, y_ref, xa_ref, xpad, hst) = refs
    else:
        (xa_ref, dt_ref, alog_ref, alogT_ref, dtb_ref, selb_ref, selx_ref,
         yb_ref, z_ref, dsk_ref, nw_ref, y_ref, hst) = refs
    i = pl.program_id(1)
    n = pl.num_programs(1)
    L = SSD_L
    H = SSD_HEADS
    P = SSD_HEAD_DIM
    nchunk = TS // L
    d0 = H if backward else 0
    gw = SSD_INNER // SSD_GROUPS

    @pl.when(i == 0)
    def _():
        hst[...] = jnp.zeros_like(hst)

    if backward:
        t = n - 1 - i
        rows = TS + 2 * HALO
        xpad[0:HALO, :] = jnp.where(t == 0, 0.0, xp_ref[...])
        xpad[HALO:HALO + TS, :] = xc_ref[...]
        xpad[HALO + TS:rows, :] = jnp.where(t == n - 1, 0.0, xn_ref[...])
        xp = xpad[...]
        acc = cb_ref[...] + cw_ref[SSD_CONV // 2:SSD_CONV // 2 + 1, :] * xp[HALO:HALO + TS]
        for j in range(SSD_CONV):
            if j != SSD_CONV // 2:
                sh = pltpu.roll(xp, (SSD_CONV // 2 - j) % rows, 0)
                acc = acc + cw_ref[j:j + 1, :] * sh[HALO:HALO + TS]
        xa_ref[...] = _silu(acc).astype(xa_ref.dtype)

    a_row = -jnp.exp(alog_ref[...])[:, d0:d0 + H]
    a_col = -jnp.exp(alogT_ref[...])[d0:d0 + H, :]
    r_i = lax.broadcasted_iota(jnp.int32, (L, L), 0)
    c_i = lax.broadcasted_iota(jnp.int32, (L, L), 1)
    keep = (c_i >= r_i) if backward else (c_i <= r_i)
    tri = jnp.where(keep, 1.0, 0.0).astype(BF16)
    triT = jnp.where(c_i <= r_i if backward else c_i >= r_i, 1.0, 0.0).astype(BF16)
    last = 0 if backward else L - 1
    lane = lax.broadcasted_iota(jnp.int32, (1, 2 * P), 1)

    def chunk(ci, carry):
        c = (nchunk - 1 - ci) if backward else ci
        r0 = pl.multiple_of(c * L, L)
        dt2 = jax.nn.softplus(dt_ref[pl.ds(r0, L), :] + dtb_ref[...])
        dt = dt2[:, d0:d0 + H]
        dtT = jnp.concatenate([dt2, jnp.zeros((L, L - 2 * H), F32)], axis=1).T[d0:d0 + H, :]
        w_hi, w_lo = _split_hi_lo(dt * a_row)
        cs = _dot(tri, w_hi) + _dot(tri, w_lo)
        wT_hi, wT_lo = _split_hi_lo(dtT * a_col)
        csT = _dot(wT_hi, triT) + _dot(wT_lo, triT)
        ecs = jnp.exp(cs)
        tot = cs[last:last + 1, :]
        scale = dt * jnp.exp(tot - cs)
        csb = _dot(_split3_tiled(cs), selb_ref[...])
        ex = _dot(_split3_tiled(ecs), selx_ref[...])
        sx = _dot(_split3_tiled(scale), selx_ref[...])
        etx = ex[last:last + 1, :]
        xs = xa_ref[pl.ds(r0, L), 0:SSD_INNER]
        bm = xa_ref[pl.ds(r0, L), SSD_INNER:SSD_INNER + SSD_GROUPS * SSD_STATE]
        cm = xa_ref[pl.ds(r0, L), SSD_INNER + SSD_GROUPS * SSD_STATE:SSD_XBC]
        xw = (xs.astype(F32) * sx).astype(BF16)
        cbs, yoffs = [], []
        for g in range(SSD_GROUPS):
            bg = bm[:, g * SSD_STATE:(g + 1) * SSD_STATE]
            cg = cm[:, g * SSD_STATE:(g + 1) * SSD_STATE]
            cbs.append(_dot_nt(cg, bg))
            hg = hst[g]
            yoffs.append(_dot(cg, hg.astype(BF16)) * ex[:, g * gw:(g + 1) * gw])
            hst[g] = etx[:, g * gw:(g + 1) * gw] * hg + _dot_tn(bg, xw[:, g * gw:(g + 1) * gw])
        ms = []
        for k in range(H):
            seg = csb[:, k * L:(k + 1) * L] - csT[k:k + 1, :]
            dm = jnp.exp(jnp.where(keep, seg, NEG)) * dtT[k:k + 1, :]
            ms.append((cbs[k // (H // SSD_GROUPS)] * dm).astype(BF16))
        ys = []
        for j in range(H // 2):
            xp2 = xs[:, 2 * j * P:(2 * j + 2) * P]
            zero = jnp.zeros_like(xp2)
            rhs = jnp.concatenate([jnp.where(lane < P, xp2, zero), jnp.where(lane >= P, xp2, zero)], axis=0)
            ys.append(_dot(jnp.concatenate([ms[2 * j], ms[2 * j + 1]], axis=1), rhs))
        y = jnp.concatenate(ys, axis=1) + jnp.concatenate(yoffs, axis=1)
        if backward:
            y_ref[pl.ds(r0, L), :] = y
        else:
            y = y + yb_ref[pl.ds(r0, L), :] + dsk_ref[...] * xs.astype(F32)
            y = y * _silu(z_ref[pl.ds(r0, L), :])
            outs = []
            for g in range(SSD_GROUPS):
                yg = y[:, g * gw:(g + 1) * gw]
                outs.append(yg * lax.rsqrt(jnp.mean(yg * yg, axis=-1, keepdims=True) + EPS))
            y_ref[pl.ds(r0, L), :] = (jnp.concatenate(outs, axis=1) * nw_ref[...]).astype(y_ref.dtype)
        return carry

    lax.fori_loop(0, nchunk, chunk, 0)


def ssd_pass(backward, src, dt, alog, alogT, dtb, B, S, extra):
    T = B * S
    nS = S // TS
    per = TS // HALO
    nh2 = 2 * SSD_HEADS
    selb, selx = (jnp.asarray(m, BF16) for m in _ssd_constants())

    def blk(b, i):
        return b * nS + ((nS - 1 - i) if backward else i)

    row = lambda c: pl.BlockSpec((TS, c), lambda b, i: (blk(b, i), 0))
    dt_specs = [row(nh2)]
    par_specs = [_const_spec(alog.shape), _const_spec(alogT.shape), _const_spec(dtb.shape),
                 _const_spec(selb.shape), _const_spec(selx.shape)]
    state = pltpu.VMEM((SSD_GROUPS, SSD_STATE, SSD_INNER // SSD_GROUPS), F32)
    if backward:
        cw, cb = extra
        in_specs = ([row(SSD_XBC),
                     pl.BlockSpec((HALO, SSD_XBC), lambda b, i: (jnp.maximum(blk(b, i) * per - 1, 0), 0)),
                     pl.BlockSpec((HALO, SSD_XBC),
                                  lambda b, i: (jnp.minimum((blk(b, i) + 1) * per, T // HALO - 1), 0))]
                    + dt_specs + [_const_spec(cw.shape), _const_spec(cb.shape)] + par_specs)
        args = [src, src, src, dt, cw, cb, alog, alogT, dtb, selb, selx]
        out_specs = [row(SSD_INNER), row(SSD_XBC)]
        out_shape = [jax.ShapeDtypeStruct((T, SSD_INNER), F32), jax.ShapeDtypeStruct((T, SSD_XBC), BF16)]
        scratch = [pltpu.VMEM((TS + 2 * HALO, SSD_XBC), F32), state]
    else:
        yb, z, dsk, nw = extra
        in_specs = ([row(SSD_XBC)] + dt_specs + par_specs
                    + [row(SSD_INNER), row(SSD_INNER), _const_spec(dsk.shape), _const_spec(nw.shape)])
        args = [src, dt, alog, alogT, dtb, selb, selx, yb, z, dsk, nw]
        out_specs = row(SSD_INNER)
        out_shape = jax.ShapeDtypeStruct((T, SSD_INNER), BF16)
        scratch = [state]
    return pl.pallas_call(
        functools.partial(_ssd_body, backward),
        grid=(B, nS),
        in_specs=in_specs,
        out_specs=out_specs,
        out_shape=out_shape,
        scratch_shapes=scratch,
        compiler_params=_cparams("parallel", "arbitrary"),
        name="ssd_bwd" if backward else "ssd_fwd",
    )(*args)


def _hgrn_constants(backward):
    C, NB = HG_C, HG_NB
    t = np.arange(C)[:, None]
    u = np.arange(C)[None, :]
    if backward:
        t, u = C - 1 - t, C - 1 - u
    bt, bu = t // HG_SUB, u // HG_SUB
    cum = (bt == bu) & (u <= t)
    masks = [cum] + [bt - bu == d for d in range(1, NB)]
    return cum.astype(np.float32), np.stack(masks).astype(np.float32)


def _rows_to_tile(rows):
    return jnp.concatenate([jnp.broadcast_to(r, (HG_SUB, r.shape[1])) for r in rows], axis=0)


def _hgrn_body(backward, *refs):
    if backward:
        (q_ref, f_ref, v_ref, lb_ref, cum_ref, msk_ref, o_ref, st) = refs
    else:
        (q_ref, f_ref, v_ref, lb_ref, cum_ref, msk_ref, ob_ref, g_ref, nw_ref, o_ref, st) = refs
    i = pl.program_id(1)
    C, NB = HG_C, HG_NB
    nchunk = TS // C
    order = list(range(NB - 1, -1, -1)) if backward else list(range(NB))
    end_row = 0 if backward else HG_SUB - 1

    @pl.when(i == 0)
    def _():
        st[...] = jnp.zeros_like(st)

    def chunk(ci, carry):
        c = (nchunk - 1 - ci) if backward else ci
        r0 = pl.multiple_of(c * C, C)
        cum = cum_ref[...]
        heads = range(HG_HEADS)
        hsl = [slice(h * HG_K, (h + 1) * HG_K) for h in heads]
        ks, qv, vv, hl = [], [], [], []
        for h in heads:
            lb = lb_ref[:, hsl[h]]
            fr = f_ref[pl.ds(r0, C), hsl[h]]
            e = jnp.exp(-jnp.abs(fr))
            r = 1.0 / (1.0 + e)
            sg = jnp.where(fr >= 0, r, e * r)
            sgn = jnp.where(fr >= 0, e * r, r)
            logf = jnp.log(lb + (1.0 - lb) * sg)
            ks.append((1.0 - lb) * sgn)
            qv.append(_silu(q_ref[pl.ds(r0, C), hsl[h]]))
            vv.append(v_ref[pl.ds(r0, C), hsl[h]].astype(BF16))
            hl.append(jnp.concatenate(_split_hi_lo(logf), axis=1))
        gls = [_dot(cum, hl[h]) for h in heads]
        qt_all, qd_all, kt_all, kh_all, qin_all, kst_all, dec_all = [], [], [], [], [], [], []
        for h in heads:
            gl = gls[h][:, :HG_K] + gls[h][:, HG_K:]
            tot = [gl[j * HG_SUB + end_row:j * HG_SUB + end_row + 1, :] for j in range(NB)]
            zero = jnp.zeros_like(tot[0])
            pre, post, near = [None] * NB, [None] * NB, [[None] * NB for _ in range(NB)]
            for p, b in enumerate(order):
                pre[b] = sum((tot[order[pp]] for pp in range(p)), zero)
                post[b] = sum((tot[order[pp]] for pp in range(p + 1, NB)), zero)
                for d in range(2, NB):
                    near[d][b] = sum((tot[order[pp]] for pp in range(max(p - d + 1, 0), p)), zero)
            qt = qv[h] * jnp.exp(gl)
            kt = ks[h] * jnp.exp(-gl)
            kh = kt * _rows_to_tile([jnp.exp(x) for x in tot])
            qin_all.append((qt * _rows_to_tile([jnp.exp(x) for x in pre])).astype(BF16))
            kst_all.append((kh * _rows_to_tile([jnp.exp(x) for x in post])).astype(BF16))
            qd = []
            for d in range(1, NB):
                blocks = sorted(order[d:])
                rows = slice(blocks[0] * HG_SUB, (blocks[-1] + 1) * HG_SUB)
                if d == 1:
                    qd.append(qt[rows].astype(BF16))
                else:
                    qd.append((qt[rows] * _rows_to_tile([jnp.exp(near[d][b]) for b in blocks])).astype(BF16))
            qd_all.append(jnp.concatenate(qd, axis=0))
            qt_all.append(qt.astype(BF16))
            kt_all.append(kt.astype(BF16))
            kh_all.append(kh.astype(BF16))
            dec_all.append(jnp.exp(sum(tot, zero)))
        att0 = [_dot_nt(qt_all[h], kt_all[h]) for h in heads]
        offs = [_dot_nt(qd_all[h], kh_all[h]) for h in heads]
        lane_blk = lax.broadcasted_iota(jnp.int32, (HG_SUB, C), 1) // HG_SUB
        atts = []
        for h in heads:
            rows_out = []
            for p, b in enumerate(order):
                r = att0[h][b * HG_SUB:(b + 1) * HG_SUB] * msk_ref[0, b * HG_SUB:(b + 1) * HG_SUB, :]
                base = 0
                for d in range(1, NB):
                    blocks = sorted(order[d:])
                    if p >= d:
                        o0 = base + (b - blocks[0]) * HG_SUB
                        r = jnp.where(lane_blk == order[p - d], offs[h][o0:o0 + HG_SUB], r)
                    base += len(blocks) * HG_SUB
                rows_out.append((b, r))
            atts.append(jnp.concatenate([r for _, r in sorted(rows_out, key=lambda t: t[0])], axis=0).astype(BF16))
        sTs = [st[h] for h in heads]
        outs = [_dot(atts[h], vv[h]) + _dot_nt(qin_all[h], sTs[h].astype(BF16)) for h in heads]
        upd = [_dot_tn(vv[h], kst_all[h]) for h in heads]
        for h in heads:
            st[h] = sTs[h] * dec_all[h] + upd[h]
            o = outs[h]
            if backward:
                o_ref[pl.ds(r0, C), hsl[h]] = o
            else:
                o = o + ob_ref[pl.ds(r0, C), hsl[h]]
                o = o * lax.rsqrt(jnp.mean(o * o, axis=-1, keepdims=True) + EPS) * nw_ref[...]
                o_ref[pl.ds(r0, C), hsl[h]] = (o * _silu(g_ref[pl.ds(r0, C), hsl[h]])).astype(o_ref.dtype)
        return carry

    lax.fori_loop(0, nchunk, chunk, 0)


def hgrn_pass(backward, proj, lb, B, S, extra=()):
    T = B * S
    nS = S // TS
    cum, masks = _hgrn_constants(backward)
    cum = jnp.asarray(cum, BF16)
    masks = jnp.asarray(masks, F32)

    def blk(b, i):
        return b * nS + ((nS - 1 - i) if backward else i)

    col = lambda c: pl.BlockSpec((TS, D_MODEL), lambda b, i: (blk(b, i), c))
    in_specs = [col(0), col(2 if backward else 1), col(3), _const_spec(lb.shape), _const_spec(cum.shape),
                _const_spec(masks.shape)]
    args = [proj, proj, proj, lb, cum, masks]
    if not backward:
        ob, nw = extra
        in_specs += [col(0), col(4), _const_spec(nw.shape)]
        args += [ob, proj, nw]
    return pl.pallas_call(
        functools.partial(_hgrn_body, backward),
        grid=(B, nS),
        in_specs=in_specs,
        out_specs=pl.BlockSpec((TS, D_MODEL), lambda b, i: (blk(b, i), 0)),
        out_shape=jax.ShapeDtypeStruct((T, D_MODEL), F32 if backward else BF16),
        scratch_shapes=[pltpu.VMEM((HG_HEADS, HG_V, HG_K), F32)],
        compiler_params=_cparams("parallel", "arbitrary"),
        name="hgrn_bwd" if backward else "hgrn_fwd",
    )(*args)


def _t5_bucket(rel):
    half = T5_BUCKETS // 2
    max_exact = half // 2
    n = np.abs(rel)
    large = max_exact + (np.log(np.maximum(n, 1) / max_exact)
                         / np.log(T5_MAX_DIST / max_exact) * (half - max_exact)).astype(np.int32)
    large = np.minimum(large, half - 1)
    return ((rel > 0).astype(np.int32) * half + np.where(n < max_exact, n, large)).astype(np.int32)


def _attention_bias(t5_bias):
    qi = np.arange(A_BLOCK)[None, :]
    kj = np.arange(3 * A_BLOCK)[:, None] - A_BLOCK
    rel = kj - qi
    onehot = (_t5_bucket(rel)[None] == np.arange(T5_BUCKETS)[:, None, None]).astype(np.float32)
    bias = jnp.einsum("bh,bkq->hkq", t5_bias.astype(F32), onehot, precision=lax.Precision.HIGHEST)
    return jnp.where(np.abs(rel)[None] <= A_BLOCK, bias, NEG)


def _trunk(x, B, S, p):
    for l in range(DEPTH):
        j = l // 2
        g = p["gains"][l]
        if l % 2 == 0:
            e = p["even"][j]
            qkv, z, xbc, dt = in_proj_even(x, g[0:1], e["wqkv"], e["wz"], e["wxbc"], e["wdt"])
            a_out = windowed_attention(qkv, p["attn_bias"], e["sink"], B, S)
            common = (dt, e["a_log"], e["a_logT"], e["dt_bias"], B, S)
            yb, xact = ssd_pass(True, xbc, *common, extra=(e["conv_w"], e["conv_b"]))
            b_out = ssd_pass(False, xact, *common, extra=(yb, z, e["d_skip"], e["norm_w"]))
            mixes, wouts = [a_out, b_out], [e["wout_a"], e["wout_b"]]
        else:
            o = p["odd"][j]
            proj = in_proj_odd(x, g[0:1], o["w_in"])
            ob = hgrn_pass(True, proj, o["lb"], B, S)
            og = hgrn_pass(False, proj, o["lb"], B, S, extra=(ob, o["norm_w"]))
            mixes, wouts = [og], [o["w_out"]]
        f = p["ffn"][l]
        x = out_proj_ffn(mixes, wouts, x, g[1:4], f["wg"], f["wu"], f["wd"])
    return x


def kernel(x_prompt, x_sample, norm_gains, t5_bias, ev_w_in, attn_sink, ssd_conv_w, ssd_conv_b, ssd_a_log,
           ssd_dt_bias, ssd_d, ssd_norm_w, ev_w_out, od_w_in, hg_lower_bounds, hg_norm_w, od_w_out,
           ffn_w_gate, ffn_w_up, ffn_w_down):
    nh2 = 2 * SSD_HEADS
    c_qkv = A_Q_DIM + 2 * A_KV_DIM
    c_z = c_qkv + SSD_INNER
    c_xbc = c_z + SSD_XBC
    lb_soft = jax.nn.softmax(hg_lower_bounds.astype(F32), axis=0)
    lb_all = jnp.cumsum(lb_soft, axis=0) - lb_soft[0]
    p = {"gains": norm_gains.astype(F32), "attn_bias": _attention_bias(t5_bias), "even": [], "odd": [], "ffn": []}
    for j in range(ev_w_in.shape[0]):
        w = ev_w_in[j]
        p["even"].append(dict(
            wqkv=w[:, :c_qkv].astype(BF16), wz=w[:, c_qkv:c_z].astype(BF16), wxbc=w[:, c_z:c_xbc].astype(BF16),
            wdt=w[:, c_xbc:].astype(BF16),
            sink=attn_sink[j].astype(F32).reshape(1, A_HEADS),
            conv_w=ssd_conv_w[j].astype(F32), conv_b=ssd_conv_b[j].astype(F32).reshape(1, SSD_XBC),
            a_log=ssd_a_log[j].astype(F32).reshape(1, nh2), a_logT=ssd_a_log[j].astype(F32).reshape(nh2, 1),
            dt_bias=ssd_dt_bias[j].astype(F32).reshape(1, nh2),
            d_skip=jnp.repeat(ssd_d[j].astype(F32), SSD_HEAD_DIM).reshape(1, SSD_INNER),
            norm_w=ssd_norm_w[j].astype(F32).reshape(1, SSD_INNER),
            wout_a=ev_w_out[j, :A_Q_DIM].astype(BF16), wout_b=ev_w_out[j, A_Q_DIM:].astype(BF16)))
    for j in range(od_w_in.shape[0]):
        p["odd"].append(dict(
            w_in=od_w_in[j].astype(BF16), lb=lb_all[j].reshape(1, HG_HEADS * HG_K),
            norm_w=hg_norm_w[j].astype(F32).reshape(1, HG_V), w_out=od_w_out[j].astype(BF16)))
    for l in range(DEPTH):
        p["ffn"].append(dict(wg=ffn_w_gate[l].astype(BF16), wu=ffn_w_up[l].astype(BF16),
                             wd=ffn_w_down[l].astype(BF16)))
    outs = []
    for x in (x_prompt, x_sample):
        B, S, _ = x.shape
        outs.append(_trunk(x.reshape(B * S, D_MODEL), B, S, p).reshape(B, S, D_MODEL))
    return tuple(outs)
```

```python
import functools
import math

import numpy as np
import jax
import jax.numpy as jnp
from jax import lax
from jax.experimental import pallas as pl
from jax.experimental.pallas import tpu as pltpu

F32 = jnp.float32
BF16 = jnp.bfloat16

D_MODEL = 1024
DEPTH = 4
EPS = 1e-6
NEG = -1e30

A_HEADS = 8
A_KV_HEADS = 2
A_HEAD_DIM = 64
A_BLOCK = 128
A_Q_DIM = A_HEADS * A_HEAD_DIM
A_KV_DIM = A_KV_HEADS * A_HEAD_DIM
T5_BUCKETS = 32
T5_MAX_DIST = 128

SSD_HEADS = 16
SSD_HEAD_DIM = 64
SSD_INNER = SSD_HEADS * SSD_HEAD_DIM
SSD_GROUPS = 2
SSD_STATE = 64
SSD_CONV = 5
SSD_XBC = SSD_INNER + 2 * SSD_GROUPS * SSD_STATE
SSD_L = 128
HALO = 8

HG_HEADS = 8
HG_K = 128
HG_V = 128
HG_C = 128
HG_SUB = 32
HG_NB = HG_C // HG_SUB

D_FF = 2816
MXU_N = 256
FF_CHUNKS = ((0, 6 * MXU_N), (6 * MXU_N, D_FF))

TM = 512
TS = 512
VMEM_LIMIT_BYTES = 56 * 1024 * 1024


def _cparams(*sem):
    return pltpu.CompilerParams(dimension_semantics=sem, vmem_limit_bytes=VMEM_LIMIT_BYTES)


def _const_spec(shape):
    nd = len(shape)
    return pl.BlockSpec(shape, lambda *_: (0,) * nd, pipeline_mode=pl.Buffered(1))


def _rms(x, g):
    return x * lax.rsqrt(jnp.mean(x * x, axis=-1, keepdims=True) + EPS) * g


def _silu(x):
    return x * (1.0 / (1.0 + jnp.exp(-x)))


def _dot(a, b):
    return jnp.dot(a, b, preferred_element_type=F32)


def _dot_nt(a, b):
    return lax.dot_general(a, b, (((1,), (1,)), ((), ())), preferred_element_type=F32)


def _dot_tn(a, b):
    return lax.dot_general(a, b, (((0,), (0,)), ((), ())), preferred_element_type=F32)


def _split_hi_lo(x):
    hi = x.astype(BF16)
    lo = (x - hi.astype(F32)).astype(BF16)
    return hi, lo


def _in_even_body(x_ref, g_ref, wqkv_ref, wz_ref, wxbc_ref, wdt_ref, qkv_ref, z_ref, xbc_ref, dt_ref):
    xn = _rms(x_ref[...], g_ref[...]).astype(BF16)
    qkv_ref[...] = _dot(xn, wqkv_ref[...]).astype(BF16)
    z_ref[...] = _dot(xn, wz_ref[...])
    xbc_ref[...] = _dot(xn, wxbc_ref[...])
    dt_ref[...] = _dot(xn, wdt_ref[...])


def in_proj_even(x, g, wqkv, wz, wxbc, wdt):
    T = x.shape[0]
    nh2 = 2 * SSD_HEADS
    row = lambda c: pl.BlockSpec((TM, c), lambda i: (i, 0))
    return pl.pallas_call(
        _in_even_body,
        grid=(T // TM,),
        in_specs=[row(D_MODEL), _const_spec((1, D_MODEL)), _const_spec(wqkv.shape), _const_spec(wz.shape),
                  _const_spec(wxbc.shape), _const_spec(wdt.shape)],
        out_specs=[row(A_Q_DIM + 2 * A_KV_DIM), row(SSD_INNER), row(SSD_XBC), row(nh2)],
        out_shape=[jax.ShapeDtypeStruct((T, A_Q_DIM + 2 * A_KV_DIM), BF16),
                   jax.ShapeDtypeStruct((T, SSD_INNER), F32),
                   jax.ShapeDtypeStruct((T, SSD_XBC), F32),
                   jax.ShapeDtypeStruct((T, nh2), F32)],
        compiler_params=_cparams("parallel"),
        name="in_proj_even",
    )(x, g, wqkv, wz, wxbc, wdt)


def _in_odd_body(x_ref, g_ref, w_ref, o_ref):
    xn = _rms(x_ref[...], g_ref[...]).astype(BF16)
    n = w_ref.shape[1]
    for c in range(0, n, D_MODEL):
        o_ref[:, c:c + D_MODEL] = _dot(xn, w_ref[:, c:c + D_MODEL])


def in_proj_odd(x, g, w):
    T = x.shape[0]
    n = w.shape[1]
    return pl.pallas_call(
        _in_odd_body,
        grid=(T // TM,),
        in_specs=[pl.BlockSpec((TM, D_MODEL), lambda i: (i, 0)), _const_spec((1, D_MODEL)), _const_spec(w.shape)],
        out_specs=pl.BlockSpec((TM, n), lambda i: (i, 0)),
        out_shape=jax.ShapeDtypeStruct((T, n), F32),
        compiler_params=_cparams("parallel"),
        name="in_proj_odd",
    )(x, g, w)


def _out_ffn_body(n_mix, *refs):
    mix_refs = refs[:n_mix]
    wout_refs = refs[n_mix:2 * n_mix]
    x_ref, g_ref, wg_ref, wu_ref, wd_ref, o_ref = refs[2 * n_mix:]
    halves = [slice(0, TM // 2), slice(TM // 2, TM)]
    ms = []
    for r in halves:
        m = _dot(mix_refs[0][r, :], wout_refs[0][...])
        for a_ref, w_ref in zip(mix_refs[1:], wout_refs[1:]):
            m = m + _dot(a_ref[r, :], w_ref[...])
        ms.append(m)
    x1s = [x_ref[r, :] + _rms(m, g_ref[0:1, :]) for r, m in zip(halves, ms)]
    hns = [_rms(x1, g_ref[1:2, :]).astype(BF16) for x1 in x1s]
    fs = [None] * len(halves)
    for c0, c1 in FF_CHUNKS:
        gates = [_dot(hn, wg_ref[:, c0:c1]) for hn in hns]
        ups = [_dot(hn, wu_ref[:, c0:c1]) for hn in hns]
        hs = [(_silu(gate) * up).astype(BF16) for gate, up in zip(gates, ups)]
        parts = [_dot(h, wd_ref[c0:c1, :]) for h in hs]
        fs = [part if f is None else f + part for f, part in zip(fs, parts)]
    for r, x1, f in zip(halves, x1s, fs):
        o_ref[r, :] = x1 + _rms(f, g_ref[2:3, :])


def out_proj_ffn(mixes, wouts, x, g3, wg, wu, wd):
    T = x.shape[0]
    n_mix = len(mixes)
    row = lambda c: pl.BlockSpec((TM, c), lambda i: (i, 0))
    in_specs = ([row(a.shape[1]) for a in mixes] + [_const_spec(w.shape) for w in wouts]
                + [row(D_MODEL), _const_spec(g3.shape), _const_spec(wg.shape), _const_spec(wu.shape),
                   _const_spec(wd.shape)])
    return pl.pallas_call(
        functools.partial(_out_ffn_body, n_mix),
        grid=(T // TM,),
        in_specs=in_specs,
        out_specs=row(D_MODEL),
        out_shape=jax.ShapeDtypeStruct((T, D_MODEL), F32),
        compiler_params=_cparams("parallel"),
        name="out_proj_ffn",
    )(*mixes, *wouts, x, g3, wg, wu, wd)


def _attn_body(q_ref, kp_ref, kc_ref, kn_ref, vp_ref, vc_ref, vn_ref, bias_ref, sink_ref, o_ref, kbuf, vtbuf):
    i = pl.program_id(1)
    n = pl.num_programs(1)
    nsub = TS // A_BLOCK
    dh = A_HEAD_DIM
    nk = TS + 2 * A_BLOCK
    row = lax.broadcasted_iota(jnp.int32, (2 * dh, nk), 0)
    for g in range(A_KV_HEADS):
        vs = []
        for lo, k_ref, v_ref in ((0, kp_ref, vp_ref), (A_BLOCK, kc_ref, vc_ref), (A_BLOCK + TS, kn_ref, vn_ref)):
            kbuf[g, lo:lo + k_ref.shape[0], :] = k_ref[:, g * dh:(g + 1) * dh]
            vs.append(v_ref[...])
        vt = jnp.concatenate(vs, axis=0).astype(F32).T[g * dh:(g + 1) * dh]
        vt = jnp.concatenate([vt, jnp.zeros_like(vt)], axis=0)
        vtbuf[g] = jnp.where(row == dh, 1.0, vt).astype(BF16)
    key = lax.broadcasted_iota(jnp.int32, (3 * A_BLOCK, A_BLOCK), 0)
    rep = A_HEADS // A_KV_HEADS
    heads = range(A_HEADS)

    def scores(sb):
        r0 = sb * A_BLOCK
        return [_dot_nt(kbuf[h // rep, r0:r0 + 3 * A_BLOCK, :], q_ref[r0:r0 + A_BLOCK, h * dh:(h + 1) * dh])
                for h in heads]

    def softmax_numerators(sb, ss):
        edge = None
        if sb == 0:
            edge = jnp.where(key < jnp.where(i == 0, A_BLOCK, 0), NEG, 0.0)
        if sb == nsub - 1:
            e2 = jnp.where(key >= jnp.where(i == n - 1, 2 * A_BLOCK, 3 * A_BLOCK), NEG, 0.0)
            edge = e2 if edge is None else edge + e2
        ps, ms = [], []
        for h in heads:
            s = ss[h] * (dh ** -0.5) + bias_ref[h]
            if edge is not None:
                s = s + edge
            m = jnp.maximum(jnp.max(s, axis=0, keepdims=True), sink_ref[0:1, h:h + 1])
            ps.append(jnp.exp(s - m).astype(BF16))
            ms.append(m)
        return ps, ms

    def weighted_values(sb, ps):
        r0 = sb * A_BLOCK
        return [_dot(vtbuf[h // rep, :, r0:r0 + 3 * A_BLOCK], ps[h]) for h in heads]

    def finish(sb, pv, ms):
        r0 = sb * A_BLOCK
        outs = []
        for h in heads:
            denom = pv[h][dh:dh + 1, :] + jnp.exp(sink_ref[0:1, h:h + 1] - ms[h])
            outs.append(pv[h][:dh, :] / denom)
        o_ref[r0:r0 + A_BLOCK, :] = jnp.concatenate(outs, axis=0).T.astype(o_ref.dtype)

    ss, pm, pv = {0: scores(0)}, {}, {}
    for sb in range(nsub + 1):
        if sb + 1 < nsub:
            ss[sb + 1] = scores(sb + 1)
        if sb < nsub:
            pm[sb] = softmax_numerators(sb, ss.pop(sb))
            pv[sb] = weighted_values(sb, pm[sb][0])
        if sb >= 1:
            finish(sb - 1, pv.pop(sb - 1), pm.pop(sb - 1)[1])


def windowed_attention(qkv, bias, sink, B, S):
    T = B * S
    nS = S // TS
    sub = TS // A_BLOCK
    nblk = T // A_BLOCK
    kcol = A_Q_DIM // A_KV_DIM
    vcol = kcol + 1
    cur = lambda c: pl.BlockSpec((TS, A_KV_DIM), lambda b, i: (b * nS + i, c))
    prev = lambda c: pl.BlockSpec((A_BLOCK, A_KV_DIM),
                                  lambda b, i: (jnp.maximum((b * nS + i) * sub - 1, 0), c))
    nxt = lambda c: pl.BlockSpec((A_BLOCK, A_KV_DIM),
                                 lambda b, i: (jnp.minimum((b * nS + i + 1) * sub, nblk - 1), c))
    return pl.pallas_call(
        _attn_body,
        grid=(B, nS),
        in_specs=[pl.BlockSpec((TS, A_Q_DIM), lambda b, i: (b * nS + i, 0)),
                  prev(kcol), cur(kcol), nxt(kcol), prev(vcol), cur(vcol), nxt(vcol),
                  _const_spec(bias.shape), _const_spec(sink.shape)],
        out_specs=pl.BlockSpec((TS, A_Q_DIM), lambda b, i: (b * nS + i, 0)),
        out_shape=jax.ShapeDtypeStruct((T, A_Q_DIM), BF16),
        scratch_shapes=[pltpu.VMEM((A_KV_HEADS, TS + 2 * A_BLOCK, A_HEAD_DIM), BF16),
                        pltpu.VMEM((A_KV_HEADS, 2 * A_HEAD_DIM, TS + 2 * A_BLOCK), BF16)],
        compiler_params=_cparams("parallel", "parallel"),
        name="windowed_attention",
    )(qkv, qkv, qkv, qkv, qkv, qkv, qkv, bias, sink)


def _split3_tiled(x):
    h = x.shape[1]
    t = jnp.concatenate([x, x, x], axis=1)
    hi = t.astype(BF16).astype(F32)
    r1 = t - hi
    mid = r1.astype(BF16).astype(F32)
    lane = lax.broadcasted_iota(jnp.int32, t.shape, 1)
    return jnp.where(lane < h, hi, jnp.where(lane < 2 * h, mid, r1 - mid)).astype(BF16)


def _ssd_constants():
    H, P, L = SSD_HEADS, SSD_HEAD_DIM, SSD_L
    selb = np.zeros((3 * H, H * L), np.float32)
    selx = np.zeros((3 * H, H * P), np.float32)
    for k in range(H):
        for part in range(3):
            selb[part * H + k, k * L:(k + 1) * L] = 1.0
            selx[part * H + k, k * P:(k + 1) * P] = 1.0
    return selb, selx


def _ssd_body(backward, *refs):
    if backward:
        (xc_ref, xp_ref, xn_ref, dt_ref, cw_ref, cb_ref, alog_ref, alogT_ref, dtb_ref,
         selb_ref, selx_ref, y_ref, xa_ref, xpad, hst, csb_s, ex_s, sx_s, rowT_s) = refs
    else:
        (xa_ref, dt_ref, alog_ref, alogT_ref, dtb_ref, selb_ref, selx_ref,
         yb_ref, z_ref, dsk_ref, nw_ref, y_ref, hst, csb_s, ex_s, sx_s, rowT_s) = refs
    i = pl.program_id(1)
    n = pl.num_programs(1)
    L = SSD_L
    H = SSD_HEADS
    P = SSD_HEAD_DIM
    nchunk = TS // L
    d0 = H if backward else 0
    gw = SSD_INNER // SSD_GROUPS

    @pl.when(i == 0)
    def _():
        hst[...] = jnp.zeros_like(hst)

    if backward:
        t = n - 1 - i
        rows = TS + 2 * HALO
        xpad[0:HALO, :] = jnp.where(t == 0, 0.0, xp_ref[...])
        xpad[HALO:HALO + TS, :] = xc_ref[...]
        xpad[HALO + TS:rows, :] = jnp.where(t == n - 1, 0.0, xn_ref[...])
        ng = L // HALO
        sub = lax.broadcasted_iota(jnp.int32, (1, HALO, 1), 1)

        def conv_rows(rb, carry):
            r0 = pl.multiple_of(rb * L, L)
            xg = xpad[pl.ds(r0, L + 2 * HALO), :].reshape(ng + 2, HALO, SSD_XBC)
            acc = cb_ref[...] + cw_ref[SSD_CONV // 2:SSD_CONV // 2 + 1, :] * xg[1:ng + 1]
            for j in range(SSD_CONV):
                d = j - SSD_CONV // 2
                if d > 0:
                    u = pltpu.roll(xg, HALO - d, 1)
                    acc = acc + cw_ref[j:j + 1, :] * jnp.where(sub < HALO - d, u[1:ng + 1], u[2:ng + 2])
                elif d < 0:
                    u = pltpu.roll(xg, -d, 1)
                    acc = acc + cw_ref[j:j + 1, :] * jnp.where(sub >= -d, u[1:ng + 1], u[0:ng])
            xa_ref[pl.ds(r0, L), :] = _silu(acc).reshape(L, SSD_XBC).astype(xa_ref.dtype)
            return carry

        lax.fori_loop(0, nchunk, conv_rows, 0)

    a_row = -jnp.exp(alog_ref[...])[:, d0:d0 + H]
    r_i = lax.broadcasted_iota(jnp.int32, (L, L), 0)
    c_i = lax.broadcasted_iota(jnp.int32, (L, L), 1)
    keep = (c_i >= r_i) if backward else (c_i <= r_i)
    tri = jnp.where(keep, 1.0, 0.0).astype(BF16)
    last = 0 if backward else L - 1
    lane = lax.broadcasted_iota(jnp.int32, (1, 2 * P), 1)

    css, dts, ecss, scales = [], [], [], []
    for c in range(nchunk):
        dt = jax.nn.softplus(dt_ref[c * L:(c + 1) * L, :] + dtb_ref[...])[:, d0:d0 + H]
        w_hi, w_lo = _split_hi_lo(dt * a_row)
        cs = _dot(tri, w_hi) + _dot(tri, w_lo)
        css.append(cs)
        dts.append(dt)
        ecss.append(jnp.exp(cs))
        scales.append(dt * jnp.exp(cs[last:last + 1, :] - cs))
    cs_all = jnp.concatenate(css, axis=0)
    dt_all = jnp.concatenate(dts, axis=0)
    csb_s[...] = _dot(_split3_tiled(cs_all), selb_ref[...])
    ex_s[...] = _dot(_split3_tiled(jnp.concatenate(ecss, axis=0)), selx_ref[...])
    sx_s[...] = _dot(_split3_tiled(jnp.concatenate(scales, axis=0)), selx_ref[...])
    rowT_s[...] = jnp.concatenate([cs_all, dt_all, jnp.zeros((TS, L - 2 * H), F32)], axis=1).T[0:2 * H]

    def chunk(ci, carry):
        c = (nchunk - 1 - ci) if backward else ci
        r0 = pl.multiple_of(c * L, L)
        csb = csb_s[pl.ds(r0, L), :]
        ex = ex_s[pl.ds(r0, L), :]
        sx = sx_s[pl.ds(r0, L), :]
        csT = rowT_s[0:H, pl.ds(r0, L)]
        dtT = rowT_s[H:2 * H, pl.ds(r0, L)]
        etx = ex[last:last + 1, :]
        xs = xa_ref[pl.ds(r0, L), 0:SSD_INNER]
        bm = xa_ref[pl.ds(r0, L), SSD_INNER:SSD_INNER + SSD_GROUPS * SSD_STATE]
        cm = xa_ref[pl.ds(r0, L), SSD_INNER + SSD_GROUPS * SSD_STATE:SSD_XBC]
        xw = (xs.astype(F32) * sx).astype(BF16)
        cbs, yoffs = [], []
        for g in range(SSD_GROUPS):
            bg = bm[:, g * SSD_STATE:(g + 1) * SSD_STATE]
            cg = cm[:, g * SSD_STATE:(g + 1) * SSD_STATE]
            cbs.append(_dot_nt(cg, bg))
            hg = hst[g]
            yoffs.append(_dot(cg, hg.astype(BF16)) * ex[:, g * gw:(g + 1) * gw])
            hst[g] = etx[:, g * gw:(g + 1) * gw] * hg + _dot_tn(bg, xw[:, g * gw:(g + 1) * gw])
        ms = []
        for k in range(H):
            seg = csb[:, k * L:(k + 1) * L] - csT[k:k + 1, :]
            dm = jnp.exp(jnp.where(keep, seg, NEG)) * dtT[k:k + 1, :]
            ms.append((cbs[k // (H // SSD_GROUPS)] * dm).astype(BF16))
        ys = []
        for j in range(H // 2):
            xp2 = xs[:, 2 * j * P:(2 * j + 2) * P]
            zero = jnp.zeros_like(xp2)
            rhs = jnp.concatenate([jnp.where(lane < P, xp2, zero), jnp.where(lane >= P, xp2, zero)], axis=0)
            ys.append(_dot(jnp.concatenate([ms[2 * j], ms[2 * j + 1]], axis=1), rhs))
        y = jnp.concatenate(ys, axis=1) + jnp.concatenate(yoffs, axis=1)
        if backward:
            y_ref[pl.ds(r0, L), :] = y
        else:
            y = y + yb_ref[pl.ds(r0, L), :] + dsk_ref[...] * xs.astype(F32)
            y = y * _silu(z_ref[pl.ds(r0, L), :])
            outs = []
            for g in range(SSD_GROUPS):
                yg = y[:, g * gw:(g + 1) * gw]
                outs.append(yg * lax.rsqrt(jnp.mean(yg * yg, axis=-1, keepdims=True) + EPS))
            y_ref[pl.ds(r0, L), :] = (jnp.concatenate(outs, axis=1) * nw_ref[...]).astype(y_ref.dtype)
        return carry

    lax.fori_loop(0, nchunk, chunk, 0)


def ssd_pass(backward, src, dt, alog, alogT, dtb, B, S, extra):
    T = B * S
    nS = S // TS
    per = TS // HALO
    nh2 = 2 * SSD_HEADS
    selb, selx = (jnp.asarray(m, BF16) for m in _ssd_constants())

    def blk(b, i):
        return b * nS + ((nS - 1 - i) if backward else i)

    row = lambda c: pl.BlockSpec((TS, c), lambda b, i: (blk(b, i), 0))
    dt_specs = [row(nh2)]
    par_specs = [_const_spec(alog.shape), _const_spec(alogT.shape), _const_spec(dtb.shape),
                 _const_spec(selb.shape), _const_spec(selx.shape)]
    state = [pltpu.VMEM((SSD_GROUPS, SSD_STATE, SSD_INNER // SSD_GROUPS), F32),
             pltpu.VMEM((TS, SSD_HEADS * SSD_L), F32), pltpu.VMEM((TS, SSD_INNER), F32),
             pltpu.VMEM((TS, SSD_INNER), F32), pltpu.VMEM((nh2, TS), F32)]
    if backward:
        cw, cb = extra
        in_specs = ([row(SSD_XBC),
                     pl.BlockSpec((HALO, SSD_XBC), lambda b, i: (jnp.maximum(blk(b, i) * per - 1, 0), 0)),
                     pl.BlockSpec((HALO, SSD_XBC),
                                  lambda b, i: (jnp.minimum((blk(b, i) + 1) * per, T // HALO - 1), 0))]
                    + dt_specs + [_const_spec(cw.shape), _const_spec(cb.shape)] + par_specs)
        args = [src, src, src, dt, cw, cb, alog, alogT, dtb, selb, selx]
        out_specs = [row(SSD_INNER), row(SSD_XBC)]
        out_shape = [jax.ShapeDtypeStruct((T, SSD_INNER), F32), jax.ShapeDtypeStruct((T, SSD_XBC), BF16)]
        scratch = [pltpu.VMEM((TS + 2 * HALO, SSD_XBC), F32)] + state
    else:
        yb, z, dsk, nw = extra
        in_specs = ([row(SSD_XBC)] + dt_specs + par_specs
                    + [row(SSD_INNER), row(SSD_INNER), _const_spec(dsk.shape), _const_spec(nw.shape)])
        args = [src, dt, alog, alogT, dtb, selb, selx, yb, z, dsk, nw]
        out_specs = row(SSD_INNER)
        out_shape = jax.ShapeDtypeStruct((T, SSD_INNER), BF16)
        scratch = state
    return pl.pallas_call(
        functools.partial(_ssd_body, backward),
        grid=(B, nS),
        in_specs=in_specs,
        out_specs=out_specs,
        out_shape=out_shape,
        scratch_shapes=scratch,
        compiler_params=_cparams("parallel", "arbitrary"),
        name="ssd_bwd" if backward else "ssd_fwd",
    )(*args)


def _hgrn_constants(backward):
    C, NB = HG_C, HG_NB
    t = np.arange(C)[:, None]
    u = np.arange(C)[None, :]
    if backward:
        t, u = C - 1 - t, C - 1 - u
    bt, bu = t // HG_SUB, u // HG_SUB
    cum = (bt == bu) & (u <= t)
    masks = [cum] + [bt - bu == d for d in range(1, NB)]
    return cum.astype(np.float32), np.stack(masks).astype(np.float32)


def _rows_to_tile(rows):
    return jnp.concatenate([jnp.broadcast_to(r, (HG_SUB, r.shape[1])) for r in rows], axis=0)


def _hgrn_body(backward, *refs):
    if backward:
        (q_ref, f_ref, v_ref, lb_ref, cum_ref, msk_ref, o_ref, st) = refs
    else:
        (q_ref, f_ref, v_ref, lb_ref, cum_ref, msk_ref, ob_ref, g_ref, nw_ref, o_ref, st) = refs
    i = pl.program_id(1)
    C, NB = HG_C, HG_NB
    nchunk = TS // C
    order = list(range(NB - 1, -1, -1)) if backward else list(range(NB))
    end_row = 0 if backward else HG_SUB - 1

    @pl.when(i == 0)
    def _():
        st[...] = jnp.zeros_like(st)

    def chunk(ci, carry):
        c = (nchunk - 1 - ci) if backward else ci
        r0 = pl.multiple_of(c * C, C)
        cum = cum_ref[...]
        heads = range(HG_HEADS)
        hsl = [slice(h * HG_K, (h + 1) * HG_K) for h in heads]
        ks, qv, vv, hl = [], [], [], []
        for h in heads:
            lb = lb_ref[:, hsl[h]]
            fr = f_ref[pl.ds(r0, C), hsl[h]]
            e = jnp.exp(-jnp.abs(fr))
            r = 1.0 / (1.0 + e)
            sg = jnp.where(fr >= 0, r, e * r)
            sgn = jnp.where(fr >= 0, e * r, r)
            logf = jnp.log(lb + (1.0 - lb) * sg)
            ks.append((1.0 - lb) * sgn)
            qv.append(_silu(q_ref[pl.ds(r0, C), hsl[h]]))
            vv.append(v_ref[pl.ds(r0, C), hsl[h]].astype(BF16))
            hl.append(jnp.concatenate(_split_hi_lo(logf), axis=1))
        gls = [_dot(cum, hl[h]) for h in heads]
        qt_all, qd_all, kt_all, kh_all, qin_all, kst_all, dec_all = [], [], [], [], [], [], []
        for h in heads:
            gl = gls[h][:, :HG_K] + gls[h][:, HG_K:]
            tot = [gl[j * HG_SUB + end_row:j * HG_SUB + end_row + 1, :] for j in range(NB)]
            zero = jnp.zeros_like(tot[0])
            pre, post, near = [None] * NB, [None] * NB, [[None] * NB for _ in range(NB)]
            for p, b in enumerate(order):
                pre[b] = sum((tot[order[pp]] for pp in range(p)), zero)
                post[b] = sum((tot[order[pp]] for pp in range(p + 1, NB)), zero)
                for d in range(2, NB):
                    near[d][b] = sum((tot[order[pp]] for pp in range(max(p - d + 1, 0), p)), zero)
            qt = (qv[h] * jnp.exp(gl)).astype(BF16)
            kt = (ks[h] * jnp.exp(-gl)).astype(BF16)
            bexp = lambda rows: _rows_to_tile([jnp.exp(x).astype(BF16) for x in rows])
            kh = kt * bexp(tot)
            qin_all.append(qt * bexp(pre))
            kst_all.append(kh * bexp(post))
            qd = []
            for d in range(1, NB):
                blocks = sorted(order[d:])
                rows = slice(blocks[0] * HG_SUB, (blocks[-1] + 1) * HG_SUB)
                qd.append(qt[rows] if d == 1 else qt[rows] * bexp([near[d][b] for b in blocks]))
            qd_all.append(jnp.concatenate(qd, axis=0))
            qt_all.append(qt)
            kt_all.append(kt)
            kh_all.append(kh)
            dec_all.append(jnp.exp(sum(tot, zero)))
        att0 = [_dot_nt(qt_all[h], kt_all[h]) for h in heads]
        offs = [_dot_nt(qd_all[h], kh_all[h]) for h in heads]
        lane_blk = lax.broadcasted_iota(jnp.int32, (HG_SUB, C), 1) // HG_SUB
        atts = []
        for h in heads:
            rows_out = []
            for p, b in enumerate(order):
                r = att0[h][b * HG_SUB:(b + 1) * HG_SUB] * msk_ref[0, b * HG_SUB:(b + 1) * HG_SUB, :]
                base = 0
                for d in range(1, NB):
                    blocks = sorted(order[d:])
                    if p >= d:
                        o0 = base + (b - blocks[0]) * HG_SUB
                        r = jnp.where(lane_blk == order[p - d], offs[h][o0:o0 + HG_SUB], r)
                    base += len(blocks) * HG_SUB
                rows_out.append((b, r))
            atts.append(jnp.concatenate([r for _, r in sorted(rows_out, key=lambda t: t[0])], axis=0).astype(BF16))
        sTs = [st[h] for h in heads]
        outs = [_dot(atts[h], vv[h]) + _dot_nt(qin_all[h], sTs[h].astype(BF16)) for h in heads]
        upd = [_dot_tn(vv[h], kst_all[h]) for h in heads]
        for h in heads:
            st[h] = sTs[h] * dec_all[h] + upd[h]
            o = outs[h]
            if backward:
                o_ref[pl.ds(r0, C), hsl[h]] = o
            else:
                o = o + ob_ref[pl.ds(r0, C), hsl[h]]
                o = o * lax.rsqrt(jnp.mean(o * o, axis=-1, keepdims=True) + EPS) * nw_ref[...]
                o_ref[pl.ds(r0, C), hsl[h]] = (o * _silu(g_ref[pl.ds(r0, C), hsl[h]])).astype(o_ref.dtype)
        return carry

    lax.fori_loop(0, nchunk, chunk, 0)


def hgrn_pass(backward, proj, lb, B, S, extra=()):
    T = B * S
    nS = S // TS
    cum, masks = _hgrn_constants(backward)
    cum = jnp.asarray(cum, BF16)
    masks = jnp.asarray(masks, F32)

    def blk(b, i):
        return b * nS + ((nS - 1 - i) if backward else i)

    col = lambda c: pl.BlockSpec((TS, D_MODEL), lambda b, i: (blk(b, i), c))
    in_specs = [col(0), col(2 if backward else 1), col(3), _const_spec(lb.shape), _const_spec(cum.shape),
                _const_spec(masks.shape)]
    args = [proj, proj, proj, lb, cum, masks]
    if not backward:
        ob, nw = extra
        in_specs += [col(0), col(4), _const_spec(nw.shape)]
        args += [ob, proj, nw]
    return pl.pallas_call(
        functools.partial(_hgrn_body, backward),
        grid=(B, nS),
        in_specs=in_specs,
        out_specs=pl.BlockSpec((TS, D_MODEL), lambda b, i: (blk(b, i), 0)),
        out_shape=jax.ShapeDtypeStruct((T, D_MODEL), F32 if backward else BF16),
        scratch_shapes=[pltpu.VMEM((HG_HEADS, HG_V, HG_K), F32)],
        compiler_params=_cparams("parallel", "arbitrary"),
        name="hgrn_bwd" if backward else "hgrn_fwd",
    )(*args)


def _t5_bucket(rel):
    half = T5_BUCKETS // 2
    max_exact = half // 2
    n = np.abs(rel)
    large = max_exact + (np.log(np.maximum(n, 1) / max_exact)
                         / np.log(T5_MAX_DIST / max_exact) * (half - max_exact)).astype(np.int32)
    large = np.minimum(large, half - 1)
    return ((rel > 0).astype(np.int32) * half + np.where(n < max_exact, n, large)).astype(np.int32)


def _attention_bias(t5_bias):
    qi = np.arange(A_BLOCK)[None, :]
    kj = np.arange(3 * A_BLOCK)[:, None] - A_BLOCK
    rel = kj - qi
    onehot = (_t5_bucket(rel)[None] == np.arange(T5_BUCKETS)[:, None, None]).astype(np.float32)
    bias = jnp.einsum("bh,bkq->hkq", t5_bias.astype(F32), onehot, precision=lax.Precision.HIGHEST)
    return jnp.where(np.abs(rel)[None] <= A_BLOCK, bias, NEG)


def _trunk(x, B, S, p):
    for l in range(DEPTH):
        j = l // 2
        g = p["gains"][l]
        if l % 2 == 0:
            e = p["even"][j]
            qkv, z, xbc, dt = in_proj_even(x, g[0:1], e["wqkv"], e["wz"], e["wxbc"], e["wdt"])
            a_out = windowed_attention(qkv, p["attn_bias"], e["sink"], B, S)
            common = (dt, e["a_log"], e["a_logT"], e["dt_bias"], B, S)
            yb, xact = ssd_pass(True, xbc, *common, extra=(e["conv_w"], e["conv_b"]))
            b_out = ssd_pass(False, xact, *common, extra=(yb, z, e["d_skip"], e["norm_w"]))
            mixes, wouts = [a_out, b_out], [e["wout_a"], e["wout_b"]]
        else:
            o = p["odd"][j]
            proj = in_proj_odd(x, g[0:1], o["w_in"])
            ob = hgrn_pass(True, proj, o["lb"], B, S)
            og = hgrn_pass(False, proj, o["lb"], B, S, extra=(ob, o["norm_w"]))
            mixes, wouts = [og], [o["w_out"]]
        f = p["ffn"][l]
        x = out_proj_ffn(mixes, wouts, x, g[1:4], f["wg"], f["wu"], f["wd"])
    return x


def kernel(x_prompt, x_sample, norm_gains, t5_bias, ev_w_in, attn_sink, ssd_conv_w, ssd_conv_b, ssd_a_log,
           ssd_dt_bias, ssd_d, ssd_norm_w, ev_w_out, od_w_in, hg_lower_bounds, hg_norm_w, od_w_out,
           ffn_w_gate, ffn_w_up, ffn_w_down):
    nh2 = 2 * SSD_HEADS
    c_qkv = A_Q_DIM + 2 * A_KV_DIM
    c_z = c_qkv + SSD_INNER
    c_xbc = c_z + SSD_XBC
    lb_soft = jax.nn.softmax(hg_lower_bounds.astype(F32), axis=0)
    lb_all = jnp.cumsum(lb_soft, axis=0) - lb_soft[0]
    p = {"gains": norm_gains.astype(F32), "attn_bias": _attention_bias(t5_bias), "even": [], "odd": [], "ffn": []}
    for j in range(ev_w_in.shape[0]):
        w = ev_w_in[j]
        p["even"].append(dict(
            wqkv=w[:, :c_qkv].astype(BF16), wz=w[:, c_qkv:c_z].astype(BF16), wxbc=w[:, c_z:c_xbc].astype(BF16),
            wdt=w[:, c_xbc:].astype(BF16),
            sink=attn_sink[j].astype(F32).reshape(1, A_HEADS),
            conv_w=ssd_conv_w[j].astype(F32), conv_b=ssd_conv_b[j].astype(F32).reshape(1, SSD_XBC),
            a_log=ssd_a_log[j].astype(F32).reshape(1, nh2), a_logT=ssd_a_log[j].astype(F32).reshape(nh2, 1),
            dt_bias=ssd_dt_bias[j].astype(F32).reshape(1, nh2),
            d_skip=jnp.repeat(ssd_d[j].astype(F32), SSD_HEAD_DIM).reshape(1, SSD_INNER),
            norm_w=ssd_norm_w[j].astype(F32).reshape(1, SSD_INNER),
            wout_a=ev_w_out[j, :A_Q_DIM].astype(BF16), wout_b=ev_w_out[j, A_Q_DIM:].astype(BF16)))
    for j in range(od_w_in.shape[0]):
        p["odd"].append(dict(
            w_in=od_w_in[j].astype(BF16), lb=lb_all[j].reshape(1, HG_HEADS * HG_K),
            norm_w=hg_norm_w[j].astype(F32).reshape(1, HG_V), w_out=od_w_out[j].astype(BF16)))
    for l in range(DEPTH):
        p["ffn"].append(dict(wg=ffn_w_gate[l].astype(BF16), wu=ffn_w_up[l].astype(BF16),
                             wd=ffn_w_down[l].astype(BF16)))
    outs = []
    for x in (x_prompt, x_sample):
        B, S, _ = x.shape
        outs.append(_trunk(x.reshape(B * S, D_MODEL), B, S, p).reshape(B, S, D_MODEL))
    return tuple(outs)
```

```python
import functools
import math

import numpy as np
import jax
import jax.numpy as jnp
from jax import lax
from jax.experimental import pallas as pl
from jax.experimental.pallas import tpu as pltpu

F32 = jnp.float32
BF16 = jnp.bfloat16

D_MODEL = 1024
DEPTH = 4
EPS = 1e-6
NEG = -1e30

A_HEADS = 8
A_KV_HEADS = 2
A_HEAD_DIM = 64
A_BLOCK = 128
A_Q_DIM = A_HEADS * A_HEAD_DIM
A_KV_DIM = A_KV_HEADS * A_HEAD_DIM
T5_BUCKETS = 32
T5_MAX_DIST = 128

SSD_HEADS = 16
SSD_HEAD_DIM = 64
SSD_INNER = SSD_HEADS * SSD_HEAD_DIM
SSD_GROUPS = 2
SSD_STATE = 64
SSD_CONV = 5
SSD_XBC = SSD_INNER + 2 * SSD_GROUPS * SSD_STATE
SSD_L = 128
HALO = 8

HG_HEADS = 8
HG_K = 128
HG_V = 128
HG_C = 128
HG_SUB = 32
HG_NB = HG_C // HG_SUB

D_FF = 2816
MXU_N = 256
FF_CHUNKS = ((0, 6 * MXU_N), (6 * MXU_N, D_FF))

TM = 512
TS = 512
VMEM_LIMIT_BYTES = 56 * 1024 * 1024


def _cparams(*sem):
    return pltpu.CompilerParams(dimension_semantics=sem, vmem_limit_bytes=VMEM_LIMIT_BYTES)


def _const_spec(shape):
    nd = len(shape)
    return pl.BlockSpec(shape, lambda *_: (0,) * nd, pipeline_mode=pl.Buffered(1))


def _rms(x, g):
    return x * lax.rsqrt(jnp.mean(x * x, axis=-1, keepdims=True) + EPS) * g


def _silu(x):
    return x * (1.0 / (1.0 + jnp.exp(-x)))


def _dot(a, b):
    return jnp.dot(a, b, preferred_element_type=F32)


def _dot_nt(a, b):
    return lax.dot_general(a, b, (((1,), (1,)), ((), ())), preferred_element_type=F32)


def _dot_tn(a, b):
    return lax.dot_general(a, b, (((0,), (0,)), ((), ())), preferred_element_type=F32)


def _split_hi_lo(x):
    hi = x.astype(BF16)
    lo = (x - hi.astype(F32)).astype(BF16)
    return hi, lo


def _in_even_body(x_ref, g_ref, wqkv_ref, wz_ref, wxbc_ref, wdt_ref, qkv_ref, z_ref, xbc_ref, dt_ref):
    xn = _rms(x_ref[...], g_ref[...]).astype(BF16)
    qkv_ref[...] = _dot(xn, wqkv_ref[...]).astype(BF16)
    z_ref[...] = _dot(xn, wz_ref[...])
    xbc_ref[...] = _dot(xn, wxbc_ref[...])
    dt_ref[...] = _dot(xn, wdt_ref[...])


def in_proj_even(x, g, wqkv, wz, wxbc, wdt):
    T = x.shape[0]
    nh2 = 2 * SSD_HEADS
    row = lambda c: pl.BlockSpec((TM, c), lambda i: (i, 0))
    return pl.pallas_call(
        _in_even_body,
        grid=(T // TM,),
        in_specs=[row(D_MODEL), _const_spec((1, D_MODEL)), _const_spec(wqkv.shape), _const_spec(wz.shape),
                  _const_spec(wxbc.shape), _const_spec(wdt.shape)],
        out_specs=[row(A_Q_DIM + 2 * A_KV_DIM), row(SSD_INNER), row(SSD_XBC), row(nh2)],
        out_shape=[jax.ShapeDtypeStruct((T, A_Q_DIM + 2 * A_KV_DIM), BF16),
                   jax.ShapeDtypeStruct((T, SSD_INNER), F32),
                   jax.ShapeDtypeStruct((T, SSD_XBC), F32),
                   jax.ShapeDtypeStruct((T, nh2), F32)],
        compiler_params=_cparams("parallel"),
        name="in_proj_even",
    )(x, g, wqkv, wz, wxbc, wdt)


def _in_odd_body(x_ref, g_ref, lb_ref, w_ref, q_ref, kf_ref, kb_ref, lfh_ref, lfl_ref, lbh_ref, lbl_ref, v_ref,
                 og_ref):
    xn = _rms(x_ref[...], g_ref[...]).astype(BF16)
    W = D_MODEL
    seg = lambda j: _dot(xn, w_ref[:, j * W:(j + 1) * W])
    q_ref[...] = _silu(seg(0)).astype(BF16)
    lb = lb_ref[...]
    for j, k_ref, hi_ref, lo_ref in ((1, kf_ref, lfh_ref, lfl_ref), (2, kb_ref, lbh_ref, lbl_ref)):
        sg = 1.0 / (1.0 + jnp.exp(-seg(j)))
        k_ref[...] = ((1.0 - lb) * (1.0 - sg)).astype(BF16)
        hi_ref[...], lo_ref[...] = _split_hi_lo(jnp.log(lb + (1.0 - lb) * sg))
    v_ref[...] = seg(3).astype(BF16)
    og_ref[...] = _silu(seg(4)).astype(BF16)


def in_proj_odd(x, g, lb, w):
    T = x.shape[0]
    row = pl.BlockSpec((TM, D_MODEL), lambda i: (i, 0))
    n_out = 9
    return pl.pallas_call(
        _in_odd_body,
        grid=(T // TM,),
        in_specs=[row, _const_spec((1, D_MODEL)), _const_spec(lb.shape), _const_spec(w.shape)],
        out_specs=[row] * n_out,
        out_shape=[jax.ShapeDtypeStruct((T, D_MODEL), BF16)] * n_out,
        compiler_params=_cparams("parallel"),
        name="in_proj_odd",
    )(x, g, lb, w)


def _out_ffn_body(n_mix, *refs):
    mix_refs = refs[:n_mix]
    wout_refs = refs[n_mix:2 * n_mix]
    x_ref, g_ref, wg_ref, wu_ref, wd_ref, o_ref = refs[2 * n_mix:]
    halves = [slice(0, TM // 2), slice(TM // 2, TM)]
    ms = []
    for r in halves:
        m = _dot(mix_refs[0][r, :], wout_refs[0][...])
        for a_ref, w_ref in zip(mix_refs[1:], wout_refs[1:]):
            m = m + _dot(a_ref[r, :], w_ref[...])
        ms.append(m)
    x1s = [x_ref[r, :] + _rms(m, g_ref[0:1, :]) for r, m in zip(halves, ms)]
    hns = [_rms(x1, g_ref[1:2, :]).astype(BF16) for x1 in x1s]
    fs = [None] * len(halves)
    for c0, c1 in FF_CHUNKS:
        gates = [_dot(hn, wg_ref[:, c0:c1]) for hn in hns]
        ups = [_dot(hn, wu_ref[:, c0:c1]) for hn in hns]
        hs = [(_silu(gate) * up).astype(BF16) for gate, up in zip(gates, ups)]
        parts = [_dot(h, wd_ref[c0:c1, :]) for h in hs]
        fs = [part if f is None else f + part for f, part in zip(fs, parts)]
    for r, x1, f in zip(halves, x1s, fs):
        o_ref[r, :] = x1 + _rms(f, g_ref[2:3, :])


def out_proj_ffn(mixes, wouts, x, g3, wg, wu, wd):
    T = x.shape[0]
    n_mix = len(mixes)
    row = lambda c: pl.BlockSpec((TM, c), lambda i: (i, 0))
    in_specs = ([row(a.shape[1]) for a in mixes] + [_const_spec(w.shape) for w in wouts]
                + [row(D_MODEL), _const_spec(g3.shape), _const_spec(wg.shape), _const_spec(wu.shape),
                   _const_spec(wd.shape)])
    return pl.pallas_call(
        functools.partial(_out_ffn_body, n_mix),
        grid=(T // TM,),
        in_specs=in_specs,
        out_specs=row(D_MODEL),
        out_shape=jax.ShapeDtypeStruct((T, D_MODEL), F32),
        compiler_params=_cparams("parallel"),
        name="out_proj_ffn",
    )(*mixes, *wouts, x, g3, wg, wu, wd)


def _attn_body(q_ref, kp_ref, kc_ref, kn_ref, vp_ref, vc_ref, vn_ref, bias_ref, sink_ref, o_ref, kbuf, vtbuf):
    i = pl.program_id(1)
    n = pl.num_programs(1)
    nsub = TS // A_BLOCK
    dh = A_HEAD_DIM
    nk = TS + 2 * A_BLOCK
    row = lax.broadcasted_iota(jnp.int32, (2 * dh, nk), 0)
    for g in range(A_KV_HEADS):
        vs = []
        for lo, k_ref, v_ref in ((0, kp_ref, vp_ref), (A_BLOCK, kc_ref, vc_ref), (A_BLOCK + TS, kn_ref, vn_ref)):
            kbuf[g, lo:lo + k_ref.shape[0], :] = k_ref[:, g * dh:(g + 1) * dh]
            vs.append(v_ref[...])
        vt = jnp.concatenate(vs, axis=0).astype(F32).T[g * dh:(g + 1) * dh]
        vt = jnp.concatenate([vt, jnp.zeros_like(vt)], axis=0)
        vtbuf[g] = jnp.where(row == dh, 1.0, vt).astype(BF16)
    key = lax.broadcasted_iota(jnp.int32, (3 * A_BLOCK, A_BLOCK), 0)
    rep = A_HEADS // A_KV_HEADS
    heads = range(A_HEADS)

    def scores(sb):
        r0 = sb * A_BLOCK
        return [_dot_nt(kbuf[h // rep, r0:r0 + 3 * A_BLOCK, :], q_ref[r0:r0 + A_BLOCK, h * dh:(h + 1) * dh])
                for h in heads]

    def softmax_numerators(sb, ss):
        edge = None
        if sb == 0:
            edge = jnp.where(key < jnp.where(i == 0, A_BLOCK, 0), NEG, 0.0)
        if sb == nsub - 1:
            e2 = jnp.where(key >= jnp.where(i == n - 1, 2 * A_BLOCK, 3 * A_BLOCK), NEG, 0.0)
            edge = e2 if edge is None else edge + e2
        ps, ms = [], []
        for h in heads:
            s = ss[h] * (dh ** -0.5) + bias_ref[h]
            if edge is not None:
                s = s + edge
            m = jnp.maximum(jnp.max(s, axis=0, keepdims=True), sink_ref[0:1, h:h + 1])
            ps.append(jnp.exp(s - m).astype(BF16))
            ms.append(m)
        return ps, ms

    def weighted_values(sb, ps):
        r0 = sb * A_BLOCK
        return [_dot(vtbuf[h // rep, :, r0:r0 + 3 * A_BLOCK], ps[h]) for h in heads]

    def finish(sb, pv, ms):
        r0 = sb * A_BLOCK
        outs = []
        for h in heads:
            denom = pv[h][dh:dh + 1, :] + jnp.exp(sink_ref[0:1, h:h + 1] - ms[h])
            outs.append(pv[h][:dh, :] / denom)
        o_ref[r0:r0 + A_BLOCK, :] = jnp.concatenate(outs, axis=0).T.astype(o_ref.dtype)

    ss, pm, pv = {0: scores(0)}, {}, {}
    for sb in range(nsub + 1):
        if sb + 1 < nsub:
            ss[sb + 1] = scores(sb + 1)
        if sb < nsub:
            pm[sb] = softmax_numerators(sb, ss.pop(sb))
            pv[sb] = weighted_values(sb, pm[sb][0])
        if sb >= 1:
            finish(sb - 1, pv.pop(sb - 1), pm.pop(sb - 1)[1])


def windowed_attention(qkv, bias, sink, B, S):
    T = B * S
    nS = S // TS
    sub = TS // A_BLOCK
    nblk = T // A_BLOCK
    kcol = A_Q_DIM // A_KV_DIM
    vcol = kcol + 1
    cur = lambda c: pl.BlockSpec((TS, A_KV_DIM), lambda b, i: (b * nS + i, c))
    prev = lambda c: pl.BlockSpec((A_BLOCK, A_KV_DIM),
                                  lambda b, i: (jnp.maximum((b * nS + i) * sub - 1, 0), c))
    nxt = lambda c: pl.BlockSpec((A_BLOCK, A_KV_DIM),
                                 lambda b, i: (jnp.minimum((b * nS + i + 1) * sub, nblk - 1), c))
    return pl.pallas_call(
        _attn_body,
        grid=(B, nS),
        in_specs=[pl.BlockSpec((TS, A_Q_DIM), lambda b, i: (b * nS + i, 0)),
                  prev(kcol), cur(kcol), nxt(kcol), prev(vcol), cur(vcol), nxt(vcol),
                  _const_spec(bias.shape), _const_spec(sink.shape)],
        out_specs=pl.BlockSpec((TS, A_Q_DIM), lambda b, i: (b * nS + i, 0)),
        out_shape=jax.ShapeDtypeStruct((T, A_Q_DIM), BF16),
        scratch_shapes=[pltpu.VMEM((A_KV_HEADS, TS + 2 * A_BLOCK, A_HEAD_DIM), BF16),
                        pltpu.VMEM((A_KV_HEADS, 2 * A_HEAD_DIM, TS + 2 * A_BLOCK), BF16)],
        compiler_params=_cparams("parallel", "parallel"),
        name="windowed_attention",
    )(qkv, qkv, qkv, qkv, qkv, qkv, qkv, bias, sink)


def _split3_tiled(x):
    h = x.shape[1]
    t = jnp.concatenate([x, x, x], axis=1)
    hi = t.astype(BF16).astype(F32)
    r1 = t - hi
    mid = r1.astype(BF16).astype(F32)
    lane = lax.broadcasted_iota(jnp.int32, t.shape, 1)
    return jnp.where(lane < h, hi, jnp.where(lane < 2 * h, mid, r1 - mid)).astype(BF16)


def _ssd_constants():
    H, P, L = SSD_HEADS, SSD_HEAD_DIM, SSD_L
    selb = np.zeros((3 * H, H * L), np.float32)
    selx = np.zeros((3 * H, H * P), np.float32)
    for k in range(H):
        for part in range(3):
            selb[part * H + k, k * L:(k + 1) * L] = 1.0
            selx[part * H + k, k * P:(k + 1) * P] = 1.0
    return selb, selx


def _ssd_body(backward, *refs):
    if backward:
        (xc_ref, xp_ref, xn_ref, dt_ref, cw_ref, cb_ref, alog_ref, alogT_ref, dtb_ref,
         selb_ref, selx_ref, y_ref, xa_ref, xpad, hst, csb_s, ex_s, sx_s, rowT_s) = refs
    else:
        (xa_ref, dt_ref, alog_ref, alogT_ref, dtb_ref, selb_ref, selx_ref,
         yb_ref, z_ref, dsk_ref, nw_ref, y_ref, hst, csb_s, ex_s, sx_s, rowT_s) = refs
    i = pl.program_id(1)
    n = pl.num_programs(1)
    L = SSD_L
    H = SSD_HEADS
    P = SSD_HEAD_DIM
    nchunk = TS // L
    d0 = H if backward else 0
    gw = SSD_INNER // SSD_GROUPS

    @pl.when(i == 0)
    def _():
        hst[...] = jnp.zeros_like(hst)

    if backward:
        t = n - 1 - i
        rows = TS + 2 * HALO
        xpad[0:HALO, :] = jnp.where(t == 0, 0.0, xp_ref[...])
        xpad[HALO:HALO + TS, :] = xc_ref[...]
        xpad[HALO + TS:rows, :] = jnp.where(t == n - 1, 0.0, xn_ref[...])
        ng = L // HALO
        sub = lax.broadcasted_iota(jnp.int32, (1, HALO, 1), 1)

        def conv_rows(rb, carry):
            r0 = pl.multiple_of(rb * L, L)
            xg = xpad[pl.ds(r0, L + 2 * HALO), :].reshape(ng + 2, HALO, SSD_XBC)
            acc = cb_ref[...] + cw_ref[SSD_CONV // 2:SSD_CONV // 2 + 1, :] * xg[1:ng + 1]
            for j in range(SSD_CONV):
                d = j - SSD_CONV // 2
                if d > 0:
                    u = pltpu.roll(xg, HALO - d, 1)
                    acc = acc + cw_ref[j:j + 1, :] * jnp.where(sub < HALO - d, u[1:ng + 1], u[2:ng + 2])
                elif d < 0:
                    u = pltpu.roll(xg, -d, 1)
                    acc = acc + cw_ref[j:j + 1, :] * jnp.where(sub >= -d, u[1:ng + 1], u[0:ng])
            xa_ref[pl.ds(r0, L), :] = _silu(acc).reshape(L, SSD_XBC).astype(xa_ref.dtype)
            return carry

        lax.fori_loop(0, nchunk, conv_rows, 0)

    a_row = -jnp.exp(alog_ref[...])[:, d0:d0 + H]
    r_i = lax.broadcasted_iota(jnp.int32, (L, L), 0)
    c_i = lax.broadcasted_iota(jnp.int32, (L, L), 1)
    keep = (c_i >= r_i) if backward else (c_i <= r_i)
    tri = jnp.where(keep, 1.0, 0.0).astype(BF16)
    last = 0 if backward else L - 1
    lane = lax.broadcasted_iota(jnp.int32, (1, 2 * P), 1)

    css, dts, ecss, scales = [], [], [], []
    for c in range(nchunk):
        dt = jax.nn.softplus(dt_ref[c * L:(c + 1) * L, :] + dtb_ref[...])[:, d0:d0 + H]
        w_hi, w_lo = _split_hi_lo(dt * a_row)
        cs = _dot(tri, w_hi) + _dot(tri, w_lo)
        css.append(cs)
        dts.append(dt)
        ecss.append(jnp.exp(cs))
        scales.append(dt * jnp.exp(cs[last:last + 1, :] - cs))
    cs_all = jnp.concatenate(css, axis=0)
    dt_all = jnp.concatenate(dts, axis=0)
    csb_s[...] = _dot(_split3_tiled(cs_all), selb_ref[...])
    ex_s[...] = _dot(_split3_tiled(jnp.concatenate(ecss, axis=0)), selx_ref[...])
    sx_s[...] = _dot(_split3_tiled(jnp.concatenate(scales, axis=0)), selx_ref[...])
    rowT_s[...] = jnp.concatenate([cs_all, dt_all, jnp.zeros((TS, L - 2 * H), F32)], axis=1).T[0:2 * H]

    def chunk(ci, carry):
        c = (nchunk - 1 - ci) if backward else ci
        r0 = pl.multiple_of(c * L, L)
        csb = csb_s[pl.ds(r0, L), :]
        ex = ex_s[pl.ds(r0, L), :]
        sx = sx_s[pl.ds(r0, L), :]
        csT = rowT_s[0:H, pl.ds(r0, L)]
        dtT = rowT_s[H:2 * H, pl.ds(r0, L)]
        etx = ex[last:last + 1, :]
        xs = xa_ref[pl.ds(r0, L), 0:SSD_INNER]
        bm = xa_ref[pl.ds(r0, L), SSD_INNER:SSD_INNER + SSD_GROUPS * SSD_STATE]
        cm = xa_ref[pl.ds(r0, L), SSD_INNER + SSD_GROUPS * SSD_STATE:SSD_XBC]
        xw = (xs.astype(F32) * sx).astype(BF16)
        cbs, yoffs = [], []
        for g in range(SSD_GROUPS):
            bg = bm[:, g * SSD_STATE:(g + 1) * SSD_STATE]
            cg = cm[:, g * SSD_STATE:(g + 1) * SSD_STATE]
            cbs.append(_dot_nt(cg, bg))
            hg = hst[g]
            yoffs.append(_dot(cg, hg.astype(BF16)) * ex[:, g * gw:(g + 1) * gw])
            hst[g] = etx[:, g * gw:(g + 1) * gw] * hg + _dot_tn(bg, xw[:, g * gw:(g + 1) * gw])
        ms = []
        for k in range(H):
            seg = csb[:, k * L:(k + 1) * L] - csT[k:k + 1, :]
            dm = jnp.exp(jnp.where(keep, seg, NEG)) * dtT[k:k + 1, :]
            ms.append((cbs[k // (H // SSD_GROUPS)] * dm).astype(BF16))
        ys = []
        for j in range(H // 2):
            xp2 = xs[:, 2 * j * P:(2 * j + 2) * P]
            zero = jnp.zeros_like(xp2)
            rhs = jnp.concatenate([jnp.where(lane < P, xp2, zero), jnp.where(lane >= P, xp2, zero)], axis=0)
            ys.append(_dot(jnp.concatenate([ms[2 * j], ms[2 * j + 1]], axis=1), rhs))
        y = jnp.concatenate(ys, axis=1) + jnp.concatenate(yoffs, axis=1)
        if backward:
            y_ref[pl.ds(r0, L), :] = y
        else:
            y = y + yb_ref[pl.ds(r0, L), :] + dsk_ref[...] * xs.astype(F32)
            y = y * _silu(z_ref[pl.ds(r0, L), :])
            outs = []
            for g in range(SSD_GROUPS):
                yg = y[:, g * gw:(g + 1) * gw]
                outs.append(yg * lax.rsqrt(jnp.mean(yg * yg, axis=-1, keepdims=True) + EPS))
            y_ref[pl.ds(r0, L), :] = (jnp.concatenate(outs, axis=1) * nw_ref[...]).astype(y_ref.dtype)
        return carry

    lax.fori_loop(0, nchunk, chunk, 0)


def ssd_pass(backward, src, dt, alog, alogT, dtb, B, S, extra):
    T = B * S
    nS = S // TS
    per = TS // HALO
    nh2 = 2 * SSD_HEADS
    selb, selx = (jnp.asarray(m, BF16) for m in _ssd_constants())

    def blk(b, i):
        return b * nS + ((nS - 1 - i) if backward else i)

    row = lambda c: pl.BlockSpec((TS, c), lambda b, i: (blk(b, i), 0))
    dt_specs = [row(nh2)]
    par_specs = [_const_spec(alog.shape), _const_spec(alogT.shape), _const_spec(dtb.shape),
                 _const_spec(selb.shape), _const_spec(selx.shape)]
    state = [pltpu.VMEM((SSD_GROUPS, SSD_STATE, SSD_INNER // SSD_GROUPS), F32),
             pltpu.VMEM((TS, SSD_HEADS * SSD_L), F32), pltpu.VMEM((TS, SSD_INNER), F32),
             pltpu.VMEM((TS, SSD_INNER), F32), pltpu.VMEM((nh2, TS), F32)]
    if backward:
        cw, cb = extra
        in_specs = ([row(SSD_XBC),
                     pl.BlockSpec((HALO, SSD_XBC), lambda b, i: (jnp.maximum(blk(b, i) * per - 1, 0), 0)),
                     pl.BlockSpec((HALO, SSD_XBC),
                                  lambda b, i: (jnp.minimum((blk(b, i) + 1) * per, T // HALO - 1), 0))]
                    + dt_specs + [_const_spec(cw.shape), _const_spec(cb.shape)] + par_specs)
        args = [src, src, src, dt, cw, cb, alog, alogT, dtb, selb, selx]
        out_specs = [row(SSD_INNER), row(SSD_XBC)]
        out_shape = [jax.ShapeDtypeStruct((T, SSD_INNER), F32), jax.ShapeDtypeStruct((T, SSD_XBC), BF16)]
        scratch = [pltpu.VMEM((TS + 2 * HALO, SSD_XBC), F32)] + state
    else:
        yb, z, dsk, nw = extra
        in_specs = ([row(SSD_XBC)] + dt_specs + par_specs
                    + [row(SSD_INNER), row(SSD_INNER), _const_spec(dsk.shape), _const_spec(nw.shape)])
        args = [src, dt, alog, alogT, dtb, selb, selx, yb, z, dsk, nw]
        out_specs = row(SSD_INNER)
        out_shape = jax.ShapeDtypeStruct((T, SSD_INNER), BF16)
        scratch = state
    return pl.pallas_call(
        functools.partial(_ssd_body, backward),
        grid=(B, nS),
        in_specs=in_specs,
        out_specs=out_specs,
        out_shape=out_shape,
        scratch_shapes=scratch,
        compiler_params=_cparams("parallel", "arbitrary"),
        name="ssd_bwd" if backward else "ssd_fwd",
    )(*args)


def _hgrn_constants(backward):
    C, NB = HG_C, HG_NB
    t = np.arange(C)[:, None]
    u = np.arange(C)[None, :]
    if backward:
        t, u = C - 1 - t, C - 1 - u
    bt, bu = t // HG_SUB, u // HG_SUB
    cum = (bt == bu) & (u <= t)
    masks = [cum] + [bt - bu == d for d in range(1, NB)]
    return cum.astype(np.float32), np.stack(masks).astype(np.float32)


def _rows_to_tile(rows):
    return jnp.concatenate([jnp.broadcast_to(r, (HG_SUB, r.shape[1])) for r in rows], axis=0)


def _hgrn_body(backward, *refs):
    if backward:
        (q_ref, k_ref, lh_ref, ll_ref, v_ref, cum_ref, msk_ref, o_ref, st) = refs
    else:
        (q_ref, k_ref, lh_ref, ll_ref, v_ref, cum_ref, msk_ref, ob_ref, g_ref, nw_ref, o_ref, st) = refs
    i = pl.program_id(1)
    C, NB = HG_C, HG_NB
    nchunk = TS // C
    order = list(range(NB - 1, -1, -1)) if backward else list(range(NB))
    end_row = 0 if backward else HG_SUB - 1

    @pl.when(i == 0)
    def _():
        st[...] = jnp.zeros_like(st)

    def chunk(ci, carry):
        c = (nchunk - 1 - ci) if backward else ci
        r0 = pl.multiple_of(c * C, C)
        cum = cum_ref[...]
        heads = range(HG_HEADS)
        hsl = [slice(h * HG_K, (h + 1) * HG_K) for h in heads]
        ks = [k_ref[pl.ds(r0, C), hsl[h]].astype(F32) for h in heads]
        qv = [q_ref[pl.ds(r0, C), hsl[h]].astype(F32) for h in heads]
        vv = [v_ref[pl.ds(r0, C), hsl[h]] for h in heads]
        hl = [jnp.concatenate([lh_ref[pl.ds(r0, C), hsl[h]], ll_ref[pl.ds(r0, C), hsl[h]]], axis=1)
              for h in heads]
        gls = [_dot(cum, hl[h]) for h in heads]
        qt_all, qd_all, kt_all, kh_all, qin_all, kst_all, dec_all = [], [], [], [], [], [], []
        for h in heads:
            gl = gls[h][:, :HG_K] + gls[h][:, HG_K:]
            tot = [gl[j * HG_SUB + end_row:j * HG_SUB + end_row + 1, :] for j in range(NB)]
            zero = jnp.zeros_like(tot[0])
            pre, post, near = [None] * NB, [None] * NB, [[None] * NB for _ in range(NB)]
            for p, b in enumerate(order):
                pre[b] = sum((tot[order[pp]] for pp in range(p)), zero)
                post[b] = sum((tot[order[pp]] for pp in range(p + 1, NB)), zero)
                for d in range(2, NB):
                    near[d][b] = sum((tot[order[pp]] for pp in range(max(p - d + 1, 0), p)), zero)
            qt = (qv[h] * jnp.exp(gl)).astype(BF16)
            kt = (ks[h] * jnp.exp(-gl)).astype(BF16)
            bexp = lambda rows: _rows_to_tile([jnp.exp(x).astype(BF16) for x in rows])
            kh = kt * bexp(tot)
            qin_all.append(qt * bexp(pre))
            kst_all.append(kh * bexp(post))
            qd = []
            for d in range(1, NB):
                blocks = sorted(order[d:])
                rows = slice(blocks[0] * HG_SUB, (blocks[-1] + 1) * HG_SUB)
                qd.append(qt[rows] if d == 1 else qt[rows] * bexp([near[d][b] for b in blocks]))
            qd_all.append(jnp.concatenate(qd, axis=0))
            qt_all.append(qt)
            kt_all.append(kt)
            kh_all.append(kh)
            dec_all.append(jnp.exp(sum(tot, zero)))
        att0 = [_dot_nt(qt_all[h], kt_all[h]) for h in heads]
        offs = [_dot_nt(qd_all[h], kh_all[h]) for h in heads]
        lane_blk = lax.broadcasted_iota(jnp.int32, (HG_SUB, C), 1) // HG_SUB
        atts = []
        for h in heads:
            rows_out = []
            for p, b in enumerate(order):
                r = att0[h][b * HG_SUB:(b + 1) * HG_SUB] * msk_ref[0, b * HG_SUB:(b + 1) * HG_SUB, :]
                base = 0
                for d in range(1, NB):
                    blocks = sorted(order[d:])
                    if p >= d:
                        o0 = base + (b - blocks[0]) * HG_SUB
                        r = jnp.where(lane_blk == order[p - d], offs[h][o0:o0 + HG_SUB], r)
                    base += len(blocks) * HG_SUB
                rows_out.append((b, r))
            atts.append(jnp.concatenate([r for _, r in sorted(rows_out, key=lambda t: t[0])], axis=0).astype(BF16))
        sTs = [st[h] for h in heads]
        outs = [_dot(atts[h], vv[h]) + _dot_nt(qin_all[h], sTs[h].astype(BF16)) for h in heads]
        upd = [_dot_tn(vv[h], kst_all[h]) for h in heads]
        for h in heads:
            st[h] = sTs[h] * dec_all[h] + upd[h]
            o = outs[h]
            if backward:
                o_ref[pl.ds(r0, C), hsl[h]] = o.astype(o_ref.dtype)
            else:
                o = o + ob_ref[pl.ds(r0, C), hsl[h]].astype(F32)
                o = o * lax.rsqrt(jnp.mean(o * o, axis=-1, keepdims=True) + EPS) * nw_ref[...]
                o_ref[pl.ds(r0, C), hsl[h]] = (o * g_ref[pl.ds(r0, C), hsl[h]].astype(F32)).astype(o_ref.dtype)
        return carry

    lax.fori_loop(0, nchunk, chunk, 0)


def hgrn_pass(backward, q, k, lh, ll, v, B, S, extra=()):
    T = B * S
    nS = S // TS
    cum, masks = _hgrn_constants(backward)
    cum = jnp.asarray(cum, BF16)
    masks = jnp.asarray(masks, F32)

    def blk(b, i):
        return b * nS + ((nS - 1 - i) if backward else i)

    row = lambda c: pl.BlockSpec((TS, c), lambda b, i: (blk(b, i), 0))
    in_specs = [row(D_MODEL)] * 5 + [_const_spec(cum.shape), _const_spec(masks.shape)]
    args = [q, k, lh, ll, v, cum, masks]
    if not backward:
        ob, og, nw = extra
        in_specs += [row(D_MODEL), row(D_MODEL), _const_spec(nw.shape)]
        args += [ob, og, nw]
    return pl.pallas_call(
        functools.partial(_hgrn_body, backward),
        grid=(B, nS),
        in_specs=in_specs,
        out_specs=row(D_MODEL),
        out_shape=jax.ShapeDtypeStruct((T, D_MODEL), BF16),
        scratch_shapes=[pltpu.VMEM((HG_HEADS, HG_V, HG_K), F32)],
        compiler_params=_cparams("parallel", "arbitrary"),
        name="hgrn_bwd" if backward else "hgrn_fwd",
    )(*args)


def _t5_bucket(rel):
    half = T5_BUCKETS // 2
    max_exact = half // 2
    n = np.abs(rel)
    large = max_exact + (np.log(np.maximum(n, 1) / max_exact)
                         / np.log(T5_MAX_DIST / max_exact) * (half - max_exact)).astype(np.int32)
    large = np.minimum(large, half - 1)
    return ((rel > 0).astype(np.int32) * half + np.where(n < max_exact, n, large)).astype(np.int32)


def _attention_bias(t5_bias):
    qi = np.arange(A_BLOCK)[None, :]
    kj = np.arange(3 * A_BLOCK)[:, None] - A_BLOCK
    rel = kj - qi
    onehot = (_t5_bucket(rel)[None] == np.arange(T5_BUCKETS)[:, None, None]).astype(np.float32)
    bias = jnp.einsum("bh,bkq->hkq", t5_bias.astype(F32), onehot, precision=lax.Precision.HIGHEST)
    return jnp.where(np.abs(rel)[None] <= A_BLOCK, bias, NEG)


def _trunk(x, B, S, p):
    for l in range(DEPTH):
        j = l // 2
        g = p["gains"][l]
        if l % 2 == 0:
            e = p["even"][j]
            qkv, z, xbc, dt = in_proj_even(x, g[0:1], e["wqkv"], e["wz"], e["wxbc"], e["wdt"])
            a_out = windowed_attention(qkv, p["attn_bias"], e["sink"], B, S)
            common = (dt, e["a_log"], e["a_logT"], e["dt_bias"], B, S)
            yb, xact = ssd_pass(True, xbc, *common, extra=(e["conv_w"], e["conv_b"]))
            b_out = ssd_pass(False, xact, *common, extra=(yb, z, e["d_skip"], e["norm_w"]))
            mixes, wouts = [a_out, b_out], [e["wout_a"], e["wout_b"]]
        else:
            o = p["odd"][j]
            q, kf, kb, lfh, lfl, lbh, lbl, v, og = in_proj_odd(x, g[0:1], o["lb"], o["w_in"])
            ob = hgrn_pass(True, q, kb, lbh, lbl, v, B, S)
            mix = hgrn_pass(False, q, kf, lfh, lfl, v, B, S, extra=(ob, og, o["norm_w"]))
            mixes, wouts = [mix], [o["w_out"]]
        f = p["ffn"][l]
        x = out_proj_ffn(mixes, wouts, x, g[1:4], f["wg"], f["wu"], f["wd"])
    return x


def kernel(x_prompt, x_sample, norm_gains, t5_bias, ev_w_in, attn_sink, ssd_conv_w, ssd_conv_b, ssd_a_log,
           ssd_dt_bias, ssd_d, ssd_norm_w, ev_w_out, od_w_in, hg_lower_bounds, hg_norm_w, od_w_out,
           ffn_w_gate, ffn_w_up, ffn_w_down):
    nh2 = 2 * SSD_HEADS
    c_qkv = A_Q_DIM + 2 * A_KV_DIM
    c_z = c_qkv + SSD_INNER
    c_xbc = c_z + SSD_XBC
    lb_soft = jax.nn.softmax(hg_lower_bounds.astype(F32), axis=0)
    lb_all = jnp.cumsum(lb_soft, axis=0) - lb_soft[0]
    p = {"gains": norm_gains.astype(F32), "attn_bias": _attention_bias(t5_bias), "even": [], "odd": [], "ffn": []}
    for j in range(ev_w_in.shape[0]):
        w = ev_w_in[j]
        p["even"].append(dict(
            wqkv=w[:, :c_qkv].astype(BF16), wz=w[:, c_qkv:c_z].astype(BF16), wxbc=w[:, c_z:c_xbc].astype(BF16),
            wdt=w[:, c_xbc:].astype(BF16),
            sink=attn_sink[j].astype(F32).reshape(1, A_HEADS),
            conv_w=ssd_conv_w[j].astype(F32), conv_b=ssd_conv_b[j].astype(F32).reshape(1, SSD_XBC),
            a_log=ssd_a_log[j].astype(F32).reshape(1, nh2), a_logT=ssd_a_log[j].astype(F32).reshape(nh2, 1),
            dt_bias=ssd_dt_bias[j].astype(F32).reshape(1, nh2),
            d_skip=jnp.repeat(ssd_d[j].astype(F32), SSD_HEAD_DIM).reshape(1, SSD_INNER),
            norm_w=ssd_norm_w[j].astype(F32).reshape(1, SSD_INNER),
            wout_a=ev_w_out[j, :A_Q_DIM].astype(BF16), wout_b=ev_w_out[j, A_Q_DIM:].astype(BF16)))
    for j in range(od_w_in.shape[0]):
        p["odd"].append(dict(
            w_in=od_w_in[j].astype(BF16), lb=lb_all[j].reshape(1, HG_HEADS * HG_K),
            norm_w=hg_norm_w[j].astype(F32).reshape(1, HG_V), w_out=od_w_out[j].astype(BF16)))
    for l in range(DEPTH):
        p["ffn"].append(dict(wg=ffn_w_gate[l].astype(BF16), wu=ffn_w_up[l].astype(BF16),
                             wd=ffn_w_down[l].astype(BF16)))
    outs = []
    for x in (x_prompt, x_sample):
        B, S, _ = x.shape
        outs.append(_trunk(x.reshape(B * S, D_MODEL), B, S, p).reshape(B, S, D_MODEL))
    return tuple(outs)
```

```python
import functools
import math

import numpy as np
import jax
import jax.numpy as jnp
from jax import lax
from jax.experimental import pallas as pl
from jax.experimental.pallas import tpu as pltpu

F32 = jnp.float32
BF16 = jnp.bfloat16

D_MODEL = 1024
DEPTH = 4
EPS = 1e-6
NEG = -1e30

A_HEADS = 8
A_KV_HEADS = 2
A_HEAD_DIM = 64
A_BLOCK = 128
A_Q_DIM = A_HEADS * A_HEAD_DIM
A_KV_DIM = A_KV_HEADS * A_HEAD_DIM
T5_BUCKETS = 32
T5_MAX_DIST = 128

SSD_HEADS = 16
SSD_HEAD_DIM = 64
SSD_INNER = SSD_HEADS * SSD_HEAD_DIM
SSD_GROUPS = 2
SSD_STATE = 64
SSD_CONV = 5
SSD_XBC = SSD_INNER + 2 * SSD_GROUPS * SSD_STATE
SSD_L = 128
SSD_CONV_ROWS = 128
HALO = 8

HG_HEADS = 8
HG_K = 128
HG_V = 128
HG_C = 128
HG_SUB = 32
HG_NB = HG_C // HG_SUB
HG_SEQS = 4
HG_TS = 256

D_FF = 2816
MXU_N = 256
FF_CHUNKS = ((0, 6 * MXU_N), (6 * MXU_N, D_FF))

TM = 512
TS = 512
VMEM_LIMIT_BYTES = 56 * 1024 * 1024


def _cparams(*sem):
    return pltpu.CompilerParams(dimension_semantics=sem, vmem_limit_bytes=VMEM_LIMIT_BYTES)


def _const_spec(shape):
    nd = len(shape)
    return pl.BlockSpec(shape, lambda *_: (0,) * nd, pipeline_mode=pl.Buffered(1))


def _rms(x, g):
    return x * lax.rsqrt(jnp.mean(x * x, axis=-1, keepdims=True) + EPS) * g


def _silu(x):
    return x * (1.0 / (1.0 + jnp.exp(-x)))


def _dot(a, b):
    return jnp.dot(a, b, preferred_element_type=F32)


def _dot_nt(a, b):
    return lax.dot_general(a, b, (((1,), (1,)), ((), ())), preferred_element_type=F32)


def _dot_tn(a, b):
    return lax.dot_general(a, b, (((0,), (0,)), ((), ())), preferred_element_type=F32)


def _split_hi_lo(x):
    hi = x.astype(BF16)
    lo = (x - hi.astype(F32)).astype(BF16)
    return hi, lo


def _in_even_body(x_ref, g_ref, wqkv_ref, wz_ref, wxbc_ref, wdt_ref, qkv_ref, z_ref, xbc_ref, dt_ref):
    xn = _rms(x_ref[...], g_ref[...]).astype(BF16)
    qkv_ref[...] = _dot(xn, wqkv_ref[...]).astype(BF16)
    z_ref[...] = _dot(xn, wz_ref[...])
    xbc_ref[...] = _dot(xn, wxbc_ref[...])
    dt_ref[...] = _dot(xn, wdt_ref[...])


def in_proj_even(x, g, wqkv, wz, wxbc, wdt):
    T = x.shape[0]
    nh2 = 2 * SSD_HEADS
    row = lambda c: pl.BlockSpec((TM, c), lambda i: (i, 0))
    return pl.pallas_call(
        _in_even_body,
        grid=(T // TM,),
        in_specs=[row(D_MODEL), _const_spec((1, D_MODEL)), _const_spec(wqkv.shape), _const_spec(wz.shape),
                  _const_spec(wxbc.shape), _const_spec(wdt.shape)],
        out_specs=[row(A_Q_DIM + 2 * A_KV_DIM), row(SSD_INNER), row(SSD_XBC), row(nh2)],
        out_shape=[jax.ShapeDtypeStruct((T, A_Q_DIM + 2 * A_KV_DIM), BF16),
                   jax.ShapeDtypeStruct((T, SSD_INNER), F32),
                   jax.ShapeDtypeStruct((T, SSD_XBC), F32),
                   jax.ShapeDtypeStruct((T, nh2), F32)],
        compiler_params=_cparams("parallel"),
        name="in_proj_even",
    )(x, g, wqkv, wz, wxbc, wdt)


def _in_odd_body(x_ref, g_ref, lb_ref, w_ref, q_ref, kf_ref, kb_ref, lfh_ref, lfl_ref, lbh_ref, lbl_ref, v_ref,
                 og_ref):
    xn = _rms(x_ref[...], g_ref[...]).astype(BF16)
    W = D_MODEL
    seg = lambda j: _dot(xn, w_ref[:, j * W:(j + 1) * W])
    q_ref[...] = _silu(seg(0)).astype(BF16)
    lb = lb_ref[...]
    for j, k_ref, hi_ref, lo_ref in ((1, kf_ref, lfh_ref, lfl_ref), (2, kb_ref, lbh_ref, lbl_ref)):
        sg = 1.0 / (1.0 + jnp.exp(-seg(j)))
        k_ref[...] = ((1.0 - lb) * (1.0 - sg)).astype(BF16)
        hi_ref[...], lo_ref[...] = _split_hi_lo(jnp.log(lb + (1.0 - lb) * sg))
    v_ref[...] = seg(3).astype(BF16)
    og_ref[...] = _silu(seg(4)).astype(BF16)


def in_proj_odd(x, g, lb, w):
    T = x.shape[0]
    row = pl.BlockSpec((TM, D_MODEL), lambda i: (i, 0))
    n_out = 9
    return pl.pallas_call(
        _in_odd_body,
        grid=(T // TM,),
        in_specs=[row, _const_spec((1, D_MODEL)), _const_spec(lb.shape), _const_spec(w.shape)],
        out_specs=[row] * n_out,
        out_shape=[jax.ShapeDtypeStruct((T, D_MODEL), BF16)] * n_out,
        compiler_params=_cparams("parallel"),
        name="in_proj_odd",
    )(x, g, lb, w)


def _out_ffn_body(n_mix, *refs):
    mix_refs = refs[:n_mix]
    wout_refs = refs[n_mix:2 * n_mix]
    x_ref, g_ref, wg_ref, wu_ref, wd_ref, o_ref = refs[2 * n_mix:]
    halves = [slice(0, TM // 2), slice(TM // 2, TM)]
    ms = []
    for r in halves:
        m = _dot(mix_refs[0][r, :], wout_refs[0][...])
        for a_ref, w_ref in zip(mix_refs[1:], wout_refs[1:]):
            m = m + _dot(a_ref[r, :], w_ref[...])
        ms.append(m)
    x1s = [x_ref[r, :] + _rms(m, g_ref[0:1, :]) for r, m in zip(halves, ms)]
    hns = [_rms(x1, g_ref[1:2, :]).astype(BF16) for x1 in x1s]
    fs = [None] * len(halves)
    for c0, c1 in FF_CHUNKS:
        gates = [_dot(hn, wg_ref[:, c0:c1]) for hn in hns]
        ups = [_dot(hn, wu_ref[:, c0:c1]) for hn in hns]
        hs = [(_silu(gate) * up).astype(BF16) for gate, up in zip(gates, ups)]
        parts = [_dot(h, wd_ref[c0:c1, :]) for h in hs]
        fs = [part if f is None else f + part for f, part in zip(fs, parts)]
    for r, x1, f in zip(halves, x1s, fs):
        o_ref[r, :] = x1 + _rms(f, g_ref[2:3, :])


def out_proj_ffn(mixes, wouts, x, g3, wg, wu, wd):
    T = x.shape[0]
    n_mix = len(mixes)
    row = lambda c: pl.BlockSpec((TM, c), lambda i: (i, 0))
    in_specs = ([row(a.shape[1]) for a in mixes] + [_const_spec(w.shape) for w in wouts]
                + [row(D_MODEL), _const_spec(g3.shape), _const_spec(wg.shape), _const_spec(wu.shape),
                   _const_spec(wd.shape)])
    return pl.pallas_call(
        functools.partial(_out_ffn_body, n_mix),
        grid=(T // TM,),
        in_specs=in_specs,
        out_specs=row(D_MODEL),
        out_shape=jax.ShapeDtypeStruct((T, D_MODEL), F32),
        compiler_params=_cparams("parallel"),
        name="out_proj_ffn",
    )(*mixes, *wouts, x, g3, wg, wu, wd)


def _attn_body(q_ref, kp_ref, kc_ref, kn_ref, vp_ref, vc_ref, vn_ref, bias_ref, sink_ref, o_ref, kbuf, vtbuf):
    i = pl.program_id(1)
    n = pl.num_programs(1)
    nsub = TS // A_BLOCK
    dh = A_HEAD_DIM
    nk = TS + 2 * A_BLOCK
    row = lax.broadcasted_iota(jnp.int32, (2 * dh, nk), 0)
    for g in range(A_KV_HEADS):
        vs = []
        for lo, k_ref, v_ref in ((0, kp_ref, vp_ref), (A_BLOCK, kc_ref, vc_ref), (A_BLOCK + TS, kn_ref, vn_ref)):
            kbuf[g, lo:lo + k_ref.shape[0], :] = k_ref[:, g * dh:(g + 1) * dh]
            vs.append(v_ref[...])
        vt = jnp.concatenate(vs, axis=0).astype(F32).T[g * dh:(g + 1) * dh]
        vt = jnp.concatenate([vt, jnp.zeros_like(vt)], axis=0)
        vtbuf[g] = jnp.where(row == dh, 1.0, vt).astype(BF16)
    key = lax.broadcasted_iota(jnp.int32, (3 * A_BLOCK, A_BLOCK), 0)
    rep = A_HEADS // A_KV_HEADS
    heads = range(A_HEADS)

    def scores(sb):
        r0 = sb * A_BLOCK
        return [_dot_nt(kbuf[h // rep, r0:r0 + 3 * A_BLOCK, :], q_ref[r0:r0 + A_BLOCK, h * dh:(h + 1) * dh])
                for h in heads]

    def softmax_numerators(sb, ss):
        edge = None
        if sb == 0:
            edge = jnp.where(key < jnp.where(i == 0, A_BLOCK, 0), NEG, 0.0)
        if sb == nsub - 1:
            e2 = jnp.where(key >= jnp.where(i == n - 1, 2 * A_BLOCK, 3 * A_BLOCK), NEG, 0.0)
            edge = e2 if edge is None else edge + e2
        ps, ms = [], []
        for h in heads:
            s = ss[h] * (dh ** -0.5) + bias_ref[h]
            if edge is not None:
                s = s + edge
            m = jnp.maximum(jnp.max(s, axis=0, keepdims=True), sink_ref[0:1, h:h + 1])
            ps.append(jnp.exp(s - m).astype(BF16))
            ms.append(m)
        return ps, ms

    def weighted_values(sb, ps):
        r0 = sb * A_BLOCK
        return [_dot(vtbuf[h // rep, :, r0:r0 + 3 * A_BLOCK], ps[h]) for h in heads]

    def finish(sb, pv, ms):
        r0 = sb * A_BLOCK
        outs = []
        for h in heads:
            denom = pv[h][dh:dh + 1, :] + jnp.exp(sink_ref[0:1, h:h + 1] - ms[h])
            outs.append(pv[h][:dh, :] / denom)
        o_ref[r0:r0 + A_BLOCK, :] = jnp.concatenate(outs, axis=0).T.astype(o_ref.dtype)

    ss, pm, pv = {0: scores(0)}, {}, {}
    for sb in range(nsub + 1):
        if sb + 1 < nsub:
            ss[sb + 1] = scores(sb + 1)
        if sb < nsub:
            pm[sb] = softmax_numerators(sb, ss.pop(sb))
            pv[sb] = weighted_values(sb, pm[sb][0])
        if sb >= 1:
            finish(sb - 1, pv.pop(sb - 1), pm.pop(sb - 1)[1])


def windowed_attention(qkv, bias, sink, B, S):
    T = B * S
    nS = S // TS
    sub = TS // A_BLOCK
    nblk = T // A_BLOCK
    kcol = A_Q_DIM // A_KV_DIM
    vcol = kcol + 1
    cur = lambda c: pl.BlockSpec((TS, A_KV_DIM), lambda b, i: (b * nS + i, c))
    prev = lambda c: pl.BlockSpec((A_BLOCK, A_KV_DIM),
                                  lambda b, i: (jnp.maximum((b * nS + i) * sub - 1, 0), c))
    nxt = lambda c: pl.BlockSpec((A_BLOCK, A_KV_DIM),
                                 lambda b, i: (jnp.minimum((b * nS + i + 1) * sub, nblk - 1), c))
    return pl.pallas_call(
        _attn_body,
        grid=(B, nS),
        in_specs=[pl.BlockSpec((TS, A_Q_DIM), lambda b, i: (b * nS + i, 0)),
                  prev(kcol), cur(kcol), nxt(kcol), prev(vcol), cur(vcol), nxt(vcol),
                  _const_spec(bias.shape), _const_spec(sink.shape)],
        out_specs=pl.BlockSpec((TS, A_Q_DIM), lambda b, i: (b * nS + i, 0)),
        out_shape=jax.ShapeDtypeStruct((T, A_Q_DIM), BF16),
        scratch_shapes=[pltpu.VMEM((A_KV_HEADS, TS + 2 * A_BLOCK, A_HEAD_DIM), BF16),
                        pltpu.VMEM((A_KV_HEADS, 2 * A_HEAD_DIM, TS + 2 * A_BLOCK), BF16)],
        compiler_params=_cparams("parallel", "parallel"),
        name="windowed_attention",
    )(qkv, qkv, qkv, qkv, qkv, qkv, qkv, bias, sink)


def _split3_tiled(x):
    h = x.shape[1]
    t = jnp.concatenate([x, x, x], axis=1)
    hi = t.astype(BF16).astype(F32)
    r1 = t - hi
    mid = r1.astype(BF16).astype(F32)
    lane = lax.broadcasted_iota(jnp.int32, t.shape, 1)
    return jnp.where(lane < h, hi, jnp.where(lane < 2 * h, mid, r1 - mid)).astype(BF16)


SSD_HEAD_ORDER = list(range(0, SSD_HEADS, 2)) + list(range(1, SSD_HEADS, 2))


def _ssd_constants():
    H, P, L = SSD_HEADS, SSD_HEAD_DIM, SSD_L
    selb = np.zeros((3 * H, H * L), np.float32)
    selx = np.zeros((3 * H, H * P), np.float32)
    for c, head in enumerate(SSD_HEAD_ORDER):
        for part in range(3):
            selb[part * H + c, head * L:(head + 1) * L] = 1.0
            selx[part * H + c, head * P:(head + 1) * P] = 1.0
    return selb, selx


def _ssd_body(backward, *refs):
    if backward:
        (xc_ref, xp_ref, xn_ref, dt_ref, cw_ref, cb_ref, alog_ref, dtb_ref,
         selb_ref, selx_ref, y_ref, xa_ref, xpad, hst, csb_s, ex_s, sx_s, rowp_s) = refs
    else:
        (xa_ref, dt_ref, alog_ref, dtb_ref, selb_ref, selx_ref,
         yb_ref, z_ref, dsk_ref, nw_ref, y_ref, hst, csb_s, ex_s, sx_s, rowp_s) = refs
    i = pl.program_id(1)
    n = pl.num_programs(1)
    L = SSD_L
    H = SSD_HEADS
    P = SSD_HEAD_DIM
    nchunk = TS // L
    d0 = H if backward else 0
    gw = SSD_INNER // SSD_GROUPS

    @pl.when(i == 0)
    def _():
        hst[...] = jnp.zeros_like(hst)

    if backward:
        t = n - 1 - i
        rows = TS + 2 * HALO
        xpad[0:HALO, :] = jnp.where(t == 0, 0.0, xp_ref[...])
        xpad[HALO:HALO + TS, :] = xc_ref[...]
        xpad[HALO + TS:rows, :] = jnp.where(t == n - 1, 0.0, xn_ref[...])
        CR = SSD_CONV_ROWS
        ng = CR // HALO
        sub = lax.broadcasted_iota(jnp.int32, (1, HALO, 1), 1)

        def conv_rows(rb, carry):
            r0 = pl.multiple_of(rb * CR, CR)
            xg = xpad[pl.ds(r0, CR + 2 * HALO), :].reshape(ng + 2, HALO, SSD_XBC)
            acc = cb_ref[...] + cw_ref[SSD_CONV // 2:SSD_CONV // 2 + 1, :] * xg[1:ng + 1]
            for j in range(SSD_CONV):
                d = j - SSD_CONV // 2
                if d > 0:
                    u = pltpu.roll(xg, HALO - d, 1)
                    acc = acc + cw_ref[j:j + 1, :] * jnp.where(sub < HALO - d, u[1:ng + 1], u[2:ng + 2])
                elif d < 0:
                    u = pltpu.roll(xg, -d, 1)
                    acc = acc + cw_ref[j:j + 1, :] * jnp.where(sub >= -d, u[1:ng + 1], u[0:ng])
            xa_ref[pl.ds(r0, CR), :] = _silu(acc).reshape(CR, SSD_XBC).astype(xa_ref.dtype)
            return carry

        lax.fori_loop(0, TS // CR, conv_rows, 0)

    a_row = -jnp.exp(alog_ref[...])[:, d0:d0 + H]
    r_i = lax.broadcasted_iota(jnp.int32, (L, L), 0)
    c_i = lax.broadcasted_iota(jnp.int32, (L, L), 1)
    keep = (c_i >= r_i) if backward else (c_i <= r_i)
    tri = jnp.where(keep, 1.0, 0.0).astype(BF16)
    r_2 = lax.broadcasted_iota(jnp.int32, (L, 2 * L), 0)
    c_2 = lax.broadcasted_iota(jnp.int32, (L, 2 * L), 1)
    c_2 = jnp.where(c_2 >= L, c_2 - L, c_2)
    keep2 = (c_2 >= r_2) if backward else (c_2 <= r_2)
    last = 0 if backward else L - 1
    lane = lax.broadcasted_iota(jnp.int32, (1, 2 * P), 1)
    hp = H // 2

    css, ecss, scales = [], [], []
    for c in range(nchunk):
        dt = jax.nn.softplus(dt_ref[c * L:(c + 1) * L, :] + dtb_ref[...])[:, d0:d0 + H]
        w_hi, w_lo = _split_hi_lo(dt * a_row)
        cs = _dot(tri, w_hi) + _dot(tri, w_lo)
        css.append(cs)
        ecss.append(jnp.exp(cs))
        scales.append(dt * jnp.exp(cs[last:last + 1, :] - cs))
        tt = jnp.concatenate([cs, dt, jnp.zeros((L, 2 * L - 2 * H), F32)], axis=1)
        tt = jnp.concatenate([tt, jnp.zeros_like(tt)], axis=0).T
        for q, src in enumerate((tt[0:H], tt[H:2 * H])):
            rowp_s[q * hp:(q + 1) * hp, c * 2 * L:(c + 1) * 2 * L] = jnp.concatenate(
                [src[0:hp, 0:L], src[hp:H, 0:L]], axis=1)
    csb_s[...] = _dot(_split3_tiled(jnp.concatenate(css, axis=0)), selb_ref[...])
    ex_s[...] = _dot(_split3_tiled(jnp.concatenate(ecss, axis=0)), selx_ref[...])
    sx_s[...] = _dot(_split3_tiled(jnp.concatenate(scales, axis=0)), selx_ref[...])

    def chunk(ci, carry):
        c = (nchunk - 1 - ci) if backward else ci
        r0 = pl.multiple_of(c * L, L)
        p0 = pl.multiple_of(c * 2 * L, 2 * L)
        csb = csb_s[pl.ds(r0, L), :]
        ex = ex_s[pl.ds(r0, L), :]
        sx = sx_s[pl.ds(r0, L), :]
        csp = rowp_s[0:hp, pl.ds(p0, 2 * L)]
        dtp = rowp_s[hp:H, pl.ds(p0, 2 * L)]
        etx = ex[last:last + 1, :]
        xs = xa_ref[pl.ds(r0, L), 0:SSD_INNER]
        bm = xa_ref[pl.ds(r0, L), SSD_INNER:SSD_INNER + SSD_GROUPS * SSD_STATE]
        cm = xa_ref[pl.ds(r0, L), SSD_INNER + SSD_GROUPS * SSD_STATE:SSD_XBC]
        xw = (xs.astype(F32) * sx).astype(BF16)
        cbs, yoffs = [], []
        for g in range(SSD_GROUPS):
            bg = bm[:, g * SSD_STATE:(g + 1) * SSD_STATE]
            cg = cm[:, g * SSD_STATE:(g + 1) * SSD_STATE]
            cb = _dot_nt(cg, bg)
            cbs.append(jnp.concatenate([cb, cb], axis=1))
            hg = hst[g]
            yoffs.append(_dot(cg, hg.astype(BF16)) * ex[:, g * gw:(g + 1) * gw])
            hst[g] = etx[:, g * gw:(g + 1) * gw] * hg + _dot_tn(bg, xw[:, g * gw:(g + 1) * gw])
        ys = []
        for j in range(hp):
            seg = csb[:, j * 2 * L:(j + 1) * 2 * L] - csp[j:j + 1, :]
            dm = jnp.exp(jnp.where(keep2, seg, NEG)) * dtp[j:j + 1, :]
            m = (cbs[2 * j // (H // SSD_GROUPS)] * dm).astype(BF16)
            xp2 = xs[:, 2 * j * P:(2 * j + 2) * P]
            zero = jnp.zeros_like(xp2)
            rhs = jnp.concatenate([jnp.where(lane < P, xp2, zero), jnp.where(lane >= P, xp2, zero)], axis=0)
            ys.append(_dot(m, rhs))
        y = jnp.concatenate(ys, axis=1) + jnp.concatenate(yoffs, axis=1)
        if backward:
            y_ref[pl.ds(r0, L), :] = y
        else:
            y = y + yb_ref[pl.ds(r0, L), :] + dsk_ref[...] * xs.astype(F32)
            y = y * _silu(z_ref[pl.ds(r0, L), :])
            outs = []
            for g in range(SSD_GROUPS):
                yg = y[:, g * gw:(g + 1) * gw]
                outs.append(yg * lax.rsqrt(jnp.mean(yg * yg, axis=-1, keepdims=True) + EPS))
            y_ref[pl.ds(r0, L), :] = (jnp.concatenate(outs, axis=1) * nw_ref[...]).astype(y_ref.dtype)
        return carry

    lax.fori_loop(0, nchunk, chunk, 0)


def ssd_pass(backward, src, dt, alog, dtb, B, S, extra):
    T = B * S
    nS = S // TS
    per = TS // HALO
    nh2 = 2 * SSD_HEADS
    selb, selx = (jnp.asarray(m, BF16) for m in _ssd_constants())

    def blk(b, i):
        return b * nS + ((nS - 1 - i) if backward else i)

    row = lambda c: pl.BlockSpec((TS, c), lambda b, i: (blk(b, i), 0))
    dt_specs = [row(nh2)]
    par_specs = [_const_spec(alog.shape), _const_spec(dtb.shape),
                 _const_spec(selb.shape), _const_spec(selx.shape)]
    state = [pltpu.VMEM((SSD_GROUPS, SSD_STATE, SSD_INNER // SSD_GROUPS), F32),
             pltpu.VMEM((TS, SSD_HEADS * SSD_L), F32), pltpu.VMEM((TS, SSD_INNER), F32),
             pltpu.VMEM((TS, SSD_INNER), F32), pltpu.VMEM((SSD_HEADS, 2 * TS), F32)]
    if backward:
        cw, cb = extra
        in_specs = ([row(SSD_XBC),
                     pl.BlockSpec((HALO, SSD_XBC), lambda b, i: (jnp.maximum(blk(b, i) * per - 1, 0), 0)),
                     pl.BlockSpec((HALO, SSD_XBC),
                                  lambda b, i: (jnp.minimum((blk(b, i) + 1) * per, T // HALO - 1), 0))]
                    + dt_specs + [_const_spec(cw.shape), _const_spec(cb.shape)] + par_specs)
        args = [src, src, src, dt, cw, cb, alog, dtb, selb, selx]
        out_specs = [row(SSD_INNER), row(SSD_XBC)]
        out_shape = [jax.ShapeDtypeStruct((T, SSD_INNER), F32), jax.ShapeDtypeStruct((T, SSD_XBC), BF16)]
        scratch = [pltpu.VMEM((TS + 2 * HALO, SSD_XBC), F32)] + state
    else:
        yb, z, dsk, nw = extra
        in_specs = ([row(SSD_XBC)] + dt_specs + par_specs
                    + [row(SSD_INNER), row(SSD_INNER), _const_spec(dsk.shape), _const_spec(nw.shape)])
        args = [src, dt, alog, dtb, selb, selx, yb, z, dsk, nw]
        out_specs = row(SSD_INNER)
        out_shape = jax.ShapeDtypeStruct((T, SSD_INNER), BF16)
        scratch = state
    return pl.pallas_call(
        functools.partial(_ssd_body, backward),
        grid=(B, nS),
        in_specs=in_specs,
        out_specs=out_specs,
        out_shape=out_shape,
        scratch_shapes=scratch,
        compiler_params=_cparams("parallel", "arbitrary"),
        name="ssd_bwd" if backward else "ssd_fwd",
    )(*args)


def _hgrn_constants(backward):
    C, NB = HG_C, HG_NB
    t = np.arange(C)[:, None]
    u = np.arange(C)[None, :]
    if backward:
        t, u = C - 1 - t, C - 1 - u
    bt, bu = t // HG_SUB, u // HG_SUB
    cum = (bt == bu) & (u <= t)
    masks = [cum] + [bt - bu == d for d in range(1, NB)]
    return cum.astype(np.float32), np.stack(masks).astype(np.float32)


def _rows_to_tile(rows):
    return jnp.concatenate([jnp.broadcast_to(r, (HG_SUB, r.shape[1])) for r in rows], axis=0)


def _hgrn_body(backward, *refs):
    if backward:
        (q_ref, k_ref, lh_ref, ll_ref, v_ref, cum_ref, msk_ref, o_ref, st) = refs
    else:
        (q_ref, k_ref, lh_ref, ll_ref, v_ref, cum_ref, msk_ref, ob_ref, g_ref, nw_ref, o_ref, st) = refs
    i = pl.program_id(1)
    C, NB = HG_C, HG_NB
    nseq = q_ref.shape[0]
    nchunk = q_ref.shape[1] // C
    order = list(range(NB - 1, -1, -1)) if backward else list(range(NB))
    end_row = 0 if backward else HG_SUB - 1

    @pl.when(i == 0)
    def _():
        st[...] = jnp.zeros_like(st)

    def chunk(ci, carry):
        c = (nchunk - 1 - ci) if backward else ci
        r0 = pl.multiple_of(c * C, C)
        cum = cum_ref[...]
        hsl = [slice(h * HG_K, (h + 1) * HG_K) for h in range(HG_HEADS)]
        lane_blk = lax.broadcasted_iota(jnp.int32, (HG_SUB, C), 1) // HG_SUB

        def load(heads, t):
            t["k"] = {h: k_ref[h[0], pl.ds(r0, C), hsl[h[1]]].astype(F32) for h in heads}
            t["q"] = {h: q_ref[h[0], pl.ds(r0, C), hsl[h[1]]].astype(F32) for h in heads}
            t["v"] = {h: v_ref[h[0], pl.ds(r0, C), hsl[h[1]]] for h in heads}
            hl = {h: jnp.concatenate([lh_ref[h[0], pl.ds(r0, C), hsl[h[1]]], ll_ref[h[0], pl.ds(r0, C), hsl[h[1]]]],
                                     axis=1) for h in heads}
            t["gl"] = {h: _dot(cum, hl[h]) for h in heads}

        def operands(heads, t):
            for name in ("qt", "qd", "kt", "kh", "qin", "kst", "dec"):
                t[name] = {}
            for h in heads:
                gl = t["gl"][h][:, :HG_K] + t["gl"][h][:, HG_K:]
                tot = [gl[j * HG_SUB + end_row:j * HG_SUB + end_row + 1, :] for j in range(NB)]
                zero = jnp.zeros_like(tot[0])
                pre, post, near = [None] * NB, [None] * NB, [[None] * NB for _ in range(NB)]
                for p, b in enumerate(order):
                    pre[b] = sum((tot[order[pp]] for pp in range(p)), zero)
                    post[b] = sum((tot[order[pp]] for pp in range(p + 1, NB)), zero)
                    for d in range(2, NB):
                        near[d][b] = sum((tot[order[pp]] for pp in range(max(p - d + 1, 0), p)), zero)
                qt = (t["q"][h] * jnp.exp(gl)).astype(BF16)
                kt = (t["k"][h] * jnp.exp(-gl)).astype(BF16)
                bexp = lambda rows: _rows_to_tile([jnp.exp(x).astype(BF16) for x in rows])
                kh = kt * bexp(tot)
                t["qin"][h] = qt * bexp(pre)
                t["kst"][h] = kh * bexp(post)
                qd = []
                for d in range(1, NB):
                    blocks = sorted(order[d:])
                    rows = slice(blocks[0] * HG_SUB, (blocks[-1] + 1) * HG_SUB)
                    qd.append(qt[rows] if d == 1 else qt[rows] * bexp([near[d][b] for b in blocks]))
                t["qd"][h] = jnp.concatenate(qd, axis=0)
                t["qt"][h], t["kt"][h], t["kh"][h] = qt, kt, kh
                t["dec"][h] = jnp.exp(sum(tot, zero))

        def scores(heads, t):
            t["att0"] = {h: _dot_nt(t["qt"][h], t["kt"][h]) for h in heads}
            t["offs"] = {h: _dot_nt(t["qd"][h], t["kh"][h]) for h in heads}

        def assemble(heads, t):
            t["att"] = {}
            for h in heads:
                rows_out = []
                for p, b in enumerate(order):
                    r = t["att0"][h][b * HG_SUB:(b + 1) * HG_SUB] * msk_ref[0, b * HG_SUB:(b + 1) * HG_SUB, :]
                    base = 0
                    for d in range(1, NB):
                        blocks = sorted(order[d:])
                        if p >= d:
                            o0 = base + (b - blocks[0]) * HG_SUB
                            r = jnp.where(lane_blk == order[p - d], t["offs"][h][o0:o0 + HG_SUB], r)
                        base += len(blocks) * HG_SUB
                    rows_out.append((b, r))
                t["att"][h] = jnp.concatenate([r for _, r in sorted(rows_out, key=lambda x: x[0])],
                                              axis=0).astype(BF16)
            t["sT"] = {h: st[h[0], h[1]] for h in heads}
            t["out"] = {h: _dot(t["att"][h], t["v"][h]) + _dot_nt(t["qin"][h], t["sT"][h].astype(BF16))
                        for h in heads}
            t["upd"] = {h: _dot_tn(t["v"][h], t["kst"][h]) for h in heads}

        def finish(heads, t):
            for h in heads:
                rows, cols = (h[0], pl.ds(r0, C)), hsl[h[1]]
                st[h[0], h[1]] = t["sT"][h] * t["dec"][h] + t["upd"][h]
                o = t["out"][h]
                if backward:
                    o_ref[rows + (cols,)] = o.astype(o_ref.dtype)
                else:
                    o = o + ob_ref[rows + (cols,)].astype(F32)
                    o = o * lax.rsqrt(jnp.mean(o * o, axis=-1, keepdims=True) + EPS) * nw_ref[...]
                    o_ref[rows + (cols,)] = (o * g_ref[rows + (cols,)].astype(F32)).astype(o_ref.dtype)

        t = {}
        units = [(sq, h) for sq in range(nseq) for h in range(HG_HEADS)]
        for stage in (load, operands, scores, assemble, finish):
            stage(units, t)
        return carry

    lax.fori_loop(0, nchunk, chunk, 0)


def hgrn_pass(backward, q, k, lh, ll, v, B, S, extra=()):
    T = B * S
    nseq = HG_SEQS if B % HG_SEQS == 0 else 1
    ts = HG_TS if nseq > 1 else TS
    nS = S // ts
    cum, masks = _hgrn_constants(backward)
    cum = jnp.asarray(cum, BF16)
    masks = jnp.asarray(masks, F32)

    tile = pl.BlockSpec((nseq, ts, D_MODEL), lambda b, i: (b, (nS - 1 - i) if backward else i, 0))
    seqs = lambda a: a.reshape(B, S, D_MODEL)
    in_specs = [tile] * 5 + [_const_spec(cum.shape), _const_spec(masks.shape)]
    args = [seqs(q), seqs(k), seqs(lh), seqs(ll), seqs(v), cum, masks]
    if not backward:
        ob, og, nw = extra
        in_specs += [tile, tile, _const_spec(nw.shape)]
        args += [seqs(ob), seqs(og), nw]
    return pl.pallas_call(
        functools.partial(_hgrn_body, backward),
        grid=(B // nseq, nS),
        in_specs=in_specs,
        out_specs=tile,
        out_shape=jax.ShapeDtypeStruct((B, S, D_MODEL), BF16),
        scratch_shapes=[pltpu.VMEM((nseq, HG_HEADS, HG_V, HG_K), F32)],
        compiler_params=_cparams("parallel", "arbitrary"),
        name="hgrn_bwd" if backward else "hgrn_fwd",
    )(*args).reshape(T, D_MODEL)


def _t5_bucket(rel):
    half = T5_BUCKETS // 2
    max_exact = half // 2
    n = np.abs(rel)
    large = max_exact + (np.log(np.maximum(n, 1) / max_exact)
                         / np.log(T5_MAX_DIST / max_exact) * (half - max_exact)).astype(np.int32)
    large = np.minimum(large, half - 1)
    return ((rel > 0).astype(np.int32) * half + np.where(n < max_exact, n, large)).astype(np.int32)


def _attention_bias(t5_bias):
    qi = np.arange(A_BLOCK)[None, :]
    kj = np.arange(3 * A_BLOCK)[:, None] - A_BLOCK
    rel = kj - qi
    onehot = (_t5_bucket(rel)[None] == np.arange(T5_BUCKETS)[:, None, None]).astype(np.float32)
    bias = jnp.einsum("bh,bkq->hkq", t5_bias.astype(F32), onehot, precision=lax.Precision.HIGHEST)
    return jnp.where(np.abs(rel)[None] <= A_BLOCK, bias, NEG)


def _trunk(x, B, S, p):
    for l in range(DEPTH):
        j = l // 2
        g = p["gains"][l]
        if l % 2 == 0:
            e = p["even"][j]
            qkv, z, xbc, dt = in_proj_even(x, g[0:1], e["wqkv"], e["wz"], e["wxbc"], e["wdt"])
            a_out = windowed_attention(qkv, p["attn_bias"], e["sink"], B, S)
            common = (dt, e["a_log"], e["dt_bias"], B, S)
            yb, xact = ssd_pass(True, xbc, *common, extra=(e["conv_w"], e["conv_b"]))
            b_out = ssd_pass(False, xact, *common, extra=(yb, z, e["d_skip"], e["norm_w"]))
            mixes, wouts = [a_out, b_out], [e["wout_a"], e["wout_b"]]
        else:
            o = p["odd"][j]
            q, kf, kb, lfh, lfl, lbh, lbl, v, og = in_proj_odd(x, g[0:1], o["lb"], o["w_in"])
            ob = hgrn_pass(True, q, kb, lbh, lbl, v, B, S)
            mix = hgrn_pass(False, q, kf, lfh, lfl, v, B, S, extra=(ob, og, o["norm_w"]))
            mixes, wouts = [mix], [o["w_out"]]
        f = p["ffn"][l]
        x = out_proj_ffn(mixes, wouts, x, g[1:4], f["wg"], f["wu"], f["wd"])
    return x


def kernel(x_prompt, x_sample, norm_gains, t5_bias, ev_w_in, attn_sink, ssd_conv_w, ssd_conv_b, ssd_a_log,
           ssd_dt_bias, ssd_d, ssd_norm_w, ev_w_out, od_w_in, hg_lower_bounds, hg_norm_w, od_w_out,
           ffn_w_gate, ffn_w_up, ffn_w_down):
    nh2 = 2 * SSD_HEADS
    c_qkv = A_Q_DIM + 2 * A_KV_DIM
    c_z = c_qkv + SSD_INNER
    c_xbc = c_z + SSD_XBC
    dt_cols = np.array(SSD_HEAD_ORDER + [SSD_HEADS + h for h in SSD_HEAD_ORDER])
    lb_soft = jax.nn.softmax(hg_lower_bounds.astype(F32), axis=0)
    lb_all = jnp.cumsum(lb_soft, axis=0) - lb_soft[0]
    p = {"gains": norm_gains.astype(F32), "attn_bias": _attention_bias(t5_bias), "even": [], "odd": [], "ffn": []}
    for j in range(ev_w_in.shape[0]):
        w = ev_w_in[j]
        p["even"].append(dict(
            wqkv=w[:, :c_qkv].astype(BF16), wz=w[:, c_qkv:c_z].astype(BF16), wxbc=w[:, c_z:c_xbc].astype(BF16),
            wdt=w[:, c_xbc:][:, dt_cols].astype(BF16),
            sink=attn_sink[j].astype(F32).reshape(1, A_HEADS),
            conv_w=ssd_conv_w[j].astype(F32), conv_b=ssd_conv_b[j].astype(F32).reshape(1, SSD_XBC),
            a_log=ssd_a_log[j].astype(F32).reshape(1, nh2)[:, dt_cols],
            dt_bias=ssd_dt_bias[j].astype(F32).reshape(1, nh2)[:, dt_cols],
            d_skip=jnp.repeat(ssd_d[j].astype(F32), SSD_HEAD_DIM).reshape(1, SSD_INNER),
            norm_w=ssd_norm_w[j].astype(F32).reshape(1, SSD_INNER),
            wout_a=ev_w_out[j, :A_Q_DIM].astype(BF16), wout_b=ev_w_out[j, A_Q_DIM:].astype(BF16)))
    for j in range(od_w_in.shape[0]):
        p["odd"].append(dict(
            w_in=od_w_in[j].astype(BF16), lb=lb_all[j].reshape(1, HG_HEADS * HG_K),
            norm_w=hg_norm_w[j].astype(F32).reshape(1, HG_V), w_out=od_w_out[j].astype(BF16)))
    for l in range(DEPTH):
        p["ffn"].append(dict(wg=ffn_w_gate[l].astype(BF16), wu=ffn_w_up[l].astype(BF16),
                             wd=ffn_w_down[l].astype(BF16)))
    outs = []
    for x in (x_prompt, x_sample):
        B, S, _ = x.shape
        outs.append(_trunk(x.reshape(B * S, D_MODEL), B, S, p).reshape(B, S, D_MODEL))
    return tuple(outs)
```

```python
import functools
import math

import numpy as np
import jax
import jax.numpy as jnp
from jax import lax
from jax.experimental import pallas as pl
from jax.experimental.pallas import tpu as pltpu

F32 = jnp.float32
BF16 = jnp.bfloat16

D_MODEL = 1024
DEPTH = 4
EPS = 1e-6
NEG = -1e30

A_HEADS = 8
A_KV_HEADS = 2
A_HEAD_DIM = 64
A_BLOCK = 128
A_Q_DIM = A_HEADS * A_HEAD_DIM
A_KV_DIM = A_KV_HEADS * A_HEAD_DIM
T5_BUCKETS = 32
T5_MAX_DIST = 128

SSD_HEADS = 16
SSD_HEAD_DIM = 64
SSD_INNER = SSD_HEADS * SSD_HEAD_DIM
SSD_GROUPS = 2
SSD_STATE = 64
SSD_CONV = 5
SSD_XBC = SSD_INNER + 2 * SSD_GROUPS * SSD_STATE
SSD_L = 128
SSD_CONV_ROWS = 128
HALO = 8

HG_HEADS = 8
HG_K = 128
HG_V = 128
HG_C = 128
HG_SUB = 32
HG_NB = HG_C // HG_SUB
HG_SEQS = 4
HG_TS = 256

D_FF = 2816
MXU_N = 256
FF_CHUNKS = ((0, 6 * MXU_N), (6 * MXU_N, D_FF))

TM = 512
TS = 512
VMEM_LIMIT_BYTES = 56 * 1024 * 1024


def _cparams(*sem):
    return pltpu.CompilerParams(dimension_semantics=sem, vmem_limit_bytes=VMEM_LIMIT_BYTES)


def _const_spec(shape):
    nd = len(shape)
    return pl.BlockSpec(shape, lambda *_: (0,) * nd, pipeline_mode=pl.Buffered(1))


def _rms(x, g):
    return x * lax.rsqrt(jnp.mean(x * x, axis=-1, keepdims=True) + EPS) * g


def _silu(x):
    return x * (1.0 / (1.0 + jnp.exp(-x)))


def _dot(a, b):
    return jnp.dot(a, b, preferred_element_type=F32)


def _dot_nt(a, b):
    return lax.dot_general(a, b, (((1,), (1,)), ((), ())), preferred_element_type=F32)


def _dot_tn(a, b):
    return lax.dot_general(a, b, (((0,), (0,)), ((), ())), preferred_element_type=F32)


def _split_hi_lo(x):
    hi = x.astype(BF16)
    lo = (x - hi.astype(F32)).astype(BF16)
    return hi, lo


def _in_even_body(x_ref, g_ref, wqkv_ref, wz_ref, wxbc_ref, wdt_ref, qkv_ref, z_ref, xbc_ref, dt_ref):
    xn = _rms(x_ref[...], g_ref[...]).astype(BF16)
    qkv_ref[...] = _dot(xn, wqkv_ref[...]).astype(BF16)
    z_ref[...] = _dot(xn, wz_ref[...])
    xbc_ref[...] = _dot(xn, wxbc_ref[...])
    dt_ref[...] = _dot(xn, wdt_ref[...])


def in_proj_even(x, g, wqkv, wz, wxbc, wdt):
    T = x.shape[0]
    nh2 = 2 * SSD_HEADS
    row = lambda c: pl.BlockSpec((TM, c), lambda i: (i, 0))
    return pl.pallas_call(
        _in_even_body,
        grid=(T // TM,),
        in_specs=[row(D_MODEL), _const_spec((1, D_MODEL)), _const_spec(wqkv.shape), _const_spec(wz.shape),
                  _const_spec(wxbc.shape), _const_spec(wdt.shape)],
        out_specs=[row(A_Q_DIM + 2 * A_KV_DIM), row(SSD_INNER), row(SSD_XBC), row(nh2)],
        out_shape=[jax.ShapeDtypeStruct((T, A_Q_DIM + 2 * A_KV_DIM), BF16),
                   jax.ShapeDtypeStruct((T, SSD_INNER), F32),
                   jax.ShapeDtypeStruct((T, SSD_XBC), F32),
                   jax.ShapeDtypeStruct((T, nh2), F32)],
        compiler_params=_cparams("parallel"),
        name="in_proj_even",
    )(x, g, wqkv, wz, wxbc, wdt)


def _in_odd_body(x_ref, g_ref, lb_ref, w_ref, q_ref, kf_ref, kb_ref, lf_ref, lbw_ref, v_ref, og_ref):
    xn = _rms(x_ref[...], g_ref[...]).astype(BF16)
    W = D_MODEL
    seg = lambda j: _dot(xn, w_ref[:, j * W:(j + 1) * W])
    q_ref[...] = _silu(seg(0)).astype(BF16)
    lb = lb_ref[...]
    for j, k_ref, l_ref in ((1, kf_ref, lf_ref), (2, kb_ref, lbw_ref)):
        sg = 1.0 / (1.0 + jnp.exp(-seg(j)))
        k_ref[...] = ((1.0 - lb) * (1.0 - sg)).astype(BF16)
        l_ref[...] = jnp.log(lb + (1.0 - lb) * sg)
    v_ref[...] = seg(3).astype(BF16)
    og_ref[...] = _silu(seg(4)).astype(BF16)


def in_proj_odd(x, g, lb, w):
    T = x.shape[0]
    row = pl.BlockSpec((TM, D_MODEL), lambda i: (i, 0))
    dtypes = [BF16, BF16, BF16, F32, F32, BF16, BF16]
    return pl.pallas_call(
        _in_odd_body,
        grid=(T // TM,),
        in_specs=[row, _const_spec((1, D_MODEL)), _const_spec(lb.shape), _const_spec(w.shape)],
        out_specs=[row] * len(dtypes),
        out_shape=[jax.ShapeDtypeStruct((T, D_MODEL), d) for d in dtypes],
        compiler_params=_cparams("parallel"),
        name="in_proj_odd",
    )(x, g, lb, w)


def _out_ffn_body(n_mix, *refs):
    mix_refs = refs[:n_mix]
    wout_refs = refs[n_mix:2 * n_mix]
    x_ref, g_ref, wg_ref, wu_ref, wd_ref, o_ref = refs[2 * n_mix:]
    halves = [slice(0, TM // 2), slice(TM // 2, TM)]
    ms = []
    for r in halves:
        m = _dot(mix_refs[0][r, :], wout_refs[0][...])
        for a_ref, w_ref in zip(mix_refs[1:], wout_refs[1:]):
            m = m + _dot(a_ref[r, :], w_ref[...])
        ms.append(m)
    x1s = [x_ref[r, :] + _rms(m, g_ref[0:1, :]) for r, m in zip(halves, ms)]
    hns = [_rms(x1, g_ref[1:2, :]).astype(BF16) for x1 in x1s]
    fs = [None] * len(halves)
    for c0, c1 in FF_CHUNKS:
        gates = [_dot(hn, wg_ref[:, c0:c1]) for hn in hns]
        ups = [_dot(hn, wu_ref[:, c0:c1]) for hn in hns]
        hs = [(_silu(gate) * up).astype(BF16) for gate, up in zip(gates, ups)]
        parts = [_dot(h, wd_ref[c0:c1, :]) for h in hs]
        fs = [part if f is None else f + part for f, part in zip(fs, parts)]
    for r, x1, f in zip(halves, x1s, fs):
        o_ref[r, :] = x1 + _rms(f, g_ref[2:3, :])


def out_proj_ffn(mixes, wouts, x, g3, wg, wu, wd):
    T = x.shape[0]
    n_mix = len(mixes)
    row = lambda c: pl.BlockSpec((TM, c), lambda i: (i, 0))
    in_specs = ([row(a.shape[1]) for a in mixes] + [_const_spec(w.shape) for w in wouts]
                + [row(D_MODEL), _const_spec(g3.shape), _const_spec(wg.shape), _const_spec(wu.shape),
                   _const_spec(wd.shape)])
    return pl.pallas_call(
        functools.partial(_out_ffn_body, n_mix),
        grid=(T // TM,),
        in_specs=in_specs,
        out_specs=row(D_MODEL),
        out_shape=jax.ShapeDtypeStruct((T, D_MODEL), F32),
        compiler_params=_cparams("parallel"),
        name="out_proj_ffn",
    )(*mixes, *wouts, x, g3, wg, wu, wd)


def _attn_body(q_ref, kp_ref, kc_ref, kn_ref, vp_ref, vc_ref, vn_ref, bias_ref, sink_ref, o_ref, kbuf, vtbuf):
    i = pl.program_id(1)
    n = pl.num_programs(1)
    nsub = TS // A_BLOCK
    dh = A_HEAD_DIM
    nk = TS + 2 * A_BLOCK
    row = lax.broadcasted_iota(jnp.int32, (2 * dh, nk), 0)
    for g in range(A_KV_HEADS):
        vs = []
        for lo, k_ref, v_ref in ((0, kp_ref, vp_ref), (A_BLOCK, kc_ref, vc_ref), (A_BLOCK + TS, kn_ref, vn_ref)):
            kbuf[g, lo:lo + k_ref.shape[0], :] = k_ref[:, g * dh:(g + 1) * dh]
            vs.append(v_ref[...])
        vt = jnp.concatenate(vs, axis=0).astype(F32).T[g * dh:(g + 1) * dh]
        vt = jnp.concatenate([vt, jnp.zeros_like(vt)], axis=0)
        vtbuf[g] = jnp.where(row == dh, 1.0, vt).astype(BF16)
    key = lax.broadcasted_iota(jnp.int32, (3 * A_BLOCK, A_BLOCK), 0)
    rep = A_HEADS // A_KV_HEADS
    heads = range(A_HEADS)

    def scores(sb):
        r0 = sb * A_BLOCK
        return [_dot_nt(kbuf[h // rep, r0:r0 + 3 * A_BLOCK, :], q_ref[r0:r0 + A_BLOCK, h * dh:(h + 1) * dh])
                for h in heads]

    def softmax_numerators(sb, ss):
        edge = None
        if sb == 0:
            edge = jnp.where(key < jnp.where(i == 0, A_BLOCK, 0), NEG, 0.0)
        if sb == nsub - 1:
            e2 = jnp.where(key >= jnp.where(i == n - 1, 2 * A_BLOCK, 3 * A_BLOCK), NEG, 0.0)
            edge = e2 if edge is None else edge + e2
        ps, ms = [], []
        for h in heads:
            s = ss[h] * (dh ** -0.5) + bias_ref[h]
            if edge is not None:
                s = s + edge
            m = jnp.maximum(jnp.max(s, axis=0, keepdims=True), sink_ref[0:1, h:h + 1])
            ps.append(jnp.exp(s - m).astype(BF16))
            ms.append(m)
        return ps, ms

    def weighted_values(sb, ps):
        r0 = sb * A_BLOCK
        return [_dot(vtbuf[h // rep, :, r0:r0 + 3 * A_BLOCK], ps[h]) for h in heads]

    def finish(sb, pv, ms):
        r0 = sb * A_BLOCK
        outs = []
        for h in heads:
            denom = pv[h][dh:dh + 1, :] + jnp.exp(sink_ref[0:1, h:h + 1] - ms[h])
            outs.append(pv[h][:dh, :] / denom)
        o_ref[r0:r0 + A_BLOCK, :] = jnp.concatenate(outs, axis=0).T.astype(o_ref.dtype)

    ss, pm, pv = {0: scores(0)}, {}, {}
    for sb in range(nsub + 1):
        if sb + 1 < nsub:
            ss[sb + 1] = scores(sb + 1)
        if sb < nsub:
            pm[sb] = softmax_numerators(sb, ss.pop(sb))
            pv[sb] = weighted_values(sb, pm[sb][0])
        if sb >= 1:
            finish(sb - 1, pv.pop(sb - 1), pm.pop(sb - 1)[1])


def windowed_attention(qkv, bias, sink, B, S):
    T = B * S
    nS = S // TS
    sub = TS // A_BLOCK
    nblk = T // A_BLOCK
    kcol = A_Q_DIM // A_KV_DIM
    vcol = kcol + 1
    cur = lambda c: pl.BlockSpec((TS, A_KV_DIM), lambda b, i: (b * nS + i, c))
    prev = lambda c: pl.BlockSpec((A_BLOCK, A_KV_DIM),
                                  lambda b, i: (jnp.maximum((b * nS + i) * sub - 1, 0), c))
    nxt = lambda c: pl.BlockSpec((A_BLOCK, A_KV_DIM),
                                 lambda b, i: (jnp.minimum((b * nS + i + 1) * sub, nblk - 1), c))
    return pl.pallas_call(
        _attn_body,
        grid=(B, nS),
        in_specs=[pl.BlockSpec((TS, A_Q_DIM), lambda b, i: (b * nS + i, 0)),
                  prev(kcol), cur(kcol), nxt(kcol), prev(vcol), cur(vcol), nxt(vcol),
                  _const_spec(bias.shape), _const_spec(sink.shape)],
        out_specs=pl.BlockSpec((TS, A_Q_DIM), lambda b, i: (b * nS + i, 0)),
        out_shape=jax.ShapeDtypeStruct((T, A_Q_DIM), BF16),
        scratch_shapes=[pltpu.VMEM((A_KV_HEADS, TS + 2 * A_BLOCK, A_HEAD_DIM), BF16),
                        pltpu.VMEM((A_KV_HEADS, 2 * A_HEAD_DIM, TS + 2 * A_BLOCK), BF16)],
        compiler_params=_cparams("parallel", "parallel"),
        name="windowed_attention",
    )(qkv, qkv, qkv, qkv, qkv, qkv, qkv, bias, sink)


def _split3_tiled(x):
    h = x.shape[1]
    t = jnp.concatenate([x, x, x], axis=1)
    hi = t.astype(BF16).astype(F32)
    r1 = t - hi
    mid = r1.astype(BF16).astype(F32)
    lane = lax.broadcasted_iota(jnp.int32, t.shape, 1)
    return jnp.where(lane < h, hi, jnp.where(lane < 2 * h, mid, r1 - mid)).astype(BF16)


SSD_HEAD_ORDER = list(range(0, SSD_HEADS, 2)) + list(range(1, SSD_HEADS, 2))


def _ssd_constants():
    H, P, L = SSD_HEADS, SSD_HEAD_DIM, SSD_L
    selb = np.zeros((3 * H, H * L), np.float32)
    selx = np.zeros((3 * H, H * P), np.float32)
    for c, head in enumerate(SSD_HEAD_ORDER):
        for part in range(3):
            selb[part * H + c, head * L:(head + 1) * L] = 1.0
            selx[part * H + c, head * P:(head + 1) * P] = 1.0
    return selb, selx


def _ssd_body(backward, *refs):
    if backward:
        (xc_ref, xp_ref, xn_ref, dt_ref, cw_ref, cb_ref, alog_ref, dtb_ref,
         selb_ref, selx_ref, y_ref, xa_ref, xpad, hst, csb_s, ex_s, sx_s, rowp_s) = refs
    else:
        (xa_ref, dt_ref, alog_ref, dtb_ref, selb_ref, selx_ref,
         yb_ref, z_ref, dsk_ref, nw_ref, y_ref, hst, csb_s, ex_s, sx_s, rowp_s) = refs
    i = pl.program_id(1)
    n = pl.num_programs(1)
    L = SSD_L
    H = SSD_HEADS
    P = SSD_HEAD_DIM
    nchunk = TS // L
    d0 = H if backward else 0
    gw = SSD_INNER // SSD_GROUPS

    @pl.when(i == 0)
    def _():
        hst[...] = jnp.zeros_like(hst)

    if backward:
        t = n - 1 - i
        rows = TS + 2 * HALO
        xpad[0:HALO, :] = jnp.where(t == 0, 0.0, xp_ref[...])
        xpad[HALO:HALO + TS, :] = xc_ref[...]
        xpad[HALO + TS:rows, :] = jnp.where(t == n - 1, 0.0, xn_ref[...])
        CR = SSD_CONV_ROWS
        ng = CR // HALO
        sub = lax.broadcasted_iota(jnp.int32, (1, HALO, 1), 1)

        def conv_rows(rb, carry):
            r0 = pl.multiple_of(rb * CR, CR)
            xg = xpad[pl.ds(r0, CR + 2 * HALO), :].reshape(ng + 2, HALO, SSD_XBC)
            acc = cb_ref[...] + cw_ref[SSD_CONV // 2:SSD_CONV // 2 + 1, :] * xg[1:ng + 1]
            for j in range(SSD_CONV):
                d = j - SSD_CONV // 2
                if d > 0:
                    u = pltpu.roll(xg, HALO - d, 1)
                    acc = acc + cw_ref[j:j + 1, :] * jnp.where(sub < HALO - d, u[1:ng + 1], u[2:ng + 2])
                elif d < 0:
                    u = pltpu.roll(xg, -d, 1)
                    acc = acc + cw_ref[j:j + 1, :] * jnp.where(sub >= -d, u[1:ng + 1], u[0:ng])
            xa_ref[pl.ds(r0, CR), :] = _silu(acc).reshape(CR, SSD_XBC).astype(xa_ref.dtype)
            return carry

        lax.fori_loop(0, TS // CR, conv_rows, 0)

    a_row = -jnp.exp(alog_ref[...])[:, d0:d0 + H]
    r_i = lax.broadcasted_iota(jnp.int32, (L, L), 0)
    c_i = lax.broadcasted_iota(jnp.int32, (L, L), 1)
    keep = (c_i >= r_i) if backward else (c_i <= r_i)
    tri = jnp.where(keep, 1.0, 0.0).astype(BF16)
    r_2 = lax.broadcasted_iota(jnp.int32, (L, 2 * L), 0)
    c_2 = lax.broadcasted_iota(jnp.int32, (L, 2 * L), 1)
    c_2 = jnp.where(c_2 >= L, c_2 - L, c_2)
    keep2 = (c_2 >= r_2) if backward else (c_2 <= r_2)
    last = 0 if backward else L - 1
    lane = lax.broadcasted_iota(jnp.int32, (1, 2 * P), 1)
    hp = H // 2

    css, ecss, scales = [], [], []
    for c in range(nchunk):
        dt = jax.nn.softplus(dt_ref[c * L:(c + 1) * L, :] + dtb_ref[...])[:, d0:d0 + H]
        w_hi, w_lo = _split_hi_lo(dt * a_row)
        cs = _dot(tri, w_hi) + _dot(tri, w_lo)
        css.append(cs)
        ecss.append(jnp.exp(cs))
        scales.append(dt * jnp.exp(cs[last:last + 1, :] - cs))
        tt = jnp.concatenate([cs, dt, jnp.zeros((L, 2 * L - 2 * H), F32)], axis=1)
        tt = jnp.concatenate([tt, jnp.zeros_like(tt)], axis=0).T
        for q, src in enumerate((tt[0:H], tt[H:2 * H])):
            rowp_s[q * hp:(q + 1) * hp, c * 2 * L:(c + 1) * 2 * L] = jnp.concatenate(
                [src[0:hp, 0:L], src[hp:H, 0:L]], axis=1)
    csb_s[...] = _dot(_split3_tiled(jnp.concatenate(css, axis=0)), selb_ref[...])
    ex_s[...] = _dot(_split3_tiled(jnp.concatenate(ecss, axis=0)), selx_ref[...])
    sx_s[...] = _dot(_split3_tiled(jnp.concatenate(scales, axis=0)), selx_ref[...])

    def chunk(ci, carry):
        c = (nchunk - 1 - ci) if backward else ci
        r0 = pl.multiple_of(c * L, L)
        p0 = pl.multiple_of(c * 2 * L, 2 * L)
        csb = csb_s[pl.ds(r0, L), :]
        ex = ex_s[pl.ds(r0, L), :]
        sx = sx_s[pl.ds(r0, L), :]
        csp = rowp_s[0:hp, pl.ds(p0, 2 * L)]
        dtp = rowp_s[hp:H, pl.ds(p0, 2 * L)]
        etx = ex[last:last + 1, :]
        xs = xa_ref[pl.ds(r0, L), 0:SSD_INNER]
        bm = xa_ref[pl.ds(r0, L), SSD_INNER:SSD_INNER + SSD_GROUPS * SSD_STATE]
        cm = xa_ref[pl.ds(r0, L), SSD_INNER + SSD_GROUPS * SSD_STATE:SSD_XBC]
        xw = (xs.astype(F32) * sx).astype(BF16)
        cbs, yoffs = [], []
        for g in range(SSD_GROUPS):
            bg = bm[:, g * SSD_STATE:(g + 1) * SSD_STATE]
            cg = cm[:, g * SSD_STATE:(g + 1) * SSD_STATE]
            cb = _dot_nt(cg, bg)
            cbs.append(jnp.concatenate([cb, cb], axis=1))
            hg = hst[g]
            yoffs.append(_dot(cg, hg.astype(BF16)) * ex[:, g * gw:(g + 1) * gw])
            hst[g] = etx[:, g * gw:(g + 1) * gw] * hg + _dot_tn(bg, xw[:, g * gw:(g + 1) * gw])
        ys = []
        for j in range(hp):
            seg = csb[:, j * 2 * L:(j + 1) * 2 * L] - csp[j:j + 1, :]
            dm = jnp.exp(jnp.where(keep2, seg, NEG)) * dtp[j:j + 1, :]
            m = (cbs[2 * j // (H // SSD_GROUPS)] * dm).astype(BF16)
            xp2 = xs[:, 2 * j * P:(2 * j + 2) * P]
            zero = jnp.zeros_like(xp2)
            rhs = jnp.concatenate([jnp.where(lane < P, xp2, zero), jnp.where(lane >= P, xp2, zero)], axis=0)
            ys.append(_dot(m, rhs))
        y = jnp.concatenate(ys, axis=1) + jnp.concatenate(yoffs, axis=1)
        if backward:
            y_ref[pl.ds(r0, L), :] = y
        else:
            y = y + yb_ref[pl.ds(r0, L), :] + dsk_ref[...] * xs.astype(F32)
            y = y * _silu(z_ref[pl.ds(r0, L), :])
            outs = []
            for g in range(SSD_GROUPS):
                yg = y[:, g * gw:(g + 1) * gw]
                outs.append(yg * lax.rsqrt(jnp.mean(yg * yg, axis=-1, keepdims=True) + EPS))
            y_ref[pl.ds(r0, L), :] = (jnp.concatenate(outs, axis=1) * nw_ref[...]).astype(y_ref.dtype)
        return carry

    lax.fori_loop(0, nchunk, chunk, 0)


def ssd_pass(backward, src, dt, alog, dtb, B, S, extra):
    T = B * S
    nS = S // TS
    per = TS // HALO
    nh2 = 2 * SSD_HEADS
    selb, selx = (jnp.asarray(m, BF16) for m in _ssd_constants())

    def blk(b, i):
        return b * nS + ((nS - 1 - i) if backward else i)

    row = lambda c: pl.BlockSpec((TS, c), lambda b, i: (blk(b, i), 0))
    dt_specs = [row(nh2)]
    par_specs = [_const_spec(alog.shape), _const_spec(dtb.shape),
                 _const_spec(selb.shape), _const_spec(selx.shape)]
    state = [pltpu.VMEM((SSD_GROUPS, SSD_STATE, SSD_INNER // SSD_GROUPS), F32),
             pltpu.VMEM((TS, SSD_HEADS * SSD_L), F32), pltpu.VMEM((TS, SSD_INNER), F32),
             pltpu.VMEM((TS, SSD_INNER), F32), pltpu.VMEM((SSD_HEADS, 2 * TS), F32)]
    if backward:
        cw, cb = extra
        in_specs = ([row(SSD_XBC),
                     pl.BlockSpec((HALO, SSD_XBC), lambda b, i: (jnp.maximum(blk(b, i) * per - 1, 0), 0)),
                     pl.BlockSpec((HALO, SSD_XBC),
                                  lambda b, i: (jnp.minimum((blk(b, i) + 1) * per, T // HALO - 1), 0))]
                    + dt_specs + [_const_spec(cw.shape), _const_spec(cb.shape)] + par_specs)
        args = [src, src, src, dt, cw, cb, alog, dtb, selb, selx]
        out_specs = [row(SSD_INNER), row(SSD_XBC)]
        out_shape = [jax.ShapeDtypeStruct((T, SSD_INNER), F32), jax.ShapeDtypeStruct((T, SSD_XBC), BF16)]
        scratch = [pltpu.VMEM((TS + 2 * HALO, SSD_XBC), F32)] + state
    else:
        yb, z, dsk, nw = extra
        in_specs = ([row(SSD_XBC)] + dt_specs + par_specs
                    + [row(SSD_INNER), row(SSD_INNER), _const_spec(dsk.shape), _const_spec(nw.shape)])
        args = [src, dt, alog, dtb, selb, selx, yb, z, dsk, nw]
        out_specs = row(SSD_INNER)
        out_shape = jax.ShapeDtypeStruct((T, SSD_INNER), BF16)
        scratch = state
    return pl.pallas_call(
        functools.partial(_ssd_body, backward),
        grid=(B, nS),
        in_specs=in_specs,
        out_specs=out_specs,
        out_shape=out_shape,
        scratch_shapes=scratch,
        compiler_params=_cparams("parallel", "arbitrary"),
        name="ssd_bwd" if backward else "ssd_fwd",
    )(*args)


def _hgrn_constants(backward):
    C = HG_C
    t = np.arange(C)[:, None]
    u = np.arange(C)[None, :]
    if backward:
        t, u = C - 1 - t, C - 1 - u
    return ((t // HG_SUB == u // HG_SUB) & (u <= t)).astype(np.float32)


def _rows_to_tile(rows):
    return jnp.concatenate([jnp.broadcast_to(r, (HG_SUB, r.shape[1])) for r in rows], axis=0)


def _hgrn_body(backward, *refs):
    if backward:
        (q_ref, k_ref, l_ref, v_ref, cum_ref, msk_ref, o_ref, st) = refs
    else:
        (q_ref, k_ref, l_ref, v_ref, cum_ref, msk_ref, ob_ref, g_ref, nw_ref, o_ref, st) = refs
    i = pl.program_id(1)
    C, NB = HG_C, HG_NB
    nseq = q_ref.shape[0]
    nchunk = q_ref.shape[1] // C
    order = list(range(NB - 1, -1, -1)) if backward else list(range(NB))
    end_row = 0 if backward else HG_SUB - 1

    @pl.when(i == 0)
    def _():
        st[...] = jnp.zeros_like(st)

    def chunk(ci, carry):
        c = (nchunk - 1 - ci) if backward else ci
        r0 = pl.multiple_of(c * C, C)
        cum = cum_ref[...]
        hsl = [slice(h * HG_K, (h + 1) * HG_K) for h in range(HG_HEADS)]
        lane_blk = lax.broadcasted_iota(jnp.int32, (HG_SUB, C), 1) // HG_SUB

        def load(heads, t):
            t["k"] = {h: k_ref[h[0], pl.ds(r0, C), hsl[h[1]]].astype(F32) for h in heads}
            t["q"] = {h: q_ref[h[0], pl.ds(r0, C), hsl[h[1]]].astype(F32) for h in heads}
            t["v"] = {h: v_ref[h[0], pl.ds(r0, C), hsl[h[1]]] for h in heads}
            hl = {h: jnp.concatenate(_split_hi_lo(l_ref[h[0], pl.ds(r0, C), hsl[h[1]]]), axis=1)
                  for h in heads}
            t["gl"] = {h: _dot(cum, hl[h]) for h in heads}

        def operands(heads, t):
            for name in ("qt", "qd", "kt", "kh", "qin", "kst", "dec"):
                t[name] = {}
            for h in heads:
                gl = t["gl"][h][:, :HG_K] + t["gl"][h][:, HG_K:]
                tot = [gl[j * HG_SUB + end_row:j * HG_SUB + end_row + 1, :] for j in range(NB)]
                zero = jnp.zeros_like(tot[0])
                pre, post, near = [None] * NB, [None] * NB, [[None] * NB for _ in range(NB)]
                for p, b in enumerate(order):
                    pre[b] = sum((tot[order[pp]] for pp in range(p)), zero)
                    post[b] = sum((tot[order[pp]] for pp in range(p + 1, NB)), zero)
                    for d in range(2, NB):
                        near[d][b] = sum((tot[order[pp]] for pp in range(max(p - d + 1, 0), p)), zero)
                qt = (t["q"][h] * jnp.exp(gl)).astype(BF16)
                kt = (t["k"][h] * jnp.exp(-gl)).astype(BF16)
                bexp = lambda rows: _rows_to_tile([jnp.exp(x).astype(BF16) for x in rows])
                kh = kt * bexp(tot)
                t["qin"][h] = qt * bexp(pre)
                t["kst"][h] = kh * bexp(post)
                qd = []
                for d in range(1, NB):
                    blocks = sorted(order[d:])
                    rows = slice(blocks[0] * HG_SUB, (blocks[-1] + 1) * HG_SUB)
                    qd.append(qt[rows] if d == 1 else qt[rows] * bexp([near[d][b] for b in blocks]))
                t["qd"][h] = jnp.concatenate(qd, axis=0)
                t["qt"][h], t["kt"][h], t["kh"][h] = qt, kt, kh
                t["dec"][h] = jnp.exp(sum(tot, zero))

        def scores(heads, t):
            t["att0"] = {h: _dot_nt(t["qt"][h], t["kt"][h]) for h in heads}
            t["offs"] = {h: _dot_nt(t["qd"][h], t["kh"][h]) for h in heads}

        def assemble(heads, t):
            t["att"] = {}
            for h in heads:
                rows_out = []
                for p, b in enumerate(order):
                    r = t["att0"][h][b * HG_SUB:(b + 1) * HG_SUB] * msk_ref[b * HG_SUB:(b + 1) * HG_SUB, :]
                    base = 0
                    for d in range(1, NB):
                        blocks = sorted(order[d:])
                        if p >= d:
                            o0 = base + (b - blocks[0]) * HG_SUB
                            r = jnp.where(lane_blk == order[p - d], t["offs"][h][o0:o0 + HG_SUB], r)
                        base += len(blocks) * HG_SUB
                    rows_out.append((b, r))
                t["att"][h] = jnp.concatenate([r for _, r in sorted(rows_out, key=lambda x: x[0])],
                                              axis=0).astype(BF16)
            t["sT"] = {h: st[h[0], h[1]] for h in heads}
            t["out"] = {h: _dot(t["att"][h], t["v"][h]) + _dot_nt(t["qin"][h], t["sT"][h].astype(BF16))
                        for h in heads}
            t["upd"] = {h: _dot_tn(t["v"][h], t["kst"][h]) for h in heads}

        def finish(heads, t):
            for h in heads:
                rows, cols = (h[0], pl.ds(r0, C)), hsl[h[1]]
                st[h[0], h[1]] = t["sT"][h] * t["dec"][h] + t["upd"][h]
                o = t["out"][h]
                if backward:
                    o_ref[rows + (cols,)] = o.astype(o_ref.dtype)
                else:
                    o = o + ob_ref[rows + (cols,)].astype(F32)
                    o = o * lax.rsqrt(jnp.mean(o * o, axis=-1, keepdims=True) + EPS) * nw_ref[...]
                    o_ref[rows + (cols,)] = (o * g_ref[rows + (cols,)].astype(F32)).astype(o_ref.dtype)

        t = {}
        units = [(sq, h) for sq in range(nseq) for h in range(HG_HEADS)]
        for stage in (load, operands, scores, assemble, finish):
            stage(units, t)
        return carry

    lax.fori_loop(0, nchunk, chunk, 0)


def hgrn_pass(backward, q, k, logf, v, B, S, extra=()):
    T = B * S
    nseq = HG_SEQS if B % HG_SEQS == 0 else 1
    ts = HG_TS if nseq > 1 else TS
    nS = S // ts
    masks = jnp.asarray(_hgrn_constants(backward), F32)
    cum = masks.astype(BF16)

    tile = pl.BlockSpec((nseq, ts, D_MODEL), lambda b, i: (b, (nS - 1 - i) if backward else i, 0))
    seqs = lambda a: a.reshape(B, S, D_MODEL)
    in_specs = [tile] * 4 + [_const_spec(cum.shape), _const_spec(masks.shape)]
    args = [seqs(q), seqs(k), seqs(logf), seqs(v), cum, masks]
    if not backward:
        ob, og, nw = extra
        in_specs += [tile, tile, _const_spec(nw.shape)]
        args += [seqs(ob), seqs(og), nw]
    return pl.pallas_call(
        functools.partial(_hgrn_body, backward),
        grid=(B // nseq, nS),
        in_specs=in_specs,
        out_specs=tile,
        out_shape=jax.ShapeDtypeStruct((B, S, D_MODEL), BF16),
        scratch_shapes=[pltpu.VMEM((nseq, HG_HEADS, HG_V, HG_K), F32)],
        compiler_params=_cparams("parallel", "arbitrary"),
        name="hgrn_bwd" if backward else "hgrn_fwd",
    )(*args).reshape(T, D_MODEL)


def _t5_bucket(rel):
    half = T5_BUCKETS // 2
    max_exact = half // 2
    n = np.abs(rel)
    large = max_exact + (np.log(np.maximum(n, 1) / max_exact)
                         / np.log(T5_MAX_DIST / max_exact) * (half - max_exact)).astype(np.int32)
    large = np.minimum(large, half - 1)
    return ((rel > 0).astype(np.int32) * half + np.where(n < max_exact, n, large)).astype(np.int32)


def _attention_bias(t5_bias):
    qi = np.arange(A_BLOCK)[None, :]
    kj = np.arange(3 * A_BLOCK)[:, None] - A_BLOCK
    rel = kj - qi
    onehot = (_t5_bucket(rel)[None] == np.arange(T5_BUCKETS)[:, None, None]).astype(np.float32)
    bias = jnp.einsum("bh,bkq->hkq", t5_bias.astype(F32), onehot, precision=lax.Precision.HIGHEST)
    return jnp.where(np.abs(rel)[None] <= A_BLOCK, bias, NEG)


def _trunk(x, B, S, p):
    for l in range(DEPTH):
        j = l // 2
        g = p["gains"][l]
        if l % 2 == 0:
            e = p["even"][j]
            qkv, z, xbc, dt = in_proj_even(x, g[0:1], e["wqkv"], e["wz"], e["wxbc"], e["wdt"])
            a_out = windowed_attention(qkv, p["attn_bias"], e["sink"], B, S)
            common = (dt, e["a_log"], e["dt_bias"], B, S)
            yb, xact = ssd_pass(True, xbc, *common, extra=(e["conv_w"], e["conv_b"]))
            b_out = ssd_pass(False, xact, *common, extra=(yb, z, e["d_skip"], e["norm_w"]))
            mixes, wouts = [a_out, b_out], [e["wout_a"], e["wout_b"]]
        else:
            o = p["odd"][j]
            q, kf, kb, lf, lbw, v, og = in_proj_odd(x, g[0:1], o["lb"], o["w_in"])
            ob = hgrn_pass(True, q, kb, lbw, v, B, S)
            mix = hgrn_pass(False, q, kf, lf, v, B, S, extra=(ob, og, o["norm_w"]))
            mixes, wouts = [mix], [o["w_out"]]
        f = p["ffn"][l]
        x = out_proj_ffn(mixes, wouts, x, g[1:4], f["wg"], f["wu"], f["wd"])
    return x


def kernel(x_prompt, x_sample, norm_gains, t5_bias, ev_w_in, attn_sink, ssd_conv_w, ssd_conv_b, ssd_a_log,
           ssd_dt_bias, ssd_d, ssd_norm_w, ev_w_out, od_w_in, hg_lower_bounds, hg_norm_w, od_w_out,
           ffn_w_gate, ffn_w_up, ffn_w_down):
    nh2 = 2 * SSD_HEADS
    c_qkv = A_Q_DIM + 2 * A_KV_DIM
    c_z = c_qkv + SSD_INNER
    c_xbc = c_z + SSD_XBC
    dt_cols = np.array(SSD_HEAD_ORDER + [SSD_HEADS + h for h in SSD_HEAD_ORDER])
    lb_soft = jax.nn.softmax(hg_lower_bounds.astype(F32), axis=0)
    lb_all = jnp.cumsum(lb_soft, axis=0) - lb_soft[0]
    p = {"gains": norm_gains.astype(F32), "attn_bias": _attention_bias(t5_bias), "even": [], "odd": [], "ffn": []}
    for j in range(ev_w_in.shape[0]):
        w = ev_w_in[j]
        p["even"].append(dict(
            wqkv=w[:, :c_qkv].astype(BF16), wz=w[:, c_qkv:c_z].astype(BF16), wxbc=w[:, c_z:c_xbc].astype(BF16),
            wdt=w[:, c_xbc:][:, dt_cols].astype(BF16),
            sink=attn_sink[j].astype(F32).reshape(1, A_HEADS),
            conv_w=ssd_conv_w[j].astype(F32), conv_b=ssd_conv_b[j].astype(F32).reshape(1, SSD_XBC),
            a_log=ssd_a_log[j].astype(F32).reshape(1, nh2)[:, dt_cols],
            dt_bias=ssd_dt_bias[j].astype(F32).reshape(1, nh2)[:, dt_cols],
            d_skip=jnp.repeat(ssd_d[j].astype(F32), SSD_HEAD_DIM).reshape(1, SSD_INNER),
            norm_w=ssd_norm_w[j].astype(F32).reshape(1, SSD_INNER),
            wout_a=ev_w_out[j, :A_Q_DIM].astype(BF16), wout_b=ev_w_out[j, A_Q_DIM:].astype(BF16)))
    for j in range(od_w_in.shape[0]):
        p["odd"].append(dict(
            w_in=od_w_in[j].astype(BF16), lb=lb_all[j].reshape(1, HG_HEADS * HG_K),
            norm_w=hg_norm_w[j].astype(F32).reshape(1, HG_V), w_out=od_w_out[j].astype(BF16)))
    for l in range(DEPTH):
        p["ffn"].append(dict(wg=ffn_w_gate[l].astype(BF16), wu=ffn_w_up[l].astype(BF16),
                             wd=ffn_w_down[l].astype(BF16)))
    outs = []
    for x in (x_prompt, x_sample):
        B, S, _ = x.shape
        outs.append(_trunk(x.reshape(B * S, D_MODEL), B, S, p).reshape(B, S, D_MODEL))
    return tuple(outs)
```

```python
import functools

import numpy as np
import jax
import jax.numpy as jnp
from jax import lax
from jax.experimental import pallas as pl
from jax.experimental.pallas import tpu as pltpu

F32 = jnp.float32
BF16 = jnp.bfloat16

D_MODEL = 1024
DEPTH = 4
EPS = 1e-6
NEG = -1e30

A_HEADS = 8
A_KV_HEADS = 2
A_HEAD_DIM = 64
A_BLOCK = 128
A_Q_DIM = A_HEADS * A_HEAD_DIM
A_KV_DIM = A_KV_HEADS * A_HEAD_DIM
T5_BUCKETS = 32
T5_MAX_DIST = 128

SSD_HEADS = 16
SSD_HEAD_DIM = 64
SSD_INNER = SSD_HEADS * SSD_HEAD_DIM
SSD_GROUPS = 2
SSD_STATE = 64
SSD_CONV = 5
SSD_XBC = SSD_INNER + 2 * SSD_GROUPS * SSD_STATE
SSD_L = 128
SSD_CONV_ROWS = 128
HALO = 8

HG_HEADS = 8
HG_K = 128
HG_V = 128
HG_C = 128
HG_SUB = 32
HG_NB = HG_C // HG_SUB
HG_SEQS = 4
HG_TS = 256

D_FF = 2816
MXU_N = 256
FF_CHUNKS = ((0, 6 * MXU_N), (6 * MXU_N, D_FF))

TM = 512
TS = 512
VMEM_LIMIT_BYTES = 56 * 1024 * 1024


def _cparams(*sem):
    return pltpu.CompilerParams(dimension_semantics=sem, vmem_limit_bytes=VMEM_LIMIT_BYTES)


def _const_spec(shape):
    nd = len(shape)
    return pl.BlockSpec(shape, lambda *_: (0,) * nd, pipeline_mode=pl.Buffered(1))


def _rms(x, g):
    return x * lax.rsqrt(jnp.mean(x * x, axis=-1, keepdims=True) + EPS) * g


def _silu(x):
    return x * (1.0 / (1.0 + jnp.exp(-x)))


def _dot(a, b):
    return jnp.dot(a, b, preferred_element_type=F32)


def _dot_nt(a, b):
    return lax.dot_general(a, b, (((1,), (1,)), ((), ())), preferred_element_type=F32)


def _dot_tn(a, b):
    return lax.dot_general(a, b, (((0,), (0,)), ((), ())), preferred_element_type=F32)


def _split_hi_lo(x):
    hi = x.astype(BF16)
    lo = (x - hi.astype(F32)).astype(BF16)
    return hi, lo


def _in_even_body(x_ref, g_ref, wqkv_ref, wz_ref, wxbc_ref, wdt_ref, qkv_ref, z_ref, xbc_ref, dt_ref):
    xn = _rms(x_ref[...], g_ref[...]).astype(BF16)
    qkv_ref[...] = _dot(xn, wqkv_ref[...]).astype(BF16)
    z_ref[...] = _dot(xn, wz_ref[...])
    xbc_ref[...] = _dot(xn, wxbc_ref[...])
    dt_ref[...] = _dot(xn, wdt_ref[...])


def in_proj_even(x, g, wqkv, wz, wxbc, wdt):
    T = x.shape[0]
    nh2 = 2 * SSD_HEADS
    row = lambda c: pl.BlockSpec((TM, c), lambda i: (i, 0))
    return pl.pallas_call(
        _in_even_body,
        grid=(T // TM,),
        in_specs=[row(D_MODEL), _const_spec((1, D_MODEL)), _const_spec(wqkv.shape), _const_spec(wz.shape),
                  _const_spec(wxbc.shape), _const_spec(wdt.shape)],
        out_specs=[row(A_Q_DIM + 2 * A_KV_DIM), row(SSD_INNER), row(SSD_XBC), row(nh2)],
        out_shape=[jax.ShapeDtypeStruct((T, A_Q_DIM + 2 * A_KV_DIM), BF16),
                   jax.ShapeDtypeStruct((T, SSD_INNER), F32),
                   jax.ShapeDtypeStruct((T, SSD_XBC), F32),
                   jax.ShapeDtypeStruct((T, nh2), F32)],
        compiler_params=_cparams("parallel"),
        name="in_proj_even",
    )(x, g, wqkv, wz, wxbc, wdt)


def _in_odd_body(x_ref, g_ref, lb_ref, w_ref, q_ref, kf_ref, kb_ref, lf_ref, lbw_ref, v_ref, og_ref):
    xn = _rms(x_ref[...], g_ref[...]).astype(BF16)
    W = D_MODEL
    seg = lambda j: _dot(xn, w_ref[:, j * W:(j + 1) * W])
    q_ref[...] = _silu(seg(0)).astype(BF16)
    lb = lb_ref[...]
    for j, k_ref, l_ref in ((1, kf_ref, lf_ref), (2, kb_ref, lbw_ref)):
        sg = 1.0 / (1.0 + jnp.exp(-seg(j)))
        k_ref[...] = ((1.0 - lb) * (1.0 - sg)).astype(BF16)
        l_ref[...] = jnp.log(lb + (1.0 - lb) * sg)
    v_ref[...] = seg(3).astype(BF16)
    og_ref[...] = _silu(seg(4)).astype(BF16)


def in_proj_odd(x, g, lb, w):
    T = x.shape[0]
    row = pl.BlockSpec((TM, D_MODEL), lambda i: (i, 0))
    dtypes = [BF16, BF16, BF16, F32, F32, BF16, BF16]
    return pl.pallas_call(
        _in_odd_body,
        grid=(T // TM,),
        in_specs=[row, _const_spec((1, D_MODEL)), _const_spec(lb.shape), _const_spec(w.shape)],
        out_specs=[row] * len(dtypes),
        out_shape=[jax.ShapeDtypeStruct((T, D_MODEL), d) for d in dtypes],
        compiler_params=_cparams("parallel"),
        name="in_proj_odd",
    )(x, g, lb, w)


def _out_ffn_body(n_mix, *refs):
    mix_refs = refs[:n_mix]
    wout_refs = refs[n_mix:2 * n_mix]
    x_ref, g_ref, wg_ref, wu_ref, wd_ref, o_ref = refs[2 * n_mix:]
    halves = [slice(0, TM // 2), slice(TM // 2, TM)]
    ms = []
    for r in halves:
        m = _dot(mix_refs[0][r, :], wout_refs[0][...])
        for a_ref, w_ref in zip(mix_refs[1:], wout_refs[1:]):
            m = m + _dot(a_ref[r, :], w_ref[...])
        ms.append(m)
    x1s = [x_ref[r, :] + _rms(m, g_ref[0:1, :]) for r, m in zip(halves, ms)]
    hns = [_rms(x1, g_ref[1:2, :]).astype(BF16) for x1 in x1s]
    fs = [None] * len(halves)
    for c0, c1 in FF_CHUNKS:
        gates = [_dot(hn, wg_ref[:, c0:c1]) for hn in hns]
        ups = [_dot(hn, wu_ref[:, c0:c1]) for hn in hns]
        hs = [(_silu(gate) * up).astype(BF16) for gate, up in zip(gates, ups)]
        parts = [_dot(h, wd_ref[c0:c1, :]) for h in hs]
        fs = [part if f is None else f + part for f, part in zip(fs, parts)]
    for r, x1, f in zip(halves, x1s, fs):
        o_ref[r, :] = x1 + _rms(f, g_ref[2:3, :])


def out_proj_ffn(mixes, wouts, x, g3, wg, wu, wd):
    T = x.shape[0]
    n_mix = len(mixes)
    row = lambda c: pl.BlockSpec((TM, c), lambda i: (i, 0))
    in_specs = ([row(a.shape[1]) for a in mixes] + [_const_spec(w.shape) for w in wouts]
                + [row(D_MODEL), _const_spec(g3.shape), _const_spec(wg.shape), _const_spec(wu.shape),
                   _const_spec(wd.shape)])
    return pl.pallas_call(
        functools.partial(_out_ffn_body, n_mix),
        grid=(T // TM,),
        in_specs=in_specs,
        out_specs=row(D_MODEL),
        out_shape=jax.ShapeDtypeStruct((T, D_MODEL), F32),
        compiler_params=_cparams("parallel"),
        name="out_proj_ffn",
    )(*mixes, *wouts, x, g3, wg, wu, wd)


def _attn_body(q_ref, kp_ref, kc_ref, kn_ref, vp_ref, vc_ref, vn_ref, bias_ref, sink_ref, o_ref, kbuf, vtbuf):
    i = pl.program_id(1)
    n = pl.num_programs(1)
    nsub = TS // A_BLOCK
    dh = A_HEAD_DIM
    nk = TS + 2 * A_BLOCK
    row = lax.broadcasted_iota(jnp.int32, (2 * dh, nk), 0)
    for g in range(A_KV_HEADS):
        vs = []
        for lo, k_ref, v_ref in ((0, kp_ref, vp_ref), (A_BLOCK, kc_ref, vc_ref), (A_BLOCK + TS, kn_ref, vn_ref)):
            kbuf[g, lo:lo + k_ref.shape[0], :] = k_ref[:, g * dh:(g + 1) * dh]
            vs.append(v_ref[...])
        vt = jnp.concatenate(vs, axis=0).astype(F32).T[g * dh:(g + 1) * dh]
        vt = jnp.concatenate([vt, jnp.zeros_like(vt)], axis=0)
        vtbuf[g] = jnp.where(row == dh, 1.0, vt).astype(BF16)
    key = lax.broadcasted_iota(jnp.int32, (3 * A_BLOCK, A_BLOCK), 0)
    rep = A_HEADS // A_KV_HEADS
    heads = range(A_HEADS)

    def scores(sb):
        r0 = sb * A_BLOCK
        return [_dot_nt(kbuf[h // rep, r0:r0 + 3 * A_BLOCK, :], q_ref[r0:r0 + A_BLOCK, h * dh:(h + 1) * dh])
                for h in heads]

    def softmax_numerators(sb, ss):
        edge = None
        if sb == 0:
            edge = jnp.where(key < jnp.where(i == 0, A_BLOCK, 0), NEG, 0.0)
        if sb == nsub - 1:
            e2 = jnp.where(key >= jnp.where(i == n - 1, 2 * A_BLOCK, 3 * A_BLOCK), NEG, 0.0)
            edge = e2 if edge is None else edge + e2
        ps, ms = [], []
        for h in heads:
            s = ss[h] * (dh ** -0.5) + bias_ref[h]
            if edge is not None:
                s = s + edge
            m = jnp.maximum(jnp.max(s, axis=0, keepdims=True), sink_ref[0:1, h:h + 1])
            ps.append(jnp.exp(s - m).astype(BF16))
            ms.append(m)
        return ps, ms

    def weighted_values(sb, ps):
        r0 = sb * A_BLOCK
        return [_dot(vtbuf[h // rep, :, r0:r0 + 3 * A_BLOCK], ps[h]) for h in heads]

    def finish(sb, pv, ms):
        r0 = sb * A_BLOCK
        outs = []
        for h in heads:
            denom = pv[h][dh:dh + 1, :] + jnp.exp(sink_ref[0:1, h:h + 1] - ms[h])
            outs.append(pv[h][:dh, :] / denom)
        o_ref[r0:r0 + A_BLOCK, :] = jnp.concatenate(outs, axis=0).T.astype(o_ref.dtype)

    ss, pm, pv = {0: scores(0)}, {}, {}
    for sb in range(nsub + 1):
        if sb + 1 < nsub:
            ss[sb + 1] = scores(sb + 1)
        if sb < nsub:
            pm[sb] = softmax_numerators(sb, ss.pop(sb))
            pv[sb] = weighted_values(sb, pm[sb][0])
        if sb >= 1:
            finish(sb - 1, pv.pop(sb - 1), pm.pop(sb - 1)[1])


def windowed_attention(qkv, bias, sink, B, S):
    T = B * S
    nS = S // TS
    sub = TS // A_BLOCK
    nblk = T // A_BLOCK
    kcol = A_Q_DIM // A_KV_DIM
    vcol = kcol + 1
    cur = lambda c: pl.BlockSpec((TS, A_KV_DIM), lambda b, i: (b * nS + i, c))
    prev = lambda c: pl.BlockSpec((A_BLOCK, A_KV_DIM),
                                  lambda b, i: (jnp.maximum((b * nS + i) * sub - 1, 0), c))
    nxt = lambda c: pl.BlockSpec((A_BLOCK, A_KV_DIM),
                                 lambda b, i: (jnp.minimum((b * nS + i + 1) * sub, nblk - 1), c))
    return pl.pallas_call(
        _attn_body,
        grid=(B, nS),
        in_specs=[pl.BlockSpec((TS, A_Q_DIM), lambda b, i: (b * nS + i, 0)),
                  prev(kcol), cur(kcol), nxt(kcol), prev(vcol), cur(vcol), nxt(vcol),
                  _const_spec(bias.shape), _const_spec(sink.shape)],
        out_specs=pl.BlockSpec((TS, A_Q_DIM), lambda b, i: (b * nS + i, 0)),
        out_shape=jax.ShapeDtypeStruct((T, A_Q_DIM), BF16),
        scratch_shapes=[pltpu.VMEM((A_KV_HEADS, TS + 2 * A_BLOCK, A_HEAD_DIM), BF16),
                        pltpu.VMEM((A_KV_HEADS, 2 * A_HEAD_DIM, TS + 2 * A_BLOCK), BF16)],
        compiler_params=_cparams("parallel", "parallel"),
        name="windowed_attention",
    )(qkv, qkv, qkv, qkv, qkv, qkv, qkv, bias, sink)


def _split3_tiled(x):
    h = x.shape[1]
    t = jnp.concatenate([x, x, x], axis=1)
    hi = t.astype(BF16).astype(F32)
    r1 = t - hi
    mid = r1.astype(BF16).astype(F32)
    lane = lax.broadcasted_iota(jnp.int32, t.shape, 1)
    return jnp.where(lane < h, hi, jnp.where(lane < 2 * h, mid, r1 - mid)).astype(BF16)


SSD_HEAD_ORDER = list(range(0, SSD_HEADS, 2)) + list(range(1, SSD_HEADS, 2))


def _ssd_constants():
    H, P, L = SSD_HEADS, SSD_HEAD_DIM, SSD_L
    selb = np.zeros((3 * H, H * L), np.float32)
    selx = np.zeros((3 * H, H * P), np.float32)
    for c, head in enumerate(SSD_HEAD_ORDER):
        for part in range(3):
            selb[part * H + c, head * L:(head + 1) * L] = 1.0
            selx[part * H + c, head * P:(head + 1) * P] = 1.0
    return selb, selx


def _ssd_body(backward, *refs):
    if backward:
        (xc_ref, xp_ref, xn_ref, dt_ref, cw_ref, cb_ref, alog_ref, dtb_ref,
         selb_ref, selx_ref, y_ref, xa_ref, xpad, hst, csb_s, ex_s, sx_s, rowp_s) = refs
    else:
        (xa_ref, dt_ref, alog_ref, dtb_ref, selb_ref, selx_ref,
         yb_ref, z_ref, dsk_ref, nw_ref, y_ref, hst, csb_s, ex_s, sx_s, rowp_s) = refs
    i = pl.program_id(1)
    n = pl.num_programs(1)
    L = SSD_L
    H = SSD_HEADS
    P = SSD_HEAD_DIM
    nchunk = TS // L
    d0 = H if backward else 0
    gw = SSD_INNER // SSD_GROUPS

    @pl.when(i == 0)
    def _():
        hst[...] = jnp.zeros_like(hst)

    if backward:
        t = n - 1 - i
        rows = TS + 2 * HALO
        xpad[0:HALO, :] = jnp.where(t == 0, 0.0, xp_ref[...])
        xpad[HALO:HALO + TS, :] = xc_ref[...]
        xpad[HALO + TS:rows, :] = jnp.where(t == n - 1, 0.0, xn_ref[...])
        CR = SSD_CONV_ROWS
        ng = CR // HALO
        sub = lax.broadcasted_iota(jnp.int32, (1, HALO, 1), 1)

        def conv_rows(rb, carry):
            r0 = pl.multiple_of(rb * CR, CR)
            xg = xpad[pl.ds(r0, CR + 2 * HALO), :].reshape(ng + 2, HALO, SSD_XBC)
            acc = cb_ref[...] + cw_ref[SSD_CONV // 2:SSD_CONV // 2 + 1, :] * xg[1:ng + 1]
            for j in range(SSD_CONV):
                d = j - SSD_CONV // 2
                if d > 0:
                    u = pltpu.roll(xg, HALO - d, 1)
                    acc = acc + cw_ref[j:j + 1, :] * jnp.where(sub < HALO - d, u[1:ng + 1], u[2:ng + 2])
                elif d < 0:
                    u = pltpu.roll(xg, -d, 1)
                    acc = acc + cw_ref[j:j + 1, :] * jnp.where(sub >= -d, u[1:ng + 1], u[0:ng])
            xa_ref[pl.ds(r0, CR), :] = _silu(acc).reshape(CR, SSD_XBC).astype(xa_ref.dtype)
            return carry

        lax.fori_loop(0, TS // CR, conv_rows, 0)

    a_row = -jnp.exp(alog_ref[...])[:, d0:d0 + H]
    r_i = lax.broadcasted_iota(jnp.int32, (L, L), 0)
    c_i = lax.broadcasted_iota(jnp.int32, (L, L), 1)
    keep = (c_i >= r_i) if backward else (c_i <= r_i)
    tri = jnp.where(keep, 1.0, 0.0).astype(BF16)
    r_2 = lax.broadcasted_iota(jnp.int32, (L, 2 * L), 0)
    c_2 = lax.broadcasted_iota(jnp.int32, (L, 2 * L), 1)
    c_2 = jnp.where(c_2 >= L, c_2 - L, c_2)
    keep2 = (c_2 >= r_2) if backward else (c_2 <= r_2)
    last = 0 if backward else L - 1
    lane = lax.broadcasted_iota(jnp.int32, (1, 2 * P), 1)
    hp = H // 2

    css, ecss, scales = [], [], []
    for c in range(nchunk):
        dt = jax.nn.softplus(dt_ref[c * L:(c + 1) * L, :] + dtb_ref[...])[:, d0:d0 + H]
        w_hi, w_lo = _split_hi_lo(dt * a_row)
        cs = _dot(tri, w_hi) + _dot(tri, w_lo)
        css.append(cs)
        ecss.append(jnp.exp(cs))
        scales.append(dt * jnp.exp(cs[last:last + 1, :] - cs))
        tt = jnp.concatenate([cs, dt, jnp.zeros((L, 2 * L - 2 * H), F32)], axis=1)
        tt = jnp.concatenate([tt, jnp.zeros_like(tt)], axis=0).T
        for q, src in enumerate((tt[0:H], tt[H:2 * H])):
            rowp_s[q * hp:(q + 1) * hp, c * 2 * L:(c + 1) * 2 * L] = jnp.concatenate(
                [src[0:hp, 0:L], src[hp:H, 0:L]], axis=1)
    csb_s[...] = _dot(_split3_tiled(jnp.concatenate(css, axis=0)), selb_ref[...])
    ex_s[...] = _dot(_split3_tiled(jnp.concatenate(ecss, axis=0)), selx_ref[...])
    sx_s[...] = _dot(_split3_tiled(jnp.concatenate(scales, axis=0)), selx_ref[...])

    def chunk(ci, carry):
        c = (nchunk - 1 - ci) if backward else ci
        r0 = pl.multiple_of(c * L, L)
        p0 = pl.multiple_of(c * 2 * L, 2 * L)
        csb = csb_s[pl.ds(r0, L), :]
        ex = ex_s[pl.ds(r0, L), :]
        sx = sx_s[pl.ds(r0, L), :]
        csp = rowp_s[0:hp, pl.ds(p0, 2 * L)]
        dtp = rowp_s[hp:H, pl.ds(p0, 2 * L)].astype(BF16)
        etx = ex[last:last + 1, :]
        xs = xa_ref[pl.ds(r0, L), 0:SSD_INNER]
        bm = xa_ref[pl.ds(r0, L), SSD_INNER:SSD_INNER + SSD_GROUPS * SSD_STATE]
        cm = xa_ref[pl.ds(r0, L), SSD_INNER + SSD_GROUPS * SSD_STATE:SSD_XBC]
        xw = xs * sx.astype(BF16)
        cbs, yoffs = [], []
        for g in range(SSD_GROUPS):
            bg = bm[:, g * SSD_STATE:(g + 1) * SSD_STATE]
            cg = cm[:, g * SSD_STATE:(g + 1) * SSD_STATE]
            cb = _dot_nt(cg, bg)
            cbs.append(jnp.concatenate([cb, cb], axis=1).astype(BF16))
            hg = hst[g]
            yoffs.append(_dot(cg, hg.astype(BF16)) * ex[:, g * gw:(g + 1) * gw])
            hst[g] = etx[:, g * gw:(g + 1) * gw] * hg + _dot_tn(bg, xw[:, g * gw:(g + 1) * gw])
        ys = []
        for j in range(hp):
            seg = csb[:, j * 2 * L:(j + 1) * 2 * L] - csp[j:j + 1, :]
            dm = jnp.exp(jnp.where(keep2, seg, NEG)).astype(BF16) * dtp[j:j + 1, :]
            m = cbs[2 * j // (H // SSD_GROUPS)] * dm
            xp2 = xs[:, 2 * j * P:(2 * j + 2) * P]
            zero = jnp.zeros_like(xp2)
            rhs = jnp.concatenate([jnp.where(lane < P, xp2, zero), jnp.where(lane >= P, xp2, zero)], axis=0)
            ys.append(_dot(m, rhs))
        y = jnp.concatenate(ys, axis=1) + jnp.concatenate(yoffs, axis=1)
        if backward:
            y_ref[pl.ds(r0, L), :] = y
        else:
            y = y + yb_ref[pl.ds(r0, L), :] + dsk_ref[...] * xs.astype(F32)
            y = y * _silu(z_ref[pl.ds(r0, L), :])
            outs = []
            for g in range(SSD_GROUPS):
                yg = y[:, g * gw:(g + 1) * gw]
                outs.append(yg * lax.rsqrt(jnp.mean(yg * yg, axis=-1, keepdims=True) + EPS))
            y_ref[pl.ds(r0, L), :] = (jnp.concatenate(outs, axis=1) * nw_ref[...]).astype(y_ref.dtype)
        return carry

    lax.fori_loop(0, nchunk, chunk, 0)


def ssd_pass(backward, src, dt, alog, dtb, B, S, extra):
    T = B * S
    nS = S // TS
    per = TS // HALO
    nh2 = 2 * SSD_HEADS
    selb, selx = (jnp.asarray(m, BF16) for m in _ssd_constants())

    def blk(b, i):
        return b * nS + ((nS - 1 - i) if backward else i)

    row = lambda c: pl.BlockSpec((TS, c), lambda b, i: (blk(b, i), 0))
    dt_specs = [row(nh2)]
    par_specs = [_const_spec(alog.shape), _const_spec(dtb.shape),
                 _const_spec(selb.shape), _const_spec(selx.shape)]
    state = [pltpu.VMEM((SSD_GROUPS, SSD_STATE, SSD_INNER // SSD_GROUPS), F32),
             pltpu.VMEM((TS, SSD_HEADS * SSD_L), F32), pltpu.VMEM((TS, SSD_INNER), F32),
             pltpu.VMEM((TS, SSD_INNER), F32), pltpu.VMEM((SSD_HEADS, 2 * TS), F32)]
    if backward:
        cw, cb = extra
        in_specs = ([row(SSD_XBC),
                     pl.BlockSpec((HALO, SSD_XBC), lambda b, i: (jnp.maximum(blk(b, i) * per - 1, 0), 0)),
                     pl.BlockSpec((HALO, SSD_XBC),
                                  lambda b, i: (jnp.minimum((blk(b, i) + 1) * per, T // HALO - 1), 0))]
                    + dt_specs + [_const_spec(cw.shape), _const_spec(cb.shape)] + par_specs)
        args = [src, src, src, dt, cw, cb, alog, dtb, selb, selx]
        out_specs = [row(SSD_INNER), row(SSD_XBC)]
        out_shape = [jax.ShapeDtypeStruct((T, SSD_INNER), F32), jax.ShapeDtypeStruct((T, SSD_XBC), BF16)]
        scratch = [pltpu.VMEM((TS + 2 * HALO, SSD_XBC), F32)] + state
    else:
        yb, z, dsk, nw = extra
        in_specs = ([row(SSD_XBC)] + dt_specs + par_specs
                    + [row(SSD_INNER), row(SSD_INNER), _const_spec(dsk.shape), _const_spec(nw.shape)])
        args = [src, dt, alog, dtb, selb, selx, yb, z, dsk, nw]
        out_specs = row(SSD_INNER)
        out_shape = jax.ShapeDtypeStruct((T, SSD_INNER), BF16)
        scratch = state
    return pl.pallas_call(
        functools.partial(_ssd_body, backward),
        grid=(B, nS),
        in_specs=in_specs,
        out_specs=out_specs,
        out_shape=out_shape,
        scratch_shapes=scratch,
        compiler_params=_cparams("parallel", "arbitrary"),
        name="ssd_bwd" if backward else "ssd_fwd",
    )(*args)


def _hgrn_constants(backward):
    C = HG_C
    t = np.arange(C)[:, None]
    u = np.arange(C)[None, :]
    if backward:
        t, u = C - 1 - t, C - 1 - u
    return ((t // HG_SUB == u // HG_SUB) & (u <= t)).astype(np.float32)


def _rows_to_tile(rows):
    return jnp.concatenate([jnp.broadcast_to(r, (HG_SUB, r.shape[1])) for r in rows], axis=0)


def _hgrn_body(backward, *refs):
    if backward:
        (q_ref, k_ref, l_ref, v_ref, cum_ref, msk_ref, o_ref, st) = refs
    else:
        (q_ref, k_ref, l_ref, v_ref, cum_ref, msk_ref, ob_ref, g_ref, nw_ref, o_ref, st) = refs
    i = pl.program_id(1)
    C, NB = HG_C, HG_NB
    nseq = q_ref.shape[0]
    nchunk = q_ref.shape[1] // C
    order = list(range(NB - 1, -1, -1)) if backward else list(range(NB))
    end_row = 0 if backward else HG_SUB - 1

    @pl.when(i == 0)
    def _():
        st[...] = jnp.zeros_like(st)

    def chunk(ci, carry):
        c = (nchunk - 1 - ci) if backward else ci
        r0 = pl.multiple_of(c * C, C)
        cum = cum_ref[...]
        hsl = [slice(h * HG_K, (h + 1) * HG_K) for h in range(HG_HEADS)]
        lane_blk = lax.broadcasted_iota(jnp.int32, (HG_SUB, C), 1) // HG_SUB

        def load(heads, t):
            t["k"] = {h: k_ref[h[0], pl.ds(r0, C), hsl[h[1]]].astype(F32) for h in heads}
            t["q"] = {h: q_ref[h[0], pl.ds(r0, C), hsl[h[1]]].astype(F32) for h in heads}
            t["v"] = {h: v_ref[h[0], pl.ds(r0, C), hsl[h[1]]] for h in heads}
            hl = {h: jnp.concatenate(_split_hi_lo(l_ref[h[0], pl.ds(r0, C), hsl[h[1]]]), axis=1)
                  for h in heads}
            t["gl"] = {h: _dot(cum, hl[h]) for h in heads}

        def operands(heads, t):
            for name in ("qt", "qd", "kt", "kh", "qin", "kst", "dec"):
                t[name] = {}
            for h in heads:
                gl = t["gl"][h][:, :HG_K] + t["gl"][h][:, HG_K:]
                tot = [gl[j * HG_SUB + end_row:j * HG_SUB + end_row + 1, :] for j in range(NB)]
                zero = jnp.zeros_like(tot[0])
                pre, post, near = [None] * NB, [None] * NB, [[None] * NB for _ in range(NB)]
                for p, b in enumerate(order):
                    pre[b] = sum((tot[order[pp]] for pp in range(p)), zero)
                    post[b] = sum((tot[order[pp]] for pp in range(p + 1, NB)), zero)
                    for d in range(2, NB):
                        near[d][b] = sum((tot[order[pp]] for pp in range(max(p - d + 1, 0), p)), zero)
                qt = (t["q"][h] * jnp.exp(gl)).astype(BF16)
                kt = (t["k"][h] * jnp.exp(-gl)).astype(BF16)
                bexp = lambda rows: _rows_to_tile([jnp.exp(x).astype(BF16) for x in rows])
                kh = kt * bexp(tot)
                t["qin"][h] = qt * bexp(pre)
                t["kst"][h] = kh * bexp(post)
                qd = []
                for d in range(1, NB):
                    blocks = sorted(order[d:])
                    rows = slice(blocks[0] * HG_SUB, (blocks[-1] + 1) * HG_SUB)
                    qd.append(qt[rows] if d == 1 else qt[rows] * bexp([near[d][b] for b in blocks]))
                t["qd"][h] = jnp.concatenate(qd, axis=0)
                t["qt"][h], t["kt"][h], t["kh"][h] = qt, kt, kh
                t["dec"][h] = jnp.exp(sum(tot, zero))

        def scores(heads, t):
            t["att0"] = {h: _dot_nt(t["qt"][h], t["kt"][h]) for h in heads}
            t["offs"] = {h: _dot_nt(t["qd"][h], t["kh"][h]) for h in heads}

        def assemble(heads, t):
            t["att"] = {}
            for h in heads:
                rows_out = []
                for p, b in enumerate(order):
                    r = t["att0"][h][b * HG_SUB:(b + 1) * HG_SUB] * msk_ref[b * HG_SUB:(b + 1) * HG_SUB, :]
                    base = 0
                    for d in range(1, NB):
                        blocks = sorted(order[d:])
                        if p >= d:
                            o0 = base + (b - blocks[0]) * HG_SUB
                            r = jnp.where(lane_blk == order[p - d], t["offs"][h][o0:o0 + HG_SUB], r)
                        base += len(blocks) * HG_SUB
                    rows_out.append((b, r))
                t["att"][h] = jnp.concatenate([r for _, r in sorted(rows_out, key=lambda x: x[0])],
                                              axis=0).astype(BF16)
            t["sT"] = {h: st[h[0], h[1]] for h in heads}
            t["out"] = {h: _dot(t["att"][h], t["v"][h]) + _dot_nt(t["qin"][h], t["sT"][h].astype(BF16))
                        for h in heads}
            t["upd"] = {h: _dot_tn(t["v"][h], t["kst"][h]) for h in heads}

        def finish(heads, t):
            for h in heads:
                rows, cols = (h[0], pl.ds(r0, C)), hsl[h[1]]
                st[h[0], h[1]] = t["sT"][h] * t["dec"][h] + t["upd"][h]
                o = t["out"][h]
                if backward:
                    o_ref[rows + (cols,)] = o.astype(o_ref.dtype)
                else:
                    o = o + ob_ref[rows + (cols,)].astype(F32)
                    o = o * lax.rsqrt(jnp.mean(o * o, axis=-1, keepdims=True) + EPS) * nw_ref[...]
                    o_ref[rows + (cols,)] = (o * g_ref[rows + (cols,)].astype(F32)).astype(o_ref.dtype)

        t = {}
        units = [(sq, h) for sq in range(nseq) for h in range(HG_HEADS)]
        for stage in (load, operands, scores, assemble, finish):
            stage(units, t)
        return carry

    lax.fori_loop(0, nchunk, chunk, 0)


def hgrn_pass(backward, q, k, logf, v, B, S, extra=()):
    T = B * S
    nseq = HG_SEQS if B % HG_SEQS == 0 else 1
    ts = HG_TS if nseq > 1 else TS
    nS = S // ts
    masks = jnp.asarray(_hgrn_constants(backward), F32)
    cum = masks.astype(BF16)

    tile = pl.BlockSpec((nseq, ts, D_MODEL), lambda b, i: (b, (nS - 1 - i) if backward else i, 0))
    seqs = lambda a: a.reshape(B, S, D_MODEL)
    in_specs = [tile] * 4 + [_const_spec(cum.shape), _const_spec(masks.shape)]
    args = [seqs(q), seqs(k), seqs(logf), seqs(v), cum, masks]
    if not backward:
        ob, og, nw = extra
        in_specs += [tile, tile, _const_spec(nw.shape)]
        args += [seqs(ob), seqs(og), nw]
    return pl.pallas_call(
        functools.partial(_hgrn_body, backward),
        grid=(B // nseq, nS),
        in_specs=in_specs,
        out_specs=tile,
        out_shape=jax.ShapeDtypeStruct((B, S, D_MODEL), BF16),
        scratch_shapes=[pltpu.VMEM((nseq, HG_HEADS, HG_V, HG_K), F32)],
        compiler_params=_cparams("parallel", "arbitrary"),
        name="hgrn_bwd" if backward else "hgrn_fwd",
    )(*args).reshape(T, D_MODEL)


def _t5_bucket(rel):
    half = T5_BUCKETS // 2
    max_exact = half // 2
    n = np.abs(rel)
    large = max_exact + (np.log(np.maximum(n, 1) / max_exact)
                         / np.log(T5_MAX_DIST / max_exact) * (half - max_exact)).astype(np.int32)
    large = np.minimum(large, half - 1)
    return ((rel > 0).astype(np.int32) * half + np.where(n < max_exact, n, large)).astype(np.int32)


def _attention_bias(t5_bias):
    qi = np.arange(A_BLOCK)[None, :]
    kj = np.arange(3 * A_BLOCK)[:, None] - A_BLOCK
    rel = kj - qi
    onehot = (_t5_bucket(rel)[None] == np.arange(T5_BUCKETS)[:, None, None]).astype(np.float32)
    bias = jnp.einsum("bh,bkq->hkq", t5_bias.astype(F32), onehot, precision=lax.Precision.HIGHEST)
    return jnp.where(np.abs(rel)[None] <= A_BLOCK, bias, NEG)


def _trunk(x, B, S, p):
    for l in range(DEPTH):
        j = l // 2
        g = p["gains"][l]
        if l % 2 == 0:
            e = p["even"][j]
            qkv, z, xbc, dt = in_proj_even(x, g[0:1], e["wqkv"], e["wz"], e["wxbc"], e["wdt"])
            a_out = windowed_attention(qkv, p["attn_bias"], e["sink"], B, S)
            common = (dt, e["a_log"], e["dt_bias"], B, S)
            yb, xact = ssd_pass(True, xbc, *common, extra=(e["conv_w"], e["conv_b"]))
            b_out = ssd_pass(False, xact, *common, extra=(yb, z, e["d_skip"], e["norm_w"]))
            mixes, wouts = [a_out, b_out], [e["wout_a"], e["wout_b"]]
        else:
            o = p["odd"][j]
            q, kf, kb, lf, lbw, v, og = in_proj_odd(x, g[0:1], o["lb"], o["w_in"])
            ob = hgrn_pass(True, q, kb, lbw, v, B, S)
            mix = hgrn_pass(False, q, kf, lf, v, B, S, extra=(ob, og, o["norm_w"]))
            mixes, wouts = [mix], [o["w_out"]]
        f = p["ffn"][l]
        x = out_proj_ffn(mixes, wouts, x, g[1:4], f["wg"], f["wu"], f["wd"])
    return x


def kernel(x_prompt, x_sample, norm_gains, t5_bias, ev_w_in, attn_sink, ssd_conv_w, ssd_conv_b, ssd_a_log,
           ssd_dt_bias, ssd_d, ssd_norm_w, ev_w_out, od_w_in, hg_lower_bounds, hg_norm_w, od_w_out,
           ffn_w_gate, ffn_w_up, ffn_w_down):
    nh2 = 2 * SSD_HEADS
    c_qkv = A_Q_DIM + 2 * A_KV_DIM
    c_z = c_qkv + SSD_INNER
    c_xbc = c_z + SSD_XBC
    dt_cols = np.array(SSD_HEAD_ORDER + [SSD_HEADS + h for h in SSD_HEAD_ORDER])
    lb_soft = jax.nn.softmax(hg_lower_bounds.astype(F32), axis=0)
    lb_all = jnp.cumsum(lb_soft, axis=0) - lb_soft[0]
    p = {"gains": norm_gains.astype(F32), "attn_bias": _attention_bias(t5_bias), "even": [], "odd": [], "ffn": []}
    for j in range(ev_w_in.shape[0]):
        w = ev_w_in[j]
        p["even"].append(dict(
            wqkv=w[:, :c_qkv].astype(BF16), wz=w[:, c_qkv:c_z].astype(BF16), wxbc=w[:, c_z:c_xbc].astype(BF16),
            wdt=w[:, c_xbc:][:, dt_cols].astype(BF16),
            sink=attn_sink[j].astype(F32).reshape(1, A_HEADS),
            conv_w=ssd_conv_w[j].astype(F32), conv_b=ssd_conv_b[j].astype(F32).reshape(1, SSD_XBC),
            a_log=ssd_a_log[j].astype(F32).reshape(1, nh2)[:, dt_cols],
            dt_bias=ssd_dt_bias[j].astype(F32).reshape(1, nh2)[:, dt_cols],
            d_skip=jnp.repeat(ssd_d[j].astype(F32), SSD_HEAD_DIM).reshape(1, SSD_INNER),
            norm_w=ssd_norm_w[j].astype(F32).reshape(1, SSD_INNER),
            wout_a=ev_w_out[j, :A_Q_DIM].astype(BF16), wout_b=ev_w_out[j, A_Q_DIM:].astype(BF16)))
    for j in range(od_w_in.shape[0]):
        p["odd"].append(dict(
            w_in=od_w_in[j].astype(BF16), lb=lb_all[j].reshape(1, HG_HEADS * HG_K),
            norm_w=hg_norm_w[j].astype(F32).reshape(1, HG_V), w_out=od_w_out[j].astype(BF16)))
    for l in range(DEPTH):
        p["ffn"].append(dict(wg=ffn_w_gate[l].astype(BF16), wu=ffn_w_up[l].astype(BF16),
                             wd=ffn_w_down[l].astype(BF16)))
    outs = []
    for x in (x_prompt, x_sample):
        B, S, _ = x.shape
        outs.append(_trunk(x.reshape(B * S, D_MODEL), B, S, p).reshape(B, S, D_MODEL))
    return tuple(outs)
```

```python
import functools

import numpy as np
import jax
import jax.numpy as jnp
from jax import lax
from jax.experimental import pallas as pl
from jax.experimental.pallas import tpu as pltpu

F32 = jnp.float32
BF16 = jnp.bfloat16

D_MODEL = 1024
DEPTH = 4
EPS = 1e-6
NEG = -1e30

A_HEADS = 8
A_KV_HEADS = 2
A_HEAD_DIM = 64
A_BLOCK = 128
A_Q_DIM = A_HEADS * A_HEAD_DIM
A_KV_DIM = A_KV_HEADS * A_HEAD_DIM
T5_BUCKETS = 32
T5_MAX_DIST = 128

SSD_HEADS = 16
SSD_HEAD_DIM = 64
SSD_INNER = SSD_HEADS * SSD_HEAD_DIM
SSD_GROUPS = 2
SSD_STATE = 64
SSD_CONV = 5
SSD_XBC = SSD_INNER + 2 * SSD_GROUPS * SSD_STATE
SSD_L = 128
SSD_CONV_ROWS = 128
HALO = 8

HG_HEADS = 8
HG_K = 128
HG_V = 128
HG_C = 128
HG_SUB = 32
HG_NB = HG_C // HG_SUB
HG_SEQS = 8
HG_TS = 128

D_FF = 2816
MXU_N = 256
FF_CHUNKS = ((0, 6 * MXU_N), (6 * MXU_N, D_FF))

TM = 512
TS = 512
VMEM_LIMIT_BYTES = 56 * 1024 * 1024


def _cparams(*sem):
    return pltpu.CompilerParams(dimension_semantics=sem, vmem_limit_bytes=VMEM_LIMIT_BYTES)


def _const_spec(shape):
    nd = len(shape)
    return pl.BlockSpec(shape, lambda *_: (0,) * nd, pipeline_mode=pl.Buffered(1))


def _rms(x, g):
    return x * lax.rsqrt(jnp.mean(x * x, axis=-1, keepdims=True) + EPS) * g


def _silu(x):
    return x * (1.0 / (1.0 + jnp.exp(-x)))


def _dot(a, b):
    return jnp.dot(a, b, preferred_element_type=F32)


def _dot_nt(a, b):
    return lax.dot_general(a, b, (((1,), (1,)), ((), ())), preferred_element_type=F32)


def _dot_tn(a, b):
    return lax.dot_general(a, b, (((0,), (0,)), ((), ())), preferred_element_type=F32)


def _split_hi_lo(x):
    hi = x.astype(BF16)
    lo = (x - hi.astype(F32)).astype(BF16)
    return hi, lo


def _in_even_body(x_ref, g_ref, wqkv_ref, wz_ref, wxbc_ref, wdt_ref, qkv_ref, z_ref, xbc_ref, dt_ref):
    xn = _rms(x_ref[...], g_ref[...]).astype(BF16)
    qkv_ref[...] = _dot(xn, wqkv_ref[...]).astype(BF16)
    z_ref[...] = _dot(xn, wz_ref[...])
    xbc_ref[...] = _dot(xn, wxbc_ref[...])
    dt_ref[...] = _dot(xn, wdt_ref[...])


def in_proj_even(x, g, wqkv, wz, wxbc, wdt):
    T = x.shape[0]
    nh2 = 2 * SSD_HEADS
    row = lambda c: pl.BlockSpec((TM, c), lambda i: (i, 0))
    return pl.pallas_call(
        _in_even_body,
        grid=(T // TM,),
        in_specs=[row(D_MODEL), _const_spec((1, D_MODEL)), _const_spec(wqkv.shape), _const_spec(wz.shape),
                  _const_spec(wxbc.shape), _const_spec(wdt.shape)],
        out_specs=[row(A_Q_DIM + 2 * A_KV_DIM), row(SSD_INNER), row(SSD_XBC), row(nh2)],
        out_shape=[jax.ShapeDtypeStruct((T, A_Q_DIM + 2 * A_KV_DIM), BF16),
                   jax.ShapeDtypeStruct((T, SSD_INNER), F32),
                   jax.ShapeDtypeStruct((T, SSD_XBC), F32),
                   jax.ShapeDtypeStruct((T, nh2), F32)],
        compiler_params=_cparams("parallel"),
        name="in_proj_even",
    )(x, g, wqkv, wz, wxbc, wdt)


def _in_odd_body(x_ref, g_ref, lb_ref, w_ref, q_ref, kf_ref, kb_ref, lf_ref, lbw_ref, v_ref, og_ref):
    xn = _rms(x_ref[...], g_ref[...]).astype(BF16)
    W = D_MODEL
    seg = lambda j: _dot(xn, w_ref[:, j * W:(j + 1) * W])
    q_ref[...] = _silu(seg(0)).astype(BF16)
    lb = lb_ref[...]
    for j, k_ref, l_ref in ((1, kf_ref, lf_ref), (2, kb_ref, lbw_ref)):
        sg = 1.0 / (1.0 + jnp.exp(-seg(j)))
        k_ref[...] = ((1.0 - lb) * (1.0 - sg)).astype(BF16)
        l_ref[...] = jnp.log(lb + (1.0 - lb) * sg)
    v_ref[...] = seg(3).astype(BF16)
    og_ref[...] = _silu(seg(4)).astype(BF16)


def in_proj_odd(x, g, lb, w):
    T = x.shape[0]
    row = pl.BlockSpec((TM, D_MODEL), lambda i: (i, 0))
    dtypes = [BF16, BF16, BF16, F32, F32, BF16, BF16]
    return pl.pallas_call(
        _in_odd_body,
        grid=(T // TM,),
        in_specs=[row, _const_spec((1, D_MODEL)), _const_spec(lb.shape), _const_spec(w.shape)],
        out_specs=[row] * len(dtypes),
        out_shape=[jax.ShapeDtypeStruct((T, D_MODEL), d) for d in dtypes],
        compiler_params=_cparams("parallel"),
        name="in_proj_odd",
    )(x, g, lb, w)


def _out_ffn_body(n_mix, *refs):
    mix_refs = refs[:n_mix]
    wout_refs = refs[n_mix:2 * n_mix]
    x_ref, g_ref, wg_ref, wu_ref, wd_ref, o_ref = refs[2 * n_mix:]
    halves = [slice(0, TM // 2), slice(TM // 2, TM)]
    ms = []
    for r in halves:
        m = _dot(mix_refs[0][r, :], wout_refs[0][...])
        for a_ref, w_ref in zip(mix_refs[1:], wout_refs[1:]):
            m = m + _dot(a_ref[r, :], w_ref[...])
        ms.append(m)
    x1s = [x_ref[r, :] + _rms(m, g_ref[0:1, :]) for r, m in zip(halves, ms)]
    hns = [_rms(x1, g_ref[1:2, :]).astype(BF16) for x1 in x1s]
    fs = [None] * len(halves)
    for c0, c1 in FF_CHUNKS:
        gates = [_dot(hn, wg_ref[:, c0:c1]) for hn in hns]
        ups = [_dot(hn, wu_ref[:, c0:c1]) for hn in hns]
        hs = [(_silu(gate) * up).astype(BF16) for gate, up in zip(gates, ups)]
        parts = [_dot(h, wd_ref[c0:c1, :]) for h in hs]
        fs = [part if f is None else f + part for f, part in zip(fs, parts)]
    for r, x1, f in zip(halves, x1s, fs):
        o_ref[r, :] = x1 + _rms(f, g_ref[2:3, :])


def out_proj_ffn(mixes, wouts, x, g3, wg, wu, wd):
    T = x.shape[0]
    n_mix = len(mixes)
    row = lambda c: pl.BlockSpec((TM, c), lambda i: (i, 0))
    in_specs = ([row(a.shape[1]) for a in mixes] + [_const_spec(w.shape) for w in wouts]
                + [row(D_MODEL), _const_spec(g3.shape), _const_spec(wg.shape), _const_spec(wu.shape),
                   _const_spec(wd.shape)])
    return pl.pallas_call(
        functools.partial(_out_ffn_body, n_mix),
        grid=(T // TM,),
        in_specs=in_specs,
        out_specs=row(D_MODEL),
        out_shape=jax.ShapeDtypeStruct((T, D_MODEL), F32),
        compiler_params=_cparams("parallel"),
        name="out_proj_ffn",
    )(*mixes, *wouts, x, g3, wg, wu, wd)


def _attn_body(q_ref, kp_ref, kc_ref, kn_ref, vp_ref, vc_ref, vn_ref, bias_ref, sink_ref, o_ref, kbuf, vtbuf):
    i = pl.program_id(1)
    n = pl.num_programs(1)
    nsub = TS // A_BLOCK
    dh = A_HEAD_DIM
    nk = TS + 2 * A_BLOCK
    row = lax.broadcasted_iota(jnp.int32, (2 * dh, nk), 0)
    for g in range(A_KV_HEADS):
        vs = []
        for lo, k_ref, v_ref in ((0, kp_ref, vp_ref), (A_BLOCK, kc_ref, vc_ref), (A_BLOCK + TS, kn_ref, vn_ref)):
            kbuf[g, lo:lo + k_ref.shape[0], :] = k_ref[:, g * dh:(g + 1) * dh]
            vs.append(v_ref[...])
        vt = jnp.concatenate(vs, axis=0).astype(F32).T[g * dh:(g + 1) * dh]
        vt = jnp.concatenate([vt, jnp.zeros_like(vt)], axis=0)
        vtbuf[g] = jnp.where(row == dh, 1.0, vt).astype(BF16)
    key = lax.broadcasted_iota(jnp.int32, (3 * A_BLOCK, A_BLOCK), 0)
    rep = A_HEADS // A_KV_HEADS
    heads = range(A_HEADS)

    def scores(sb):
        r0 = sb * A_BLOCK
        return [_dot_nt(kbuf[h // rep, r0:r0 + 3 * A_BLOCK, :], q_ref[r0:r0 + A_BLOCK, h * dh:(h + 1) * dh])
                for h in heads]

    def softmax_numerators(sb, ss):
        edge = None
        if sb == 0:
            edge = jnp.where(key < jnp.where(i == 0, A_BLOCK, 0), NEG, 0.0)
        if sb == nsub - 1:
            e2 = jnp.where(key >= jnp.where(i == n - 1, 2 * A_BLOCK, 3 * A_BLOCK), NEG, 0.0)
            edge = e2 if edge is None else edge + e2
        ps, ms = [], []
        for h in heads:
            s = ss[h] * (dh ** -0.5) + bias_ref[h]
            if edge is not None:
                s = s + edge
            m = jnp.maximum(jnp.max(s, axis=0, keepdims=True), sink_ref[0:1, h:h + 1])
            ps.append(jnp.exp(s - m).astype(BF16))
            ms.append(m)
        return ps, ms

    def weighted_values(sb, ps):
        r0 = sb * A_BLOCK
        return [_dot(vtbuf[h // rep, :, r0:r0 + 3 * A_BLOCK], ps[h]) for h in heads]

    def finish(sb, pv, ms):
        r0 = sb * A_BLOCK
        outs = []
        for h in heads:
            denom = pv[h][dh:dh + 1, :] + jnp.exp(sink_ref[0:1, h:h + 1] - ms[h])
            outs.append(pv[h][:dh, :] / denom)
        o_ref[r0:r0 + A_BLOCK, :] = jnp.concatenate(outs, axis=0).T.astype(o_ref.dtype)

    ss, pm, pv = {0: scores(0)}, {}, {}
    for sb in range(nsub + 1):
        if sb + 1 < nsub:
            ss[sb + 1] = scores(sb + 1)
        if sb < nsub:
            pm[sb] = softmax_numerators(sb, ss.pop(sb))
            pv[sb] = weighted_values(sb, pm[sb][0])
        if sb >= 1:
            finish(sb - 1, pv.pop(sb - 1), pm.pop(sb - 1)[1])


def windowed_attention(qkv, bias, sink, B, S):
    T = B * S
    nS = S // TS
    sub = TS // A_BLOCK
    nblk = T // A_BLOCK
    kcol = A_Q_DIM // A_KV_DIM
    vcol = kcol + 1
    cur = lambda c: pl.BlockSpec((TS, A_KV_DIM), lambda b, i: (b * nS + i, c))
    prev = lambda c: pl.BlockSpec((A_BLOCK, A_KV_DIM),
                                  lambda b, i: (jnp.maximum((b * nS + i) * sub - 1, 0), c))
    nxt = lambda c: pl.BlockSpec((A_BLOCK, A_KV_DIM),
                                 lambda b, i: (jnp.minimum((b * nS + i + 1) * sub, nblk - 1), c))
    return pl.pallas_call(
        _attn_body,
        grid=(B, nS),
        in_specs=[pl.BlockSpec((TS, A_Q_DIM), lambda b, i: (b * nS + i, 0)),
                  prev(kcol), cur(kcol), nxt(kcol), prev(vcol), cur(vcol), nxt(vcol),
                  _const_spec(bias.shape), _const_spec(sink.shape)],
        out_specs=pl.BlockSpec((TS, A_Q_DIM), lambda b, i: (b * nS + i, 0)),
        out_shape=jax.ShapeDtypeStruct((T, A_Q_DIM), BF16),
        scratch_shapes=[pltpu.VMEM((A_KV_HEADS, TS + 2 * A_BLOCK, A_HEAD_DIM), BF16),
                        pltpu.VMEM((A_KV_HEADS, 2 * A_HEAD_DIM, TS + 2 * A_BLOCK), BF16)],
        compiler_params=_cparams("parallel", "parallel"),
        name="windowed_attention",
    )(qkv, qkv, qkv, qkv, qkv, qkv, qkv, bias, sink)


def _split3_tiled(x):
    h = x.shape[1]
    t = jnp.concatenate([x, x, x], axis=1)
    hi = t.astype(BF16).astype(F32)
    r1 = t - hi
    mid = r1.astype(BF16).astype(F32)
    lane = lax.broadcasted_iota(jnp.int32, t.shape, 1)
    return jnp.where(lane < h, hi, jnp.where(lane < 2 * h, mid, r1 - mid)).astype(BF16)


SSD_HEAD_ORDER = list(range(0, SSD_HEADS, 2)) + list(range(1, SSD_HEADS, 2))


def _ssd_constants():
    H, P, L = SSD_HEADS, SSD_HEAD_DIM, SSD_L
    selb = np.zeros((3 * H, H * L), np.float32)
    selx = np.zeros((3 * H, H * P), np.float32)
    for c, head in enumerate(SSD_HEAD_ORDER):
        for part in range(3):
            selb[part * H + c, head * L:(head + 1) * L] = 1.0
            selx[part * H + c, head * P:(head + 1) * P] = 1.0
    return selb, selx


def _ssd_body(backward, *refs):
    if backward:
        (xc_ref, xp_ref, xn_ref, dt_ref, cw_ref, cb_ref, alog_ref, dtb_ref,
         selb_ref, selx_ref, y_ref, xa_ref, xpad, hst, csb_s, ex_s, sx_s, rowp_s) = refs
    else:
        (xa_ref, dt_ref, alog_ref, dtb_ref, selb_ref, selx_ref,
         yb_ref, z_ref, dsk_ref, nw_ref, y_ref, hst, csb_s, ex_s, sx_s, rowp_s) = refs
    i = pl.program_id(1)
    n = pl.num_programs(1)
    L = SSD_L
    H = SSD_HEADS
    P = SSD_HEAD_DIM
    nchunk = TS // L
    d0 = H if backward else 0
    gw = SSD_INNER // SSD_GROUPS

    @pl.when(i == 0)
    def _():
        hst[...] = jnp.zeros_like(hst)

    if backward:
        t = n - 1 - i
        rows = TS + 2 * HALO
        xpad[0:HALO, :] = jnp.where(t == 0, 0.0, xp_ref[...])
        xpad[HALO:HALO + TS, :] = xc_ref[...]
        xpad[HALO + TS:rows, :] = jnp.where(t == n - 1, 0.0, xn_ref[...])
        CR = SSD_CONV_ROWS
        ng = CR // HALO
        sub = lax.broadcasted_iota(jnp.int32, (1, HALO, 1), 1)

        def conv_rows(rb, carry):
            r0 = pl.multiple_of(rb * CR, CR)
            xg = xpad[pl.ds(r0, CR + 2 * HALO), :].reshape(ng + 2, HALO, SSD_XBC)
            acc = cb_ref[...] + cw_ref[SSD_CONV // 2:SSD_CONV // 2 + 1, :] * xg[1:ng + 1]
            for j in range(SSD_CONV):
                d = j - SSD_CONV // 2
                if d > 0:
                    u = pltpu.roll(xg, HALO - d, 1)
                    acc = acc + cw_ref[j:j + 1, :] * jnp.where(sub < HALO - d, u[1:ng + 1], u[2:ng + 2])
                elif d < 0:
                    u = pltpu.roll(xg, -d, 1)
                    acc = acc + cw_ref[j:j + 1, :] * jnp.where(sub >= -d, u[1:ng + 1], u[0:ng])
            xa_ref[pl.ds(r0, CR), :] = _silu(acc).reshape(CR, SSD_XBC).astype(xa_ref.dtype)
            return carry

        lax.fori_loop(0, TS // CR, conv_rows, 0)

    a_row = -jnp.exp(alog_ref[...])[:, d0:d0 + H]
    r_i = lax.broadcasted_iota(jnp.int32, (L, L), 0)
    c_i = lax.broadcasted_iota(jnp.int32, (L, L), 1)
    keep = (c_i >= r_i) if backward else (c_i <= r_i)
    tri = jnp.where(keep, 1.0, 0.0).astype(BF16)
    r_2 = lax.broadcasted_iota(jnp.int32, (L, 2 * L), 0)
    c_2 = lax.broadcasted_iota(jnp.int32, (L, 2 * L), 1)
    c_2 = jnp.where(c_2 >= L, c_2 - L, c_2)
    keep2 = (c_2 >= r_2) if backward else (c_2 <= r_2)
    last = 0 if backward else L - 1
    lane = lax.broadcasted_iota(jnp.int32, (1, 2 * P), 1)
    hp = H // 2

    css, ecss, scales = [], [], []
    for c in range(nchunk):
        dt = jax.nn.softplus(dt_ref[c * L:(c + 1) * L, :] + dtb_ref[...])[:, d0:d0 + H]
        w_hi, w_lo = _split_hi_lo(dt * a_row)
        cs = _dot(tri, w_hi) + _dot(tri, w_lo)
        css.append(cs)
        ecss.append(jnp.exp(cs))
        scales.append(dt * jnp.exp(cs[last:last + 1, :] - cs))
        tt = jnp.concatenate([cs, dt, jnp.zeros((L, 2 * L - 2 * H), F32)], axis=1)
        tt = jnp.concatenate([tt, jnp.zeros_like(tt)], axis=0).T
        for q, src in enumerate((tt[0:H], tt[H:2 * H])):
            rowp_s[q * hp:(q + 1) * hp, c * 2 * L:(c + 1) * 2 * L] = jnp.concatenate(
                [src[0:hp, 0:L], src[hp:H, 0:L]], axis=1)
    csb_s[...] = _dot(_split3_tiled(jnp.concatenate(css, axis=0)), selb_ref[...])
    ex_s[...] = _dot(_split3_tiled(jnp.concatenate(ecss, axis=0)), selx_ref[...])
    sx_s[...] = _dot(_split3_tiled(jnp.concatenate(scales, axis=0)), selx_ref[...])

    def chunk(ci, carry):
        c = (nchunk - 1 - ci) if backward else ci
        r0 = pl.multiple_of(c * L, L)
        p0 = pl.multiple_of(c * 2 * L, 2 * L)
        csb = csb_s[pl.ds(r0, L), :]
        ex = ex_s[pl.ds(r0, L), :]
        sx = sx_s[pl.ds(r0, L), :]
        csp = rowp_s[0:hp, pl.ds(p0, 2 * L)]
        dtp = rowp_s[hp:H, pl.ds(p0, 2 * L)].astype(BF16)
        etx = ex[last:last + 1, :]
        xs = xa_ref[pl.ds(r0, L), 0:SSD_INNER]
        bm = xa_ref[pl.ds(r0, L), SSD_INNER:SSD_INNER + SSD_GROUPS * SSD_STATE]
        cm = xa_ref[pl.ds(r0, L), SSD_INNER + SSD_GROUPS * SSD_STATE:SSD_XBC]
        xw = xs * sx.astype(BF16)
        cbs, yoffs = [], []
        for g in range(SSD_GROUPS):
            bg = bm[:, g * SSD_STATE:(g + 1) * SSD_STATE]
            cg = cm[:, g * SSD_STATE:(g + 1) * SSD_STATE]
            cb = _dot_nt(cg, bg)
            cbs.append(jnp.concatenate([cb, cb], axis=1).astype(BF16))
            hg = hst[g]
            yoffs.append(_dot(cg, hg.astype(BF16)) * ex[:, g * gw:(g + 1) * gw])
            hst[g] = etx[:, g * gw:(g + 1) * gw] * hg + _dot_tn(bg, xw[:, g * gw:(g + 1) * gw])
        ys = []
        for j in range(hp):
            seg = csb[:, j * 2 * L:(j + 1) * 2 * L] - csp[j:j + 1, :]
            dm = jnp.exp(jnp.where(keep2, seg, NEG)).astype(BF16) * dtp[j:j + 1, :]
            m = cbs[2 * j // (H // SSD_GROUPS)] * dm
            xp2 = xs[:, 2 * j * P:(2 * j + 2) * P]
            zero = jnp.zeros_like(xp2)
            rhs = jnp.concatenate([jnp.where(lane < P, xp2, zero), jnp.where(lane >= P, xp2, zero)], axis=0)
            ys.append(_dot(m, rhs))
        y = jnp.concatenate(ys, axis=1) + jnp.concatenate(yoffs, axis=1)
        if backward:
            y_ref[pl.ds(r0, L), :] = y
        else:
            y = y + yb_ref[pl.ds(r0, L), :] + dsk_ref[...] * xs.astype(F32)
            y = y * _silu(z_ref[pl.ds(r0, L), :])
            outs = []
            for g in range(SSD_GROUPS):
                yg = y[:, g * gw:(g + 1) * gw]
                outs.append(yg * lax.rsqrt(jnp.mean(yg * yg, axis=-1, keepdims=True) + EPS))
            y_ref[pl.ds(r0, L), :] = (jnp.concatenate(outs, axis=1) * nw_ref[...]).astype(y_ref.dtype)
        return carry

    lax.fori_loop(0, nchunk, chunk, 0)


def ssd_pass(backward, src, dt, alog, dtb, B, S, extra):
    T = B * S
    nS = S // TS
    per = TS // HALO
    nh2 = 2 * SSD_HEADS
    selb, selx = (jnp.asarray(m, BF16) for m in _ssd_constants())

    def blk(b, i):
        return b * nS + ((nS - 1 - i) if backward else i)

    row = lambda c: pl.BlockSpec((TS, c), lambda b, i: (blk(b, i), 0))
    dt_specs = [row(nh2)]
    par_specs = [_const_spec(alog.shape), _const_spec(dtb.shape),
                 _const_spec(selb.shape), _const_spec(selx.shape)]
    state = [pltpu.VMEM((SSD_GROUPS, SSD_STATE, SSD_INNER // SSD_GROUPS), F32),
             pltpu.VMEM((TS, SSD_HEADS * SSD_L), F32), pltpu.VMEM((TS, SSD_INNER), F32),
             pltpu.VMEM((TS, SSD_INNER), F32), pltpu.VMEM((SSD_HEADS, 2 * TS), F32)]
    if backward:
        cw, cb = extra
        in_specs = ([row(SSD_XBC),
                     pl.BlockSpec((HALO, SSD_XBC), lambda b, i: (jnp.maximum(blk(b, i) * per - 1, 0), 0)),
                     pl.BlockSpec((HALO, SSD_XBC),
                                  lambda b, i: (jnp.minimum((blk(b, i) + 1) * per, T // HALO - 1), 0))]
                    + dt_specs + [_const_spec(cw.shape), _const_spec(cb.shape)] + par_specs)
        args = [src, src, src, dt, cw, cb, alog, dtb, selb, selx]
        out_specs = [row(SSD_INNER), row(SSD_XBC)]
        out_shape = [jax.ShapeDtypeStruct((T, SSD_INNER), F32), jax.ShapeDtypeStruct((T, SSD_XBC), BF16)]
        scratch = [pltpu.VMEM((TS + 2 * HALO, SSD_XBC), F32)] + state
    else:
        yb, z, dsk, nw = extra
        in_specs = ([row(SSD_XBC)] + dt_specs + par_specs
                    + [row(SSD_INNER), row(SSD_INNER), _const_spec(dsk.shape), _const_spec(nw.shape)])
        args = [src, dt, alog, dtb, selb, selx, yb, z, dsk, nw]
        out_specs = row(SSD_INNER)
        out_shape = jax.ShapeDtypeStruct((T, SSD_INNER), BF16)
        scratch = state
    return pl.pallas_call(
        functools.partial(_ssd_body, backward),
        grid=(B, nS),
        in_specs=in_specs,
        out_specs=out_specs,
        out_shape=out_shape,
        scratch_shapes=scratch,
        compiler_params=_cparams("parallel", "arbitrary"),
        name="ssd_bwd" if backward else "ssd_fwd",
    )(*args)


def _hgrn_constants(backward):
    C = HG_C
    t = np.arange(C)[:, None]
    u = np.arange(C)[None, :]
    if backward:
        t, u = C - 1 - t, C - 1 - u
    return ((t // HG_SUB == u // HG_SUB) & (u <= t)).astype(np.float32)


def _rows_to_tile(rows):
    return jnp.concatenate([jnp.broadcast_to(r, (HG_SUB, r.shape[1])) for r in rows], axis=0)


def _hgrn_body(backward, *refs):
    if backward:
        (q_ref, k_ref, l_ref, v_ref, cum_ref, msk_ref, o_ref, st) = refs
    else:
        (q_ref, k_ref, l_ref, v_ref, cum_ref, msk_ref, ob_ref, g_ref, nw_ref, o_ref, st) = refs
    i = pl.program_id(1)
    C, NB = HG_C, HG_NB
    nseq = q_ref.shape[0]
    nchunk = q_ref.shape[1] // C
    order = list(range(NB - 1, -1, -1)) if backward else list(range(NB))
    end_row = 0 if backward else HG_SUB - 1

    @pl.when(i == 0)
    def _():
        st[...] = jnp.zeros_like(st)

    def chunk(ci, carry):
        c = (nchunk - 1 - ci) if backward else ci
        r0 = pl.multiple_of(c * C, C)
        cum = cum_ref[...]
        hsl = [slice(h * HG_K, (h + 1) * HG_K) for h in range(HG_HEADS)]
        lane_blk = lax.broadcasted_iota(jnp.int32, (HG_SUB, C), 1) // HG_SUB

        def load(heads, t):
            t["k"] = {h: k_ref[h[0], pl.ds(r0, C), hsl[h[1]]].astype(F32) for h in heads}
            t["q"] = {h: q_ref[h[0], pl.ds(r0, C), hsl[h[1]]].astype(F32) for h in heads}
            t["v"] = {h: v_ref[h[0], pl.ds(r0, C), hsl[h[1]]] for h in heads}
            hl = {h: jnp.concatenate(_split_hi_lo(l_ref[h[0], pl.ds(r0, C), hsl[h[1]]]), axis=1)
                  for h in heads}
            t["gl"] = {h: _dot(cum, hl[h]) for h in heads}

        def operands(heads, t):
            for name in ("qt", "qd", "kt", "kh", "qin", "kst", "dec"):
                t[name] = {}
            for h in heads:
                gl = t["gl"][h][:, :HG_K] + t["gl"][h][:, HG_K:]
                tot = [gl[j * HG_SUB + end_row:j * HG_SUB + end_row + 1, :] for j in range(NB)]
                zero = jnp.zeros_like(tot[0])
                pre, post, near = [None] * NB, [None] * NB, [[None] * NB for _ in range(NB)]
                for p, b in enumerate(order):
                    pre[b] = sum((tot[order[pp]] for pp in range(p)), zero)
                    post[b] = sum((tot[order[pp]] for pp in range(p + 1, NB)), zero)
                    for d in range(2, NB):
                        near[d][b] = sum((tot[order[pp]] for pp in range(max(p - d + 1, 0), p)), zero)
                qt = (t["q"][h] * jnp.exp(gl)).astype(BF16)
                kt = (t["k"][h] * jnp.exp(-gl)).astype(BF16)
                bexp = lambda rows: _rows_to_tile([jnp.exp(x).astype(BF16) for x in rows])
                kh = kt * bexp(tot)
                t["qin"][h] = qt * bexp(pre)
                t["kst"][h] = kh * bexp(post)
                qd = []
                for d in range(1, NB):
                    blocks = sorted(order[d:])
                    rows = slice(blocks[0] * HG_SUB, (blocks[-1] + 1) * HG_SUB)
                    qd.append(qt[rows] if d == 1 else qt[rows] * bexp([near[d][b] for b in blocks]))
                t["qd"][h] = jnp.concatenate(qd, axis=0)
                t["qt"][h], t["kt"][h], t["kh"][h] = qt, kt, kh
                t["dec"][h] = jnp.exp(sum(tot, zero))

        def scores(heads, t):
            t["att0"] = {h: _dot_nt(t["qt"][h], t["kt"][h]) for h in heads}
            t["offs"] = {h: _dot_nt(t["qd"][h], t["kh"][h]) for h in heads}

        def assemble(heads, t):
            t["att"] = {}
            for h in heads:
                rows_out = []
                for p, b in enumerate(order):
                    r = t["att0"][h][b * HG_SUB:(b + 1) * HG_SUB] * msk_ref[b * HG_SUB:(b + 1) * HG_SUB, :]
                    base = 0
                    for d in range(1, NB):
                        blocks = sorted(order[d:])
                        if p >= d:
                            o0 = base + (b - blocks[0]) * HG_SUB
                            r = jnp.where(lane_blk == order[p - d], t["offs"][h][o0:o0 + HG_SUB], r)
                        base += len(blocks) * HG_SUB
                    rows_out.append((b, r))
                t["att"][h] = jnp.concatenate([r for _, r in sorted(rows_out, key=lambda x: x[0])],
                                              axis=0).astype(BF16)
            t["sT"] = {h: st[h[0], h[1]] for h in heads}
            t["out"] = {h: _dot(t["att"][h], t["v"][h]) + _dot_nt(t["qin"][h], t["sT"][h].astype(BF16))
                        for h in heads}
            t["upd"] = {h: _dot_tn(t["v"][h], t["kst"][h]) for h in heads}

        def finish(heads, t):
            for h in heads:
                rows, cols = (h[0], pl.ds(r0, C)), hsl[h[1]]
                st[h[0], h[1]] = t["sT"][h] * t["dec"][h] + t["upd"][h]
                o = t["out"][h]
                if backward:
                    o_ref[rows + (cols,)] = o.astype(o_ref.dtype)
                else:
                    o = o + ob_ref[rows + (cols,)].astype(F32)
                    o = o * lax.rsqrt(jnp.mean(o * o, axis=-1, keepdims=True) + EPS) * nw_ref[...]
                    o_ref[rows + (cols,)] = (o * g_ref[rows + (cols,)].astype(F32)).astype(o_ref.dtype)

        t = {}
        units = [(sq, h) for sq in range(nseq) for h in range(HG_HEADS)]
        for stage in (load, operands, scores, assemble, finish):
            stage(units, t)
        return carry

    lax.fori_loop(0, nchunk, chunk, 0)


def hgrn_pass(backward, q, k, logf, v, B, S, extra=()):
    T = B * S
    nseq = HG_SEQS if B % HG_SEQS == 0 else 1
    ts = HG_TS if nseq > 1 else TS
    nS = S // ts
    masks = jnp.asarray(_hgrn_constants(backward), F32)
    cum = masks.astype(BF16)

    tile = pl.BlockSpec((nseq, ts, D_MODEL), lambda b, i: (b, (nS - 1 - i) if backward else i, 0))
    seqs = lambda a: a.reshape(B, S, D_MODEL)
    in_specs = [tile] * 4 + [_const_spec(cum.shape), _const_spec(masks.shape)]
    args = [seqs(q), seqs(k), seqs(logf), seqs(v), cum, masks]
    if not backward:
        ob, og, nw = extra
        in_specs += [tile, tile, _const_spec(nw.shape)]
        args += [seqs(ob), seqs(og), nw]
    return pl.pallas_call(
        functools.partial(_hgrn_body, backward),
        grid=(B // nseq, nS),
        in_specs=in_specs,
        out_specs=tile,
        out_shape=jax.ShapeDtypeStruct((B, S, D_MODEL), BF16),
        scratch_shapes=[pltpu.VMEM((nseq, HG_HEADS, HG_V, HG_K), F32)],
        compiler_params=_cparams("parallel", "arbitrary"),
        name="hgrn_bwd" if backward else "hgrn_fwd",
    )(*args).reshape(T, D_MODEL)


def _t5_bucket(rel):
    half = T5_BUCKETS // 2
    max_exact = half // 2
    n = np.abs(rel)
    large = max_exact + (np.log(np.maximum(n, 1) / max_exact)
                         / np.log(T5_MAX_DIST / max_exact) * (half - max_exact)).astype(np.int32)
    large = np.minimum(large, half - 1)
    return ((rel > 0).astype(np.int32) * half + np.where(n < max_exact, n, large)).astype(np.int32)


def _attention_bias(t5_bias):
    qi = np.arange(A_BLOCK)[None, :]
    kj = np.arange(3 * A_BLOCK)[:, None] - A_BLOCK
    rel = kj - qi
    onehot = (_t5_bucket(rel)[None] == np.arange(T5_BUCKETS)[:, None, None]).astype(np.float32)
    bias = jnp.einsum("bh,bkq->hkq", t5_bias.astype(F32), onehot, precision=lax.Precision.HIGHEST)
    return jnp.where(np.abs(rel)[None] <= A_BLOCK, bias, NEG)


def _trunk(x, B, S, p):
    for l in range(DEPTH):
        j = l // 2
        g = p["gains"][l]
        if l % 2 == 0:
            e = p["even"][j]
            qkv, z, xbc, dt = in_proj_even(x, g[0:1], e["wqkv"], e["wz"], e["wxbc"], e["wdt"])
            a_out = windowed_attention(qkv, p["attn_bias"], e["sink"], B, S)
            common = (dt, e["a_log"], e["dt_bias"], B, S)
            yb, xact = ssd_pass(True, xbc, *common, extra=(e["conv_w"], e["conv_b"]))
            b_out = ssd_pass(False, xact, *common, extra=(yb, z, e["d_skip"], e["norm_w"]))
            mixes, wouts = [a_out, b_out], [e["wout_a"], e["wout_b"]]
        else:
            o = p["odd"][j]
            q, kf, kb, lf, lbw, v, og = in_proj_odd(x, g[0:1], o["lb"], o["w_in"])
            ob = hgrn_pass(True, q, kb, lbw, v, B, S)
            mix = hgrn_pass(False, q, kf, lf, v, B, S, extra=(ob, og, o["norm_w"]))
            mixes, wouts = [mix], [o["w_out"]]
        f = p["ffn"][l]
        x = out_proj_ffn(mixes, wouts, x, g[1:4], f["wg"], f["wu"], f["wd"])
    return x


def kernel(x_prompt, x_sample, norm_gains, t5_bias, ev_w_in, attn_sink, ssd_conv_w, ssd_conv_b, ssd_a_log,
           ssd_dt_bias, ssd_d, ssd_norm_w, ev_w_out, od_w_in, hg_lower_bounds, hg_norm_w, od_w_out,
           ffn_w_gate, ffn_w_up, ffn_w_down):
    nh2 = 2 * SSD_HEADS
    c_qkv = A_Q_DIM + 2 * A_KV_DIM
    c_z = c_qkv + SSD_INNER
    c_xbc = c_z + SSD_XBC
    dt_cols = np.array(SSD_HEAD_ORDER + [SSD_HEADS + h for h in SSD_HEAD_ORDER])
    lb_soft = jax.nn.softmax(hg_lower_bounds.astype(F32), axis=0)
    lb_all = jnp.cumsum(lb_soft, axis=0) - lb_soft[0]
    p = {"gains": norm_gains.astype(F32), "attn_bias": _attention_bias(t5_bias), "even": [], "odd": [], "ffn": []}
    for j in range(ev_w_in.shape[0]):
        w = ev_w_in[j]
        p["even"].append(dict(
            wqkv=w[:, :c_qkv].astype(BF16), wz=w[:, c_qkv:c_z].astype(BF16), wxbc=w[:, c_z:c_xbc].astype(BF16),
            wdt=w[:, c_xbc:][:, dt_cols].astype(BF16),
            sink=attn_sink[j].astype(F32).reshape(1, A_HEADS),
            conv_w=ssd_conv_w[j].astype(F32), conv_b=ssd_conv_b[j].astype(F32).reshape(1, SSD_XBC),
            a_log=ssd_a_log[j].astype(F32).reshape(1, nh2)[:, dt_cols],
            dt_bias=ssd_dt_bias[j].astype(F32).reshape(1, nh2)[:, dt_cols],
            d_skip=jnp.repeat(ssd_d[j].astype(F32), SSD_HEAD_DIM).reshape(1, SSD_INNER),
            norm_w=ssd_norm_w[j].astype(F32).reshape(1, SSD_INNER),
            wout_a=ev_w_out[j, :A_Q_DIM].astype(BF16), wout_b=ev_w_out[j, A_Q_DIM:].astype(BF16)))
    for j in range(od_w_in.shape[0]):
        p["odd"].append(dict(
            w_in=od_w_in[j].astype(BF16), lb=lb_all[j].reshape(1, HG_HEADS * HG_K),
            norm_w=hg_norm_w[j].astype(F32).reshape(1, HG_V), w_out=od_w_out[j].astype(BF16)))
    for l in range(DEPTH):
        p["ffn"].append(dict(wg=ffn_w_gate[l].astype(BF16), wu=ffn_w_up[l].astype(BF16),
                             wd=ffn_w_down[l].astype(BF16)))
    outs = []
    for x in (x_prompt, x_sample):
        B, S, _ = x.shape
        outs.append(_trunk(x.reshape(B * S, D_MODEL), B, S, p).reshape(B, S, D_MODEL))
    return tuple(outs)
```

```python
import functools

import numpy as np
import jax
import jax.numpy as jnp
from jax import lax
from jax.experimental import pallas as pl
from jax.experimental.pallas import tpu as pltpu

F32 = jnp.float32
BF16 = jnp.bfloat16

D_MODEL = 1024
DEPTH = 4
EPS = 1e-6
NEG = -1e30

A_HEADS = 8
A_KV_HEADS = 2
A_HEAD_DIM = 64
A_BLOCK = 128
A_Q_DIM = A_HEADS * A_HEAD_DIM
A_KV_DIM = A_KV_HEADS * A_HEAD_DIM
T5_BUCKETS = 32
T5_MAX_DIST = 128

SSD_HEADS = 16
SSD_HEAD_DIM = 64
SSD_INNER = SSD_HEADS * SSD_HEAD_DIM
SSD_GROUPS = 2
SSD_STATE = 64
SSD_CONV = 5
SSD_XBC = SSD_INNER + 2 * SSD_GROUPS * SSD_STATE
SSD_L = 128
SSD_CONV_ROWS = 128
HALO = 8

HG_HEADS = 8
HG_K = 128
HG_V = 128
HG_C = 128
HG_SUB = 32
HG_NB = HG_C // HG_SUB
HG_SEQS = 8
HG_TS = 128

D_FF = 2816
MXU_N = 256
FF_CHUNKS = ((0, 6 * MXU_N), (6 * MXU_N, D_FF))

TM = 512
TS = 512
VMEM_LIMIT_BYTES = 56 * 1024 * 1024


def _cparams(*sem):
    return pltpu.CompilerParams(dimension_semantics=sem, vmem_limit_bytes=VMEM_LIMIT_BYTES)


def _const_spec(shape):
    nd = len(shape)
    return pl.BlockSpec(shape, lambda *_: (0,) * nd, pipeline_mode=pl.Buffered(1))


def _rms(x, g):
    return x * lax.rsqrt(jnp.mean(x * x, axis=-1, keepdims=True) + EPS) * g


def _silu(x):
    return x * (1.0 / (1.0 + jnp.exp(-x)))


def _dot(a, b):
    return jnp.dot(a, b, preferred_element_type=F32)


def _dot_nt(a, b):
    return lax.dot_general(a, b, (((1,), (1,)), ((), ())), preferred_element_type=F32)


def _dot_tn(a, b):
    return lax.dot_general(a, b, (((0,), (0,)), ((), ())), preferred_element_type=F32)


def _split_hi_lo(x):
    hi = x.astype(BF16)
    lo = (x - hi.astype(F32)).astype(BF16)
    return hi, lo


def _in_even_body(x_ref, g_ref, wqkv_ref, wz_ref, wxbc_ref, wdt_ref, qkv_ref, z_ref, xbc_ref, dt_ref):
    xn = _rms(x_ref[...], g_ref[...]).astype(BF16)
    qkv_ref[...] = _dot(xn, wqkv_ref[...]).astype(BF16)
    z_ref[...] = _dot(xn, wz_ref[...])
    xbc_ref[...] = _dot(xn, wxbc_ref[...])
    dt_ref[...] = _dot(xn, wdt_ref[...])


def in_proj_even(x, g, wqkv, wz, wxbc, wdt):
    T = x.shape[0]
    nh2 = 2 * SSD_HEADS
    row = lambda c: pl.BlockSpec((TM, c), lambda i: (i, 0))
    return pl.pallas_call(
        _in_even_body,
        grid=(T // TM,),
        in_specs=[row(D_MODEL), _const_spec((1, D_MODEL)), _const_spec(wqkv.shape), _const_spec(wz.shape),
                  _const_spec(wxbc.shape), _const_spec(wdt.shape)],
        out_specs=[row(A_Q_DIM + 2 * A_KV_DIM), row(SSD_INNER), row(SSD_XBC), row(nh2)],
        out_shape=[jax.ShapeDtypeStruct((T, A_Q_DIM + 2 * A_KV_DIM), BF16),
                   jax.ShapeDtypeStruct((T, SSD_INNER), F32),
                   jax.ShapeDtypeStruct((T, SSD_XBC), F32),
                   jax.ShapeDtypeStruct((T, nh2), F32)],
        compiler_params=_cparams("parallel"),
        name="in_proj_even",
    )(x, g, wqkv, wz, wxbc, wdt)


def _in_odd_body(x_ref, g_ref, lb_ref, w_ref, q_ref, kf_ref, kb_ref, lf_ref, lbw_ref, v_ref, og_ref):
    xn = _rms(x_ref[...], g_ref[...]).astype(BF16)
    W = D_MODEL
    seg = lambda j: _dot(xn, w_ref[:, j * W:(j + 1) * W])
    q_ref[...] = _silu(seg(0)).astype(BF16)
    lb = lb_ref[...]
    for j, k_ref, l_ref in ((1, kf_ref, lf_ref), (2, kb_ref, lbw_ref)):
        sg = 1.0 / (1.0 + jnp.exp(-seg(j)))
        k_ref[...] = ((1.0 - lb) * (1.0 - sg)).astype(BF16)
        l_ref[...] = jnp.log(lb + (1.0 - lb) * sg)
    v_ref[...] = seg(3).astype(BF16)
    og_ref[...] = _silu(seg(4)).astype(BF16)


def in_proj_odd(x, g, lb, w):
    T = x.shape[0]
    row = pl.BlockSpec((TM, D_MODEL), lambda i: (i, 0))
    dtypes = [BF16, BF16, BF16, F32, F32, BF16, BF16]
    return pl.pallas_call(
        _in_odd_body,
        grid=(T // TM,),
        in_specs=[row, _const_spec((1, D_MODEL)), _const_spec(lb.shape), _const_spec(w.shape)],
        out_specs=[row] * len(dtypes),
        out_shape=[jax.ShapeDtypeStruct((T, D_MODEL), d) for d in dtypes],
        compiler_params=_cparams("parallel"),
        name="in_proj_odd",
    )(x, g, lb, w)


def _out_ffn_body(n_mix, *refs):
    mix_refs = refs[:n_mix]
    wout_refs = refs[n_mix:2 * n_mix]
    x_ref, g_ref, wg_ref, wu_ref, wd_ref, o_ref = refs[2 * n_mix:]
    halves = [slice(0, TM // 2), slice(TM // 2, TM)]
    ms = []
    for r in halves:
        m = _dot(mix_refs[0][r, :], wout_refs[0][...])
        for a_ref, w_ref in zip(mix_refs[1:], wout_refs[1:]):
            m = m + _dot(a_ref[r, :], w_ref[...])
        ms.append(m)
    x1s = [x_ref[r, :] + _rms(m, g_ref[0:1, :]) for r, m in zip(halves, ms)]
    hns = [_rms(x1, g_ref[1:2, :]).astype(BF16) for x1 in x1s]
    fs = [None] * len(halves)
    for c0, c1 in FF_CHUNKS:
        gates = [_dot(hn, wg_ref[:, c0:c1]) for hn in hns]
        ups = [_dot(hn, wu_ref[:, c0:c1]) for hn in hns]
        hs = [(_silu(gate) * up).astype(BF16) for gate, up in zip(gates, ups)]
        parts = [_dot(h, wd_ref[c0:c1, :]) for h in hs]
        fs = [part if f is None else f + part for f, part in zip(fs, parts)]
    for r, x1, f in zip(halves, x1s, fs):
        o_ref[r, :] = x1 + _rms(f, g_ref[2:3, :])


def out_proj_ffn(mixes, wouts, x, g3, wg, wu, wd):
    T = x.shape[0]
    n_mix = len(mixes)
    row = lambda c: pl.BlockSpec((TM, c), lambda i: (i, 0))
    in_specs = ([row(a.shape[1]) for a in mixes] + [_const_spec(w.shape) for w in wouts]
                + [row(D_MODEL), _const_spec(g3.shape), _const_spec(wg.shape), _const_spec(wu.shape),
                   _const_spec(wd.shape)])
    return pl.pallas_call(
        functools.partial(_out_ffn_body, n_mix),
        grid=(T // TM,),
        in_specs=in_specs,
        out_specs=row(D_MODEL),
        out_shape=jax.ShapeDtypeStruct((T, D_MODEL), F32),
        compiler_params=_cparams("parallel"),
        name="out_proj_ffn",
    )(*mixes, *wouts, x, g3, wg, wu, wd)


def _attn_body(q_ref, kp_ref, kc_ref, kn_ref, vp_ref, vc_ref, vn_ref, bias_ref, sink_ref, o_ref, kbuf, vtbuf):
    i = pl.program_id(1)
    n = pl.num_programs(1)
    nsub = TS // A_BLOCK
    dh = A_HEAD_DIM
    nk = TS + 2 * A_BLOCK
    row = lax.broadcasted_iota(jnp.int32, (2 * dh, nk), 0)
    for g in range(A_KV_HEADS):
        vs = []
        for lo, k_ref, v_ref in ((0, kp_ref, vp_ref), (A_BLOCK, kc_ref, vc_ref), (A_BLOCK + TS, kn_ref, vn_ref)):
            kbuf[g, lo:lo + k_ref.shape[0], :] = k_ref[:, g * dh:(g + 1) * dh]
            vs.append(v_ref[...])
        vt = jnp.concatenate(vs, axis=0).astype(F32).T[g * dh:(g + 1) * dh]
        vt = jnp.concatenate([vt, jnp.zeros_like(vt)], axis=0)
        vtbuf[g] = jnp.where(row == dh, 1.0, vt).astype(BF16)
    key = lax.broadcasted_iota(jnp.int32, (3 * A_BLOCK, A_BLOCK), 0)
    rep = A_HEADS // A_KV_HEADS
    heads = range(A_HEADS)

    def scores(sb):
        r0 = sb * A_BLOCK
        return [_dot_nt(kbuf[h // rep, r0:r0 + 3 * A_BLOCK, :], q_ref[r0:r0 + A_BLOCK, h * dh:(h + 1) * dh])
                for h in heads]

    def softmax_numerators(sb, ss):
        edge = None
        if sb == 0:
            edge = jnp.where(key < jnp.where(i == 0, A_BLOCK, 0), NEG, 0.0)
        if sb == nsub - 1:
            e2 = jnp.where(key >= jnp.where(i == n - 1, 2 * A_BLOCK, 3 * A_BLOCK), NEG, 0.0)
            edge = e2 if edge is None else edge + e2
        ps, ms = [], []
        for h in heads:
            s = ss[h] * (dh ** -0.5) + bias_ref[h]
            if edge is not None:
                s = s + edge
            m = jnp.maximum(jnp.max(s, axis=0, keepdims=True), sink_ref[0:1, h:h + 1])
            ps.append(jnp.exp(s - m).astype(BF16))
            ms.append(m)
        return ps, ms

    def weighted_values(sb, ps):
        r0 = sb * A_BLOCK
        return [_dot(vtbuf[h // rep, :, r0:r0 + 3 * A_BLOCK], ps[h]) for h in heads]

    def finish(sb, pv, ms):
        r0 = sb * A_BLOCK
        outs = []
        for h in heads:
            denom = pv[h][dh:dh + 1, :] + jnp.exp(sink_ref[0:1, h:h + 1] - ms[h])
            outs.append(pv[h][:dh, :] / denom)
        o_ref[r0:r0 + A_BLOCK, :] = jnp.concatenate(outs, axis=0).T.astype(o_ref.dtype)

    ss, pm, pv = {0: scores(0)}, {}, {}
    for sb in range(nsub + 1):
        if sb + 1 < nsub:
            ss[sb + 1] = scores(sb + 1)
        if sb < nsub:
            pm[sb] = softmax_numerators(sb, ss.pop(sb))
            pv[sb] = weighted_values(sb, pm[sb][0])
        if sb >= 1:
            finish(sb - 1, pv.pop(sb - 1), pm.pop(sb - 1)[1])


def windowed_attention(qkv, bias, sink, B, S):
    T = B * S
    nS = S // TS
    sub = TS // A_BLOCK
    nblk = T // A_BLOCK
    kcol = A_Q_DIM // A_KV_DIM
    vcol = kcol + 1
    cur = lambda c: pl.BlockSpec((TS, A_KV_DIM), lambda b, i: (b * nS + i, c))
    prev = lambda c: pl.BlockSpec((A_BLOCK, A_KV_DIM),
                                  lambda b, i: (jnp.maximum((b * nS + i) * sub - 1, 0), c))
    nxt = lambda c: pl.BlockSpec((A_BLOCK, A_KV_DIM),
                                 lambda b, i: (jnp.minimum((b * nS + i + 1) * sub, nblk - 1), c))
    return pl.pallas_call(
        _attn_body,
        grid=(B, nS),
        in_specs=[pl.BlockSpec((TS, A_Q_DIM), lambda b, i: (b * nS + i, 0)),
                  prev(kcol), cur(kcol), nxt(kcol), prev(vcol), cur(vcol), nxt(vcol),
                  _const_spec(bias.shape), _const_spec(sink.shape)],
        out_specs=pl.BlockSpec((TS, A_Q_DIM), lambda b, i: (b * nS + i, 0)),
        out_shape=jax.ShapeDtypeStruct((T, A_Q_DIM), BF16),
        scratch_shapes=[pltpu.VMEM((A_KV_HEADS, TS + 2 * A_BLOCK, A_HEAD_DIM), BF16),
                        pltpu.VMEM((A_KV_HEADS, 2 * A_HEAD_DIM, TS + 2 * A_BLOCK), BF16)],
        compiler_params=_cparams("parallel", "parallel"),
        name="windowed_attention",
    )(qkv, qkv, qkv, qkv, qkv, qkv, qkv, bias, sink)


def _split3_tiled(x):
    h = x.shape[1]
    t = jnp.concatenate([x, x, x], axis=1)
    hi = t.astype(BF16).astype(F32)
    r1 = t - hi
    mid = r1.astype(BF16).astype(F32)
    lane = lax.broadcasted_iota(jnp.int32, t.shape, 1)
    return jnp.where(lane < h, hi, jnp.where(lane < 2 * h, mid, r1 - mid)).astype(BF16)


SSD_HEAD_ORDER = list(range(0, SSD_HEADS, 2)) + list(range(1, SSD_HEADS, 2))


def _ssd_constants():
    H, P, L = SSD_HEADS, SSD_HEAD_DIM, SSD_L
    selb = np.zeros((3 * H, H * L), np.float32)
    selx = np.zeros((3 * H, H * P), np.float32)
    for c, head in enumerate(SSD_HEAD_ORDER):
        for part in range(3):
            selb[part * H + c, head * L:(head + 1) * L] = 1.0
            selx[part * H + c, head * P:(head + 1) * P] = 1.0
    return selb, selx


def _ssd_body(backward, *refs):
    if backward:
        (xc_ref, xp_ref, xn_ref, dt_ref, cw_ref, cb_ref, alog_ref, dtb_ref,
         selb_ref, selx_ref, y_ref, xa_ref, xpad, hst, csb_s, ex_s, sx_s, rowp_s) = refs
    else:
        (xa_ref, dt_ref, alog_ref, dtb_ref, selb_ref, selx_ref,
         yb_ref, z_ref, dsk_ref, nw_ref, y_ref, hst, csb_s, ex_s, sx_s, rowp_s) = refs
    i = pl.program_id(1)
    n = pl.num_programs(1)
    L = SSD_L
    H = SSD_HEADS
    P = SSD_HEAD_DIM
    nchunk = TS // L
    d0 = H if backward else 0
    gw = SSD_INNER // SSD_GROUPS

    @pl.when(i == 0)
    def _():
        hst[...] = jnp.zeros_like(hst)

    if backward:
        t = n - 1 - i
        rows = TS + 2 * HALO
        xpad[0:HALO, :] = jnp.where(t == 0, 0.0, xp_ref[...])
        xpad[HALO:HALO + TS, :] = xc_ref[...]
        xpad[HALO + TS:rows, :] = jnp.where(t == n - 1, 0.0, xn_ref[...])
        CR = SSD_CONV_ROWS
        ng = CR // HALO
        sub = lax.broadcasted_iota(jnp.int32, (1, HALO, 1), 1)

        def conv_rows(rb, carry):
            r0 = pl.multiple_of(rb * CR, CR)
            xg = xpad[pl.ds(r0, CR + 2 * HALO), :].reshape(ng + 2, HALO, SSD_XBC)
            acc = cb_ref[...] + cw_ref[SSD_CONV // 2:SSD_CONV // 2 + 1, :] * xg[1:ng + 1]
            for j in range(SSD_CONV):
                d = j - SSD_CONV // 2
                if d > 0:
                    u = pltpu.roll(xg, HALO - d, 1)
                    acc = acc + cw_ref[j:j + 1, :] * jnp.where(sub < HALO - d, u[1:ng + 1], u[2:ng + 2])
                elif d < 0:
                    u = pltpu.roll(xg, -d, 1)
                    acc = acc + cw_ref[j:j + 1, :] * jnp.where(sub >= -d, u[1:ng + 1], u[0:ng])
            xa_ref[pl.ds(r0, CR), :] = _silu(acc).reshape(CR, SSD_XBC).astype(xa_ref.dtype)
            return carry

        lax.fori_loop(0, TS // CR, conv_rows, 0)

    a_row = -jnp.exp(alog_ref[...])[:, d0:d0 + H]
    r_i = lax.broadcasted_iota(jnp.int32, (L, L), 0)
    c_i = lax.broadcasted_iota(jnp.int32, (L, L), 1)
    keep = (c_i >= r_i) if backward else (c_i <= r_i)
    tri = jnp.where(keep, 1.0, 0.0).astype(BF16)
    r_2 = lax.broadcasted_iota(jnp.int32, (L, 2 * L), 0)
    c_2 = lax.broadcasted_iota(jnp.int32, (L, 2 * L), 1)
    c_2 = jnp.where(c_2 >= L, c_2 - L, c_2)
    keep2 = (c_2 >= r_2) if backward else (c_2 <= r_2)
    last = 0 if backward else L - 1
    lane = lax.broadcasted_iota(jnp.int32, (1, 2 * P), 1)
    hp = H // 2

    css, ecss, scales = [], [], []
    for c in range(nchunk):
        dt = jax.nn.softplus(dt_ref[c * L:(c + 1) * L, :] + dtb_ref[...])[:, d0:d0 + H]
        w_hi, w_lo = _split_hi_lo(dt * a_row)
        cs = _dot(tri, w_hi) + _dot(tri, w_lo)
        css.append(cs)
        ecss.append(jnp.exp(cs))
        scales.append(dt * jnp.exp(cs[last:last + 1, :] - cs))
        tt = jnp.concatenate([cs, dt, jnp.zeros((L, 2 * L - 2 * H), F32)], axis=1)
        tt = jnp.concatenate([tt, jnp.zeros_like(tt)], axis=0).T
        for q, src in enumerate((tt[0:H], tt[H:2 * H])):
            rowp_s[q * hp:(q + 1) * hp, c * 2 * L:(c + 1) * 2 * L] = jnp.concatenate(
                [src[0:hp, 0:L], src[hp:H, 0:L]], axis=1)
    csb_s[...] = _dot(_split3_tiled(jnp.concatenate(css, axis=0)), selb_ref[...])
    ex_s[...] = _dot(_split3_tiled(jnp.concatenate(ecss, axis=0)), selx_ref[...])
    sx_s[...] = _dot(_split3_tiled(jnp.concatenate(scales, axis=0)), selx_ref[...])

    def chunk(ci, carry):
        c = (nchunk - 1 - ci) if backward else ci
        r0 = pl.multiple_of(c * L, L)
        p0 = pl.multiple_of(c * 2 * L, 2 * L)
        csb = csb_s[pl.ds(r0, L), :]
        ex = ex_s[pl.ds(r0, L), :]
        sx = sx_s[pl.ds(r0, L), :]
        csp = rowp_s[0:hp, pl.ds(p0, 2 * L)]
        dtp = rowp_s[hp:H, pl.ds(p0, 2 * L)].astype(BF16)
        etx = ex[last:last + 1, :]
        xs = xa_ref[pl.ds(r0, L), 0:SSD_INNER]
        bm = xa_ref[pl.ds(r0, L), SSD_INNER:SSD_INNER + SSD_GROUPS * SSD_STATE]
        cm = xa_ref[pl.ds(r0, L), SSD_INNER + SSD_GROUPS * SSD_STATE:SSD_XBC]
        xw = xs * sx.astype(BF16)
        cbs, yoffs = [], []
        for g in range(SSD_GROUPS):
            bg = bm[:, g * SSD_STATE:(g + 1) * SSD_STATE]
            cg = cm[:, g * SSD_STATE:(g + 1) * SSD_STATE]
            cb = _dot_nt(cg, bg)
            cbs.append(jnp.concatenate([cb, cb], axis=1).astype(BF16))
            hg = hst[g]
            yoffs.append(_dot(cg, hg.astype(BF16)) * ex[:, g * gw:(g + 1) * gw])
            hst[g] = etx[:, g * gw:(g + 1) * gw] * hg + _dot_tn(bg, xw[:, g * gw:(g + 1) * gw])
        ys = []
        for j in range(hp):
            seg = csb[:, j * 2 * L:(j + 1) * 2 * L] - csp[j:j + 1, :]
            dm = jnp.exp(jnp.where(keep2, seg, NEG)).astype(BF16) * dtp[j:j + 1, :]
            m = cbs[2 * j // (H // SSD_GROUPS)] * dm
            xp2 = xs[:, 2 * j * P:(2 * j + 2) * P]
            zero = jnp.zeros_like(xp2)
            rhs = jnp.concatenate([jnp.where(lane < P, xp2, zero), jnp.where(lane >= P, xp2, zero)], axis=0)
            ys.append(_dot(m, rhs))
        y = jnp.concatenate(ys, axis=1) + jnp.concatenate(yoffs, axis=1)
        if backward:
            y_ref[pl.ds(r0, L), :] = y
        else:
            y = y + yb_ref[pl.ds(r0, L), :] + dsk_ref[...] * xs.astype(F32)
            y = y * _silu(z_ref[pl.ds(r0, L), :])
            outs = []
            for g in range(SSD_GROUPS):
                yg = y[:, g * gw:(g + 1) * gw]
                outs.append(yg * lax.rsqrt(jnp.mean(yg * yg, axis=-1, keepdims=True) + EPS))
            y_ref[pl.ds(r0, L), :] = (jnp.concatenate(outs, axis=1) * nw_ref[...]).astype(y_ref.dtype)
        return carry

    lax.fori_loop(0, nchunk, chunk, 0)


def ssd_pass(backward, src, dt, alog, dtb, B, S, extra):
    T = B * S
    nS = S // TS
    per = TS // HALO
    nh2 = 2 * SSD_HEADS
    selb, selx = (jnp.asarray(m, BF16) for m in _ssd_constants())

    def blk(b, i):
        return b * nS + ((nS - 1 - i) if backward else i)

    row = lambda c: pl.BlockSpec((TS, c), lambda b, i: (blk(b, i), 0))
    dt_specs = [row(nh2)]
    par_specs = [_const_spec(alog.shape), _const_spec(dtb.shape),
                 _const_spec(selb.shape), _const_spec(selx.shape)]
    state = [pltpu.VMEM((SSD_GROUPS, SSD_STATE, SSD_INNER // SSD_GROUPS), F32),
             pltpu.VMEM((TS, SSD_HEADS * SSD_L), F32), pltpu.VMEM((TS, SSD_INNER), F32),
             pltpu.VMEM((TS, SSD_INNER), F32), pltpu.VMEM((SSD_HEADS, 2 * TS), F32)]
    if backward:
        cw, cb = extra
        in_specs = ([row(SSD_XBC),
                     pl.BlockSpec((HALO, SSD_XBC), lambda b, i: (jnp.maximum(blk(b, i) * per - 1, 0), 0)),
                     pl.BlockSpec((HALO, SSD_XBC),
                                  lambda b, i: (jnp.minimum((blk(b, i) + 1) * per, T // HALO - 1), 0))]
                    + dt_specs + [_const_spec(cw.shape), _const_spec(cb.shape)] + par_specs)
        args = [src, src, src, dt, cw, cb, alog, dtb, selb, selx]
        out_specs = [row(SSD_INNER), row(SSD_XBC)]
        out_shape = [jax.ShapeDtypeStruct((T, SSD_INNER), F32), jax.ShapeDtypeStruct((T, SSD_XBC), BF16)]
        scratch = [pltpu.VMEM((TS + 2 * HALO, SSD_XBC), F32)] + state
    else:
        yb, z, dsk, nw = extra
        in_specs = ([row(SSD_XBC)] + dt_specs + par_specs
                    + [row(SSD_INNER), row(SSD_INNER), _const_spec(dsk.shape), _const_spec(nw.shape)])
        args = [src, dt, alog, dtb, selb, selx, yb, z, dsk, nw]
        out_specs = row(SSD_INNER)
        out_shape = jax.ShapeDtypeStruct((T, SSD_INNER), BF16)
        scratch = state
    return pl.pallas_call(
        functools.partial(_ssd_body, backward),
        grid=(B, nS),
        in_specs=in_specs,
        out_specs=out_specs,
        out_shape=out_shape,
        scratch_shapes=scratch,
        compiler_params=_cparams("parallel", "arbitrary"),
        name="ssd_bwd" if backward else "ssd_fwd",
    )(*args)


def _hgrn_constants(backward):
    C = HG_C
    t = np.arange(C)[:, None]
    u = np.arange(C)[None, :]
    if backward:
        t, u = C - 1 - t, C - 1 - u
    return ((t // HG_SUB == u // HG_SUB) & (u <= t)).astype(np.float32)


def _rows_to_tile(rows):
    return jnp.concatenate([jnp.broadcast_to(r, (HG_SUB, r.shape[1])) for r in rows], axis=0)


def _hgrn_body(backward, *refs):
    if backward:
        (q_ref, k_ref, l_ref, v_ref, cum_ref, msk_ref, o_ref, st) = refs
    else:
        (q_ref, k_ref, l_ref, v_ref, cum_ref, msk_ref, ob_ref, g_ref, nw_ref, o_ref, st) = refs
    i = pl.program_id(1)
    C, NB = HG_C, HG_NB
    nseq = q_ref.shape[0]
    nchunk = q_ref.shape[1] // C
    order = list(range(NB - 1, -1, -1)) if backward else list(range(NB))
    end_row = 0 if backward else HG_SUB - 1

    @pl.when(i == 0)
    def _():
        st[...] = jnp.zeros_like(st)

    cpi = 2 if nseq == 1 else 1

    def chunk(ci, carry):
        cs = [ci * cpi + j for j in range(cpi)]
        r0s = [pl.multiple_of(((nchunk - 1 - c) if backward else c) * C, C) for c in cs]
        cum = cum_ref[...]
        hsl = [slice(h * HG_K, (h + 1) * HG_K) for h in range(HG_HEADS)]
        lane_blk = lax.broadcasted_iota(jnp.int32, (HG_SUB, C), 1) // HG_SUB

        def load(heads, t):
            t["k"] = {h: k_ref[h[0], pl.ds(r0s[h[2]], C), hsl[h[1]]].astype(F32) for h in heads}
            t["q"] = {h: q_ref[h[0], pl.ds(r0s[h[2]], C), hsl[h[1]]].astype(F32) for h in heads}
            t["v"] = {h: v_ref[h[0], pl.ds(r0s[h[2]], C), hsl[h[1]]] for h in heads}
            hl = {h: jnp.concatenate(_split_hi_lo(l_ref[h[0], pl.ds(r0s[h[2]], C), hsl[h[1]]]), axis=1)
                  for h in heads}
            t["gl"] = {h: _dot(cum, hl[h]) for h in heads}

        def operands(heads, t):
            for name in ("qt", "qd", "kt", "kh", "qin", "kst", "dec"):
                t[name] = {}
            for h in heads:
                gl = t["gl"][h][:, :HG_K] + t["gl"][h][:, HG_K:]
                tot = [gl[j * HG_SUB + end_row:j * HG_SUB + end_row + 1, :] for j in range(NB)]
                zero = jnp.zeros_like(tot[0])
                pre, post, near = [None] * NB, [None] * NB, [[None] * NB for _ in range(NB)]
                for p, b in enumerate(order):
                    pre[b] = sum((tot[order[pp]] for pp in range(p)), zero)
                    post[b] = sum((tot[order[pp]] for pp in range(p + 1, NB)), zero)
                    for d in range(2, NB):
                        near[d][b] = sum((tot[order[pp]] for pp in range(max(p - d + 1, 0), p)), zero)
                qt = (t["q"][h] * jnp.exp(gl)).astype(BF16)
                kt = (t["k"][h] * jnp.exp(-gl)).astype(BF16)
                bexp = lambda rows: _rows_to_tile([jnp.exp(x).astype(BF16) for x in rows])
                kh = kt * bexp(tot)
                t["qin"][h] = qt * bexp(pre)
                t["kst"][h] = kh * bexp(post)
                qd = []
                for d in range(1, NB):
                    blocks = sorted(order[d:])
                    rows = slice(blocks[0] * HG_SUB, (blocks[-1] + 1) * HG_SUB)
                    qd.append(qt[rows] if d == 1 else qt[rows] * bexp([near[d][b] for b in blocks]))
                t["qd"][h] = jnp.concatenate(qd, axis=0)
                t["qt"][h], t["kt"][h], t["kh"][h] = qt, kt, kh
                t["dec"][h] = jnp.exp(sum(tot, zero))

        def scores(heads, t):
            t["att0"] = {h: _dot_nt(t["qt"][h], t["kt"][h]) for h in heads}
            t["offs"] = {h: _dot_nt(t["qd"][h], t["kh"][h]) for h in heads}

        def assemble(heads, t):
            t["att"] = {}
            for h in heads:
                rows_out = []
                for p, b in enumerate(order):
                    r = t["att0"][h][b * HG_SUB:(b + 1) * HG_SUB] * msk_ref[b * HG_SUB:(b + 1) * HG_SUB, :]
                    base = 0
                    for d in range(1, NB):
                        blocks = sorted(order[d:])
                        if p >= d:
                            o0 = base + (b - blocks[0]) * HG_SUB
                            r = jnp.where(lane_blk == order[p - d], t["offs"][h][o0:o0 + HG_SUB], r)
                        base += len(blocks) * HG_SUB
                    rows_out.append((b, r))
                t["att"][h] = jnp.concatenate([r for _, r in sorted(rows_out, key=lambda x: x[0])],
                                              axis=0).astype(BF16)
            t["sT"] = {h: st[h[0], h[1]] for h in heads}
            t["out"] = {h: _dot(t["att"][h], t["v"][h]) + _dot_nt(t["qin"][h], t["sT"][h].astype(BF16))
                        for h in heads}
            t["upd"] = {h: _dot_tn(t["v"][h], t["kst"][h]) for h in heads}

        def finish(heads, t):
            for h in heads:
                rows, cols = (h[0], pl.ds(r0s[h[2]], C)), hsl[h[1]]
                st[h[0], h[1]] = t["sT"][h] * t["dec"][h] + t["upd"][h]
                o = t["out"][h]
                if backward:
                    o_ref[rows + (cols,)] = o.astype(o_ref.dtype)
                else:
                    o = o + ob_ref[rows + (cols,)].astype(F32)
                    o = o * lax.rsqrt(jnp.mean(o * o, axis=-1, keepdims=True) + EPS) * nw_ref[...]
                    o_ref[rows + (cols,)] = (o * g_ref[rows + (cols,)].astype(F32)).astype(o_ref.dtype)

        t = {}
        units = [[(sq, h, j) for sq in range(nseq) for h in range(HG_HEADS)] for j in range(cpi)]
        for stage in (load, operands, scores):
            stage([u for us in units for u in us], t)
        for us in units:
            assemble(us, t)
            finish(us, t)
        return carry

    lax.fori_loop(0, nchunk // cpi, chunk, 0)


def hgrn_pass(backward, q, k, logf, v, B, S, extra=()):
    T = B * S
    nseq = HG_SEQS if B % HG_SEQS == 0 else 1
    ts = HG_TS if nseq > 1 else TS
    nS = S // ts
    masks = jnp.asarray(_hgrn_constants(backward), F32)
    cum = masks.astype(BF16)

    tile = pl.BlockSpec((nseq, ts, D_MODEL), lambda b, i: (b, (nS - 1 - i) if backward else i, 0))
    seqs = lambda a: a.reshape(B, S, D_MODEL)
    in_specs = [tile] * 4 + [_const_spec(cum.shape), _const_spec(masks.shape)]
    args = [seqs(q), seqs(k), seqs(logf), seqs(v), cum, masks]
    if not backward:
        ob, og, nw = extra
        in_specs += [tile, tile, _const_spec(nw.shape)]
        args += [seqs(ob), seqs(og), nw]
    return pl.pallas_call(
        functools.partial(_hgrn_body, backward),
        grid=(B // nseq, nS),
        in_specs=in_specs,
        out_specs=tile,
        out_shape=jax.ShapeDtypeStruct((B, S, D_MODEL), BF16),
        scratch_shapes=[pltpu.VMEM((nseq, HG_HEADS, HG_V, HG_K), F32)],
        compiler_params=_cparams("parallel", "arbitrary"),
        name="hgrn_bwd" if backward else "hgrn_fwd",
    )(*args).reshape(T, D_MODEL)


def _t5_bucket(rel):
    half = T5_BUCKETS // 2
    max_exact = half // 2
    n = np.abs(rel)
    large = max_exact + (np.log(np.maximum(n, 1) / max_exact)
                         / np.log(T5_MAX_DIST / max_exact) * (half - max_exact)).astype(np.int32)
    large = np.minimum(large, half - 1)
    return ((rel > 0).astype(np.int32) * half + np.where(n < max_exact, n, large)).astype(np.int32)


def _attention_bias(t5_bias):
    qi = np.arange(A_BLOCK)[None, :]
    kj = np.arange(3 * A_BLOCK)[:, None] - A_BLOCK
    rel = kj - qi
    onehot = (_t5_bucket(rel)[None] == np.arange(T5_BUCKETS)[:, None, None]).astype(np.float32)
    bias = jnp.einsum("bh,bkq->hkq", t5_bias.astype(F32), onehot, precision=lax.Precision.HIGHEST)
    return jnp.where(np.abs(rel)[None] <= A_BLOCK, bias, NEG)


def _trunk(x, B, S, p):
    for l in range(DEPTH):
        j = l // 2
        g = p["gains"][l]
        if l % 2 == 0:
            e = p["even"][j]
            qkv, z, xbc, dt = in_proj_even(x, g[0:1], e["wqkv"], e["wz"], e["wxbc"], e["wdt"])
            a_out = windowed_attention(qkv, p["attn_bias"], e["sink"], B, S)
            common = (dt, e["a_log"], e["dt_bias"], B, S)
            yb, xact = ssd_pass(True, xbc, *common, extra=(e["conv_w"], e["conv_b"]))
            b_out = ssd_pass(False, xact, *common, extra=(yb, z, e["d_skip"], e["norm_w"]))
            mixes, wouts = [a_out, b_out], [e["wout_a"], e["wout_b"]]
        else:
            o = p["odd"][j]
            q, kf, kb, lf, lbw, v, og = in_proj_odd(x, g[0:1], o["lb"], o["w_in"])
            ob = hgrn_pass(True, q, kb, lbw, v, B, S)
            mix = hgrn_pass(False, q, kf, lf, v, B, S, extra=(ob, og, o["norm_w"]))
            mixes, wouts = [mix], [o["w_out"]]
        f = p["ffn"][l]
        x = out_proj_ffn(mixes, wouts, x, g[1:4], f["wg"], f["wu"], f["wd"])
    return x


def kernel(x_prompt, x_sample, norm_gains, t5_bias, ev_w_in, attn_sink, ssd_conv_w, ssd_conv_b, ssd_a_log,
           ssd_dt_bias, ssd_d, ssd_norm_w, ev_w_out, od_w_in, hg_lower_bounds, hg_norm_w, od_w_out,
           ffn_w_gate, ffn_w_up, ffn_w_down):
    nh2 = 2 * SSD_HEADS
    c_qkv = A_Q_DIM + 2 * A_KV_DIM
    c_z = c_qkv + SSD_INNER
    c_xbc = c_z + SSD_XBC
    dt_cols = np.array(SSD_HEAD_ORDER + [SSD_HEADS + h for h in SSD_HEAD_ORDER])
    lb_soft = jax.nn.softmax(hg_lower_bounds.astype(F32), axis=0)
    lb_all = jnp.cumsum(lb_soft, axis=0) - lb_soft[0]
    p = {"gains": norm_gains.astype(F32), "attn_bias": _attention_bias(t5_bias), "even": [], "odd": [], "ffn": []}
    for j in range(ev_w_in.shape[0]):
        w = ev_w_in[j]
        p["even"].append(dict(
            wqkv=w[:, :c_qkv].astype(BF16), wz=w[:, c_qkv:c_z].astype(BF16), wxbc=w[:, c_z:c_xbc].astype(BF16),
            wdt=w[:, c_xbc:][:, dt_cols].astype(BF16),
            sink=attn_sink[j].astype(F32).reshape(1, A_HEADS),
            conv_w=ssd_conv_w[j].astype(F32), conv_b=ssd_conv_b[j].astype(F32).reshape(1, SSD_XBC),
            a_log=ssd_a_log[j].astype(F32).reshape(1, nh2)[:, dt_cols],
            dt_bias=ssd_dt_bias[j].astype(F32).reshape(1, nh2)[:, dt_cols],
            d_skip=jnp.repeat(ssd_d[j].astype(F32), SSD_HEAD_DIM).reshape(1, SSD_INNER),
            norm_w=ssd_norm_w[j].astype(F32).reshape(1, SSD_INNER),
            wout_a=ev_w_out[j, :A_Q_DIM].astype(BF16), wout_b=ev_w_out[j, A_Q_DIM:].astype(BF16)))
    for j in range(od_w_in.shape[0]):
        p["odd"].append(dict(
            w_in=od_w_in[j].astype(BF16), lb=lb_all[j].reshape(1, HG_HEADS * HG_K),
            norm_w=hg_norm_w[j].astype(F32).reshape(1, HG_V), w_out=od_w_out[j].astype(BF16)))
    for l in range(DEPTH):
        p["ffn"].append(dict(wg=ffn_w_gate[l].astype(BF16), wu=ffn_w_up[l].astype(BF16),
                             wd=ffn_w_down[l].astype(BF16)))
    outs = []
    for x in (x_prompt, x_sample):
        B, S, _ = x.shape
        outs.append(_trunk(x.reshape(B * S, D_MODEL), B, S, p).reshape(B, S, D_MODEL))
    return tuple(outs)
```

```python
import functools

import numpy as np
import jax
import jax.numpy as jnp
from jax import lax
from jax.experimental import pallas as pl
from jax.experimental.pallas import tpu as pltpu

F32 = jnp.float32
BF16 = jnp.bfloat16

D_MODEL = 1024
DEPTH = 4
EPS = 1e-6
NEG = -1e30

A_HEADS = 8
A_KV_HEADS = 2
A_HEAD_DIM = 64
A_BLOCK = 128
A_Q_DIM = A_HEADS * A_HEAD_DIM
A_KV_DIM = A_KV_HEADS * A_HEAD_DIM
T5_BUCKETS = 32
T5_MAX_DIST = 128

SSD_HEADS = 16
SSD_HEAD_DIM = 64
SSD_INNER = SSD_HEADS * SSD_HEAD_DIM
SSD_GROUPS = 2
SSD_STATE = 64
SSD_CONV = 5
SSD_XBC = SSD_INNER + 2 * SSD_GROUPS * SSD_STATE
SSD_L = 128
SSD_CONV_ROWS = 128
HALO = 8

HG_HEADS = 8
HG_K = 128
HG_V = 128
HG_C = 128
HG_SUB = 32
HG_NB = HG_C // HG_SUB
HG_SEQS = 8
HG_TS = 128

D_FF = 2816
MXU_N = 256
FF_CHUNKS = ((0, 6 * MXU_N), (6 * MXU_N, D_FF))

TM = 512
TS = 512
VMEM_LIMIT_BYTES = 56 * 1024 * 1024


def _cparams(*sem):
    return pltpu.CompilerParams(dimension_semantics=sem, vmem_limit_bytes=VMEM_LIMIT_BYTES)


def _const_spec(shape):
    nd = len(shape)
    return pl.BlockSpec(shape, lambda *_: (0,) * nd, pipeline_mode=pl.Buffered(1))


def _rms(x, g):
    return x * lax.rsqrt(jnp.mean(x * x, axis=-1, keepdims=True) + EPS) * g


def _silu(x):
    return x * (1.0 / (1.0 + jnp.exp(-x)))


def _dot(a, b):
    return jnp.dot(a, b, preferred_element_type=F32)


def _dot_nt(a, b):
    return lax.dot_general(a, b, (((1,), (1,)), ((), ())), preferred_element_type=F32)


def _dot_tn(a, b):
    return lax.dot_general(a, b, (((0,), (0,)), ((), ())), preferred_element_type=F32)


def _split_hi_lo(x):
    hi = x.astype(BF16)
    lo = (x - hi.astype(F32)).astype(BF16)
    return hi, lo


def _in_even_body(x_ref, g_ref, wqkv_ref, wz_ref, wxbc_ref, wdt_ref, qkv_ref, z_ref, xbc_ref, dt_ref):
    xn = _rms(x_ref[...], g_ref[...]).astype(BF16)
    qkv_ref[...] = _dot(xn, wqkv_ref[...]).astype(BF16)
    z_ref[...] = _dot(xn, wz_ref[...])
    xbc_ref[...] = _dot(xn, wxbc_ref[...])
    dt_ref[...] = _dot(xn, wdt_ref[...])


def in_proj_even(x, g, wqkv, wz, wxbc, wdt):
    T = x.shape[0]
    nh2 = 2 * SSD_HEADS
    row = lambda c: pl.BlockSpec((TM, c), lambda i: (i, 0))
    return pl.pallas_call(
        _in_even_body,
        grid=(T // TM,),
        in_specs=[row(D_MODEL), _const_spec((1, D_MODEL)), _const_spec(wqkv.shape), _const_spec(wz.shape),
                  _const_spec(wxbc.shape), _const_spec(wdt.shape)],
        out_specs=[row(A_Q_DIM + 2 * A_KV_DIM), row(SSD_INNER), row(SSD_XBC), row(nh2)],
        out_shape=[jax.ShapeDtypeStruct((T, A_Q_DIM + 2 * A_KV_DIM), BF16),
                   jax.ShapeDtypeStruct((T, SSD_INNER), F32),
                   jax.ShapeDtypeStruct((T, SSD_XBC), F32),
                   jax.ShapeDtypeStruct((T, nh2), F32)],
        compiler_params=_cparams("parallel"),
        name="in_proj_even",
    )(x, g, wqkv, wz, wxbc, wdt)


def _in_odd_body(x_ref, g_ref, lb_ref, w_ref, q_ref, kf_ref, kb_ref, lf_ref, lbw_ref, v_ref, og_ref):
    xn = _rms(x_ref[...], g_ref[...]).astype(BF16)
    W = D_MODEL
    seg = lambda j: _dot(xn, w_ref[:, j * W:(j + 1) * W])
    q_ref[...] = _silu(seg(0)).astype(BF16)
    lb = lb_ref[...]
    for j, k_ref, l_ref in ((1, kf_ref, lf_ref), (2, kb_ref, lbw_ref)):
        sg = 1.0 / (1.0 + jnp.exp(-seg(j)))
        k_ref[...] = ((1.0 - lb) * (1.0 - sg)).astype(BF16)
        l_ref[...] = jnp.log(lb + (1.0 - lb) * sg)
    v_ref[...] = seg(3).astype(BF16)
    og_ref[...] = _silu(seg(4)).astype(BF16)


def in_proj_odd(x, g, lb, w):
    T = x.shape[0]
    row = pl.BlockSpec((TM, D_MODEL), lambda i: (i, 0))
    dtypes = [BF16, BF16, BF16, F32, F32, BF16, BF16]
    return pl.pallas_call(
        _in_odd_body,
        grid=(T // TM,),
        in_specs=[row, _const_spec((1, D_MODEL)), _const_spec(lb.shape), _const_spec(w.shape)],
        out_specs=[row] * len(dtypes),
        out_shape=[jax.ShapeDtypeStruct((T, D_MODEL), d) for d in dtypes],
        compiler_params=_cparams("parallel"),
        name="in_proj_odd",
    )(x, g, lb, w)


def _out_ffn_body(n_mix, *refs):
    mix_refs = refs[:n_mix]
    wout_refs = refs[n_mix:2 * n_mix]
    x_ref, g_ref, wg_ref, wu_ref, wd_ref, o_ref = refs[2 * n_mix:]
    halves = [slice(0, TM // 2), slice(TM // 2, TM)]
    ms = []
    for r in halves:
        m = _dot(mix_refs[0][r, :], wout_refs[0][...])
        for a_ref, w_ref in zip(mix_refs[1:], wout_refs[1:]):
            m = m + _dot(a_ref[r, :], w_ref[...])
        ms.append(m)
    x1s = [x_ref[r, :] + _rms(m, g_ref[0:1, :]) for r, m in zip(halves, ms)]
    hns = [_rms(x1, g_ref[1:2, :]).astype(BF16) for x1 in x1s]
    fs = [None] * len(halves)
    for c0, c1 in FF_CHUNKS:
        gates = [_dot(hn, wg_ref[:, c0:c1]) for hn in hns]
        ups = [_dot(hn, wu_ref[:, c0:c1]) for hn in hns]
        hs = [(_silu(gate) * up).astype(BF16) for gate, up in zip(gates, ups)]
        parts = [_dot(h, wd_ref[c0:c1, :]) for h in hs]
        fs = [part if f is None else f + part for f, part in zip(fs, parts)]
    for r, x1, f in zip(halves, x1s, fs):
        o_ref[r, :] = x1 + _rms(f, g_ref[2:3, :])


def out_proj_ffn(mixes, wouts, x, g3, wg, wu, wd):
    T = x.shape[0]
    n_mix = len(mixes)
    row = lambda c: pl.BlockSpec((TM, c), lambda i: (i, 0))
    in_specs = ([row(a.shape[1]) for a in mixes] + [_const_spec(w.shape) for w in wouts]
                + [row(D_MODEL), _const_spec(g3.shape), _const_spec(wg.shape), _const_spec(wu.shape),
                   _const_spec(wd.shape)])
    return pl.pallas_call(
        functools.partial(_out_ffn_body, n_mix),
        grid=(T // TM,),
        in_specs=in_specs,
        out_specs=row(D_MODEL),
        out_shape=jax.ShapeDtypeStruct((T, D_MODEL), F32),
        compiler_params=_cparams("parallel"),
        name="out_proj_ffn",
    )(*mixes, *wouts, x, g3, wg, wu, wd)


def _attn_body(q_ref, kp_ref, kc_ref, kn_ref, vp_ref, vc_ref, vn_ref, bias_ref, sink_ref, o_ref, kbuf, vtbuf):
    i = pl.program_id(1)
    n = pl.num_programs(1)
    nsub = TS // A_BLOCK
    dh = A_HEAD_DIM
    nk = TS + 2 * A_BLOCK
    row = lax.broadcasted_iota(jnp.int32, (2 * dh, nk), 0)
    for g in range(A_KV_HEADS):
        vs = []
        for lo, k_ref, v_ref in ((0, kp_ref, vp_ref), (A_BLOCK, kc_ref, vc_ref), (A_BLOCK + TS, kn_ref, vn_ref)):
            kbuf[g, lo:lo + k_ref.shape[0], :] = k_ref[:, g * dh:(g + 1) * dh]
            vs.append(v_ref[...])
        vt = jnp.concatenate(vs, axis=0).astype(F32).T[g * dh:(g + 1) * dh]
        vt = jnp.concatenate([vt, jnp.zeros_like(vt)], axis=0)
        vtbuf[g] = jnp.where(row == dh, 1.0, vt).astype(BF16)
    key = lax.broadcasted_iota(jnp.int32, (3 * A_BLOCK, A_BLOCK), 0)
    rep = A_HEADS // A_KV_HEADS
    heads = range(A_HEADS)

    def scores(sb):
        r0 = sb * A_BLOCK
        return [_dot_nt(kbuf[h // rep, r0:r0 + 3 * A_BLOCK, :], q_ref[r0:r0 + A_BLOCK, h * dh:(h + 1) * dh])
                for h in heads]

    def softmax_numerators(sb, ss):
        edge = None
        if sb == 0:
            edge = jnp.where(key < jnp.where(i == 0, A_BLOCK, 0), NEG, 0.0)
        if sb == nsub - 1:
            e2 = jnp.where(key >= jnp.where(i == n - 1, 2 * A_BLOCK, 3 * A_BLOCK), NEG, 0.0)
            edge = e2 if edge is None else edge + e2
        ps, ms = [], []
        for h in heads:
            s = ss[h] * (dh ** -0.5) + bias_ref[h]
            if edge is not None:
                s = s + edge
            m = jnp.maximum(jnp.max(s, axis=0, keepdims=True), sink_ref[0:1, h:h + 1])
            ps.append(jnp.exp(s - m).astype(BF16))
            ms.append(m)
        return ps, ms

    def weighted_values(sb, ps):
        r0 = sb * A_BLOCK
        return [_dot(vtbuf[h // rep, :, r0:r0 + 3 * A_BLOCK], ps[h]) for h in heads]

    def finish(sb, pv, ms):
        r0 = sb * A_BLOCK
        outs = []
        for h in heads:
            denom = pv[h][dh:dh + 1, :] + jnp.exp(sink_ref[0:1, h:h + 1] - ms[h])
            outs.append(pv[h][:dh, :] / denom)
        o_ref[r0:r0 + A_BLOCK, :] = jnp.concatenate(outs, axis=0).T.astype(o_ref.dtype)

    ss, pm, pv = {0: scores(0)}, {}, {}
    for sb in range(nsub + 1):
        if sb + 1 < nsub:
            ss[sb + 1] = scores(sb + 1)
        if sb < nsub:
            pm[sb] = softmax_numerators(sb, ss.pop(sb))
            pv[sb] = weighted_values(sb, pm[sb][0])
        if sb >= 1:
            finish(sb - 1, pv.pop(sb - 1), pm.pop(sb - 1)[1])


def windowed_attention(qkv, bias, sink, B, S):
    T = B * S
    nS = S // TS
    sub = TS // A_BLOCK
    nblk = T // A_BLOCK
    kcol = A_Q_DIM // A_KV_DIM
    vcol = kcol + 1
    cur = lambda c: pl.BlockSpec((TS, A_KV_DIM), lambda b, i: (b * nS + i, c))
    prev = lambda c: pl.BlockSpec((A_BLOCK, A_KV_DIM),
                                  lambda b, i: (jnp.maximum((b * nS + i) * sub - 1, 0), c))
    nxt = lambda c: pl.BlockSpec((A_BLOCK, A_KV_DIM),
                                 lambda b, i: (jnp.minimum((b * nS + i + 1) * sub, nblk - 1), c))
    return pl.pallas_call(
        _attn_body,
        grid=(B, nS),
        in_specs=[pl.BlockSpec((TS, A_Q_DIM), lambda b, i: (b * nS + i, 0)),
                  prev(kcol), cur(kcol), nxt(kcol), prev(vcol), cur(vcol), nxt(vcol),
                  _const_spec(bias.shape), _const_spec(sink.shape)],
        out_specs=pl.BlockSpec((TS, A_Q_DIM), lambda b, i: (b * nS + i, 0)),
        out_shape=jax.ShapeDtypeStruct((T, A_Q_DIM), BF16),
        scratch_shapes=[pltpu.VMEM((A_KV_HEADS, TS + 2 * A_BLOCK, A_HEAD_DIM), BF16),
                        pltpu.VMEM((A_KV_HEADS, 2 * A_HEAD_DIM, TS + 2 * A_BLOCK), BF16)],
        compiler_params=_cparams("parallel", "parallel"),
        name="windowed_attention",
    )(qkv, qkv, qkv, qkv, qkv, qkv, qkv, bias, sink)


def _split3_tiled(x):
    h = x.shape[1]
    t = jnp.concatenate([x, x, x], axis=1)
    hi = t.astype(BF16).astype(F32)
    r1 = t - hi
    mid = r1.astype(BF16).astype(F32)
    lane = lax.broadcasted_iota(jnp.int32, t.shape, 1)
    return jnp.where(lane < h, hi, jnp.where(lane < 2 * h, mid, r1 - mid)).astype(BF16)


SSD_HEAD_ORDER = list(range(0, SSD_HEADS, 2)) + list(range(1, SSD_HEADS, 2))


def _ssd_constants():
    H, P, L = SSD_HEADS, SSD_HEAD_DIM, SSD_L
    selb = np.zeros((3 * H, H * L), np.float32)
    selx = np.zeros((3 * H, H * P), np.float32)
    for c, head in enumerate(SSD_HEAD_ORDER):
        for part in range(3):
            selb[part * H + c, head * L:(head + 1) * L] = 1.0
            selx[part * H + c, head * P:(head + 1) * P] = 1.0
    return selb, selx


def _ssd_body(backward, *refs):
    if backward:
        (xc_ref, xp_ref, xn_ref, dt_ref, cw_ref, cb_ref, alog_ref, dtb_ref,
         selb_ref, selx_ref, y_ref, xa_ref, xpad, hst, csb_s, ex_s, sx_s, rowp_s) = refs
    else:
        (xa_ref, dt_ref, alog_ref, dtb_ref, selb_ref, selx_ref,
         yb_ref, z_ref, dsk_ref, nw_ref, y_ref, hst, csb_s, ex_s, sx_s, rowp_s) = refs
    i = pl.program_id(1)
    n = pl.num_programs(1)
    L = SSD_L
    H = SSD_HEADS
    P = SSD_HEAD_DIM
    nchunk = TS // L
    d0 = H if backward else 0
    gw = SSD_INNER // SSD_GROUPS

    @pl.when(i == 0)
    def _():
        hst[...] = jnp.zeros_like(hst)

    if backward:
        t = n - 1 - i
        rows = TS + 2 * HALO
        xpad[0:HALO, :] = jnp.where(t == 0, 0.0, xp_ref[...])
        xpad[HALO:HALO + TS, :] = xc_ref[...]
        xpad[HALO + TS:rows, :] = jnp.where(t == n - 1, 0.0, xn_ref[...])
        CR = SSD_CONV_ROWS
        ng = CR // HALO
        sub = lax.broadcasted_iota(jnp.int32, (1, HALO, 1), 1)

        def conv_rows(rb, carry):
            r0 = pl.multiple_of(rb * CR, CR)
            xg = xpad[pl.ds(r0, CR + 2 * HALO), :].reshape(ng + 2, HALO, SSD_XBC)
            acc = cb_ref[...] + cw_ref[SSD_CONV // 2:SSD_CONV // 2 + 1, :] * xg[1:ng + 1]
            for j in range(SSD_CONV):
                d = j - SSD_CONV // 2
                if d > 0:
                    u = pltpu.roll(xg, HALO - d, 1)
                    acc = acc + cw_ref[j:j + 1, :] * jnp.where(sub < HALO - d, u[1:ng + 1], u[2:ng + 2])
                elif d < 0:
                    u = pltpu.roll(xg, -d, 1)
                    acc = acc + cw_ref[j:j + 1, :] * jnp.where(sub >= -d, u[1:ng + 1], u[0:ng])
            xa_ref[pl.ds(r0, CR), :] = _silu(acc).reshape(CR, SSD_XBC).astype(xa_ref.dtype)
            return carry

        lax.fori_loop(0, TS // CR, conv_rows, 0)

    a_row = -jnp.exp(alog_ref[...])[:, d0:d0 + H]
    r_i = lax.broadcasted_iota(jnp.int32, (L, L), 0)
    c_i = lax.broadcasted_iota(jnp.int32, (L, L), 1)
    keep = (c_i >= r_i) if backward else (c_i <= r_i)
    tri = jnp.where(keep, 1.0, 0.0).astype(BF16)
    r_2 = lax.broadcasted_iota(jnp.int32, (L, 2 * L), 0)
    c_2 = lax.broadcasted_iota(jnp.int32, (L, 2 * L), 1)
    c_2 = jnp.where(c_2 >= L, c_2 - L, c_2)
    keep2 = (c_2 >= r_2) if backward else (c_2 <= r_2)
    last = 0 if backward else L - 1
    lane = lax.broadcasted_iota(jnp.int32, (1, 2 * P), 1)
    hp = H // 2

    css, ecss, scales = [], [], []
    for c in range(nchunk):
        dt = jax.nn.softplus(dt_ref[c * L:(c + 1) * L, :] + dtb_ref[...])[:, d0:d0 + H]
        w_hi, w_lo = _split_hi_lo(dt * a_row)
        cs = _dot(tri, w_hi) + _dot(tri, w_lo)
        css.append(cs)
        ecss.append(jnp.exp(cs))
        scales.append(dt * jnp.exp(cs[last:last + 1, :] - cs))
        tt = jnp.concatenate([cs, dt, jnp.zeros((L, 2 * L - 2 * H), F32)], axis=1)
        tt = jnp.concatenate([tt, jnp.zeros_like(tt)], axis=0).T
        for q, src in enumerate((tt[0:H], tt[H:2 * H])):
            rowp_s[q * hp:(q + 1) * hp, c * 2 * L:(c + 1) * 2 * L] = jnp.concatenate(
                [src[0:hp, 0:L], src[hp:H, 0:L]], axis=1)
    csb_s[...] = _dot(_split3_tiled(jnp.concatenate(css, axis=0)), selb_ref[...])
    ex_s[...] = _dot(_split3_tiled(jnp.concatenate(ecss, axis=0)), selx_ref[...])
    sx_s[...] = _dot(_split3_tiled(jnp.concatenate(scales, axis=0)), selx_ref[...])

    def chunk(ci, carry):
        c = (nchunk - 1 - ci) if backward else ci
        r0 = pl.multiple_of(c * L, L)
        p0 = pl.multiple_of(c * 2 * L, 2 * L)
        csb = csb_s[pl.ds(r0, L), :]
        ex = ex_s[pl.ds(r0, L), :]
        sx = sx_s[pl.ds(r0, L), :]
        csp = rowp_s[0:hp, pl.ds(p0, 2 * L)]
        dtp = rowp_s[hp:H, pl.ds(p0, 2 * L)].astype(BF16)
        etx = ex[last:last + 1, :]
        xs = xa_ref[pl.ds(r0, L), 0:SSD_INNER]
        bm = xa_ref[pl.ds(r0, L), SSD_INNER:SSD_INNER + SSD_GROUPS * SSD_STATE]
        cm = xa_ref[pl.ds(r0, L), SSD_INNER + SSD_GROUPS * SSD_STATE:SSD_XBC]
        xw = xs * sx.astype(BF16)
        cbs, yoffs = [], []
        for g in range(SSD_GROUPS):
            bg = bm[:, g * SSD_STATE:(g + 1) * SSD_STATE]
            cg = cm[:, g * SSD_STATE:(g + 1) * SSD_STATE]
            cb = _dot_nt(cg, bg)
            cbs.append(jnp.concatenate([cb, cb], axis=1).astype(BF16))
            hg = hst[g]
            yoffs.append(_dot(cg, hg.astype(BF16)) * ex[:, g * gw:(g + 1) * gw])
            hst[g] = etx[:, g * gw:(g + 1) * gw] * hg + _dot_tn(bg, xw[:, g * gw:(g + 1) * gw])
        ys = []
        for j in range(hp):
            seg = csb[:, j * 2 * L:(j + 1) * 2 * L] - csp[j:j + 1, :]
            dm = jnp.exp(jnp.where(keep2, seg, NEG)).astype(BF16) * dtp[j:j + 1, :]
            m = cbs[2 * j // (H // SSD_GROUPS)] * dm
            xp2 = xs[:, 2 * j * P:(2 * j + 2) * P]
            zero = jnp.zeros_like(xp2)
            rhs = jnp.concatenate([jnp.where(lane < P, xp2, zero), jnp.where(lane >= P, xp2, zero)], axis=0)
            ys.append(_dot(m, rhs))
        y = jnp.concatenate(ys, axis=1) + jnp.concatenate(yoffs, axis=1)
        if backward:
            y_ref[pl.ds(r0, L), :] = y
        else:
            y = y + yb_ref[pl.ds(r0, L), :] + dsk_ref[...] * xs.astype(F32)
            y = y * _silu(z_ref[pl.ds(r0, L), :])
            outs = []
            for g in range(SSD_GROUPS):
                yg = y[:, g * gw:(g + 1) * gw]
                outs.append(yg * lax.rsqrt(jnp.mean(yg * yg, axis=-1, keepdims=True) + EPS))
            y_ref[pl.ds(r0, L), :] = (jnp.concatenate(outs, axis=1) * nw_ref[...]).astype(y_ref.dtype)
        return carry

    lax.fori_loop(0, nchunk, chunk, 0)


def ssd_pass(backward, src, dt, alog, dtb, B, S, extra):
    T = B * S
    nS = S // TS
    per = TS // HALO
    nh2 = 2 * SSD_HEADS
    selb, selx = (jnp.asarray(m, BF16) for m in _ssd_constants())

    def blk(b, i):
        return b * nS + ((nS - 1 - i) if backward else i)

    row = lambda c: pl.BlockSpec((TS, c), lambda b, i: (blk(b, i), 0))
    dt_specs = [row(nh2)]
    par_specs = [_const_spec(alog.shape), _const_spec(dtb.shape),
                 _const_spec(selb.shape), _const_spec(selx.shape)]
    state = [pltpu.VMEM((SSD_GROUPS, SSD_STATE, SSD_INNER // SSD_GROUPS), F32),
             pltpu.VMEM((TS, SSD_HEADS * SSD_L), F32), pltpu.VMEM((TS, SSD_INNER), F32),
             pltpu.VMEM((TS, SSD_INNER), F32), pltpu.VMEM((SSD_HEADS, 2 * TS), F32)]
    if backward:
        cw, cb = extra
        in_specs = ([row(SSD_XBC),
                     pl.BlockSpec((HALO, SSD_XBC), lambda b, i: (jnp.maximum(blk(b, i) * per - 1, 0), 0)),
                     pl.BlockSpec((HALO, SSD_XBC),
                                  lambda b, i: (jnp.minimum((blk(b, i) + 1) * per, T // HALO - 1), 0))]
                    + dt_specs + [_const_spec(cw.shape), _const_spec(cb.shape)] + par_specs)
        args = [src, src, src, dt, cw, cb, alog, dtb, selb, selx]
        out_specs = [row(SSD_INNER), row(SSD_XBC)]
        out_shape = [jax.ShapeDtypeStruct((T, SSD_INNER), F32), jax.ShapeDtypeStruct((T, SSD_XBC), BF16)]
        scratch = [pltpu.VMEM((TS + 2 * HALO, SSD_XBC), F32)] + state
    else:
        yb, z, dsk, nw = extra
        in_specs = ([row(SSD_XBC)] + dt_specs + par_specs
                    + [row(SSD_INNER), row(SSD_INNER), _const_spec(dsk.shape), _const_spec(nw.shape)])
        args = [src, dt, alog, dtb, selb, selx, yb, z, dsk, nw]
        out_specs = row(SSD_INNER)
        out_shape = jax.ShapeDtypeStruct((T, SSD_INNER), BF16)
        scratch = state
    return pl.pallas_call(
        functools.partial(_ssd_body, backward),
        grid=(B, nS),
        in_specs=in_specs,
        out_specs=out_specs,
        out_shape=out_shape,
        scratch_shapes=scratch,
        compiler_params=_cparams("parallel", "arbitrary"),
        name="ssd_bwd" if backward else "ssd_fwd",
    )(*args)


def _hgrn_constants(backward):
    C = HG_C
    t = np.arange(C)[:, None]
    u = np.arange(C)[None, :]
    if backward:
        t, u = C - 1 - t, C - 1 - u
    return ((t // HG_SUB == u // HG_SUB) & (u <= t)).astype(np.float32)


def _rows_to_tile(rows):
    return jnp.concatenate([jnp.broadcast_to(r, (HG_SUB, r.shape[1])) for r in rows], axis=0)


def _hgrn_body(backward, *refs):
    if backward:
        (q_ref, k_ref, l_ref, v_ref, cum_ref, msk_ref, o_ref, st) = refs
    else:
        (q_ref, k_ref, l_ref, v_ref, cum_ref, msk_ref, ob_ref, g_ref, nw_ref, o_ref, st) = refs
    i = pl.program_id(1)
    C, NB = HG_C, HG_NB
    nseq = q_ref.shape[0]
    nchunk = q_ref.shape[1] // C
    order = list(range(NB - 1, -1, -1)) if backward else list(range(NB))
    end_row = 0 if backward else HG_SUB - 1

    @pl.when(i == 0)
    def _():
        st[...] = jnp.zeros_like(st)

    cpi = nchunk if nseq == 1 else 1

    def chunk(ci, carry):
        cs = [ci * cpi + j for j in range(cpi)]
        r0s = [pl.multiple_of(((nchunk - 1 - c) if backward else c) * C, C) for c in cs]
        cum = cum_ref[...]
        hsl = [slice(h * HG_K, (h + 1) * HG_K) for h in range(HG_HEADS)]
        lane_blk = lax.broadcasted_iota(jnp.int32, (HG_SUB, C), 1) // HG_SUB

        def load(heads, t):
            t["k"] = {h: k_ref[h[0], pl.ds(r0s[h[2]], C), hsl[h[1]]].astype(F32) for h in heads}
            t["q"] = {h: q_ref[h[0], pl.ds(r0s[h[2]], C), hsl[h[1]]].astype(F32) for h in heads}
            t["v"] = {h: v_ref[h[0], pl.ds(r0s[h[2]], C), hsl[h[1]]] for h in heads}
            hl = {h: jnp.concatenate(_split_hi_lo(l_ref[h[0], pl.ds(r0s[h[2]], C), hsl[h[1]]]), axis=1)
                  for h in heads}
            t["gl"] = {h: _dot(cum, hl[h]) for h in heads}

        def operands(heads, t):
            for name in ("qt", "qd", "kt", "kh", "qin", "kst", "dec"):
                t[name] = {}
            for h in heads:
                gl = t["gl"][h][:, :HG_K] + t["gl"][h][:, HG_K:]
                tot = [gl[j * HG_SUB + end_row:j * HG_SUB + end_row + 1, :] for j in range(NB)]
                zero = jnp.zeros_like(tot[0])
                pre, post, near = [None] * NB, [None] * NB, [[None] * NB for _ in range(NB)]
                for p, b in enumerate(order):
                    pre[b] = sum((tot[order[pp]] for pp in range(p)), zero)
                    post[b] = sum((tot[order[pp]] for pp in range(p + 1, NB)), zero)
                    for d in range(2, NB):
                        near[d][b] = sum((tot[order[pp]] for pp in range(max(p - d + 1, 0), p)), zero)
                qt = (t["q"][h] * jnp.exp(gl)).astype(BF16)
                kt = (t["k"][h] * jnp.exp(-gl)).astype(BF16)
                bexp = lambda rows: _rows_to_tile([jnp.exp(x).astype(BF16) for x in rows])
                kh = kt * bexp(tot)
                t["qin"][h] = qt * bexp(pre)
                t["kst"][h] = kh * bexp(post)
                qd = []
                for d in range(1, NB):
                    blocks = sorted(order[d:])
                    rows = slice(blocks[0] * HG_SUB, (blocks[-1] + 1) * HG_SUB)
                    qd.append(qt[rows] if d == 1 else qt[rows] * bexp([near[d][b] for b in blocks]))
                t["qd"][h] = jnp.concatenate(qd, axis=0)
                t["qt"][h], t["kt"][h], t["kh"][h] = qt, kt, kh
                t["dec"][h] = jnp.exp(sum(tot, zero))

        def scores(heads, t):
            t["att0"] = {h: _dot_nt(t["qt"][h], t["kt"][h]) for h in heads}
            t["offs"] = {h: _dot_nt(t["qd"][h], t["kh"][h]) for h in heads}

        def assemble(heads, t):
            t["att"] = {}
            for h in heads:
                rows_out = []
                for p, b in enumerate(order):
                    r = t["att0"][h][b * HG_SUB:(b + 1) * HG_SUB] * msk_ref[b * HG_SUB:(b + 1) * HG_SUB, :]
                    base = 0
                    for d in range(1, NB):
                        blocks = sorted(order[d:])
                        if p >= d:
                            o0 = base + (b - blocks[0]) * HG_SUB
                            r = jnp.where(lane_blk == order[p - d], t["offs"][h][o0:o0 + HG_SUB], r)
                        base += len(blocks) * HG_SUB
                    rows_out.append((b, r))
                t["att"][h] = jnp.concatenate([r for _, r in sorted(rows_out, key=lambda x: x[0])],
                                              axis=0).astype(BF16)
            t["sT"] = {h: st[h[0], h[1]] for h in heads}
            t["out"] = {h: _dot(t["att"][h], t["v"][h]) + _dot_nt(t["qin"][h], t["sT"][h].astype(BF16))
                        for h in heads}
            t["upd"] = {h: _dot_tn(t["v"][h], t["kst"][h]) for h in heads}

        def finish(heads, t):
            for h in heads:
                rows, cols = (h[0], pl.ds(r0s[h[2]], C)), hsl[h[1]]
                st[h[0], h[1]] = t["sT"][h] * t["dec"][h] + t["upd"][h]
                o = t["out"][h]
                if backward:
                    o_ref[rows + (cols,)] = o.astype(o_ref.dtype)
                else:
                    o = o + ob_ref[rows + (cols,)].astype(F32)
                    o = o * lax.rsqrt(jnp.mean(o * o, axis=-1, keepdims=True) + EPS) * nw_ref[...]
                    o_ref[rows + (cols,)] = (o * g_ref[rows + (cols,)].astype(F32)).astype(o_ref.dtype)

        t = {}
        units = [[(sq, h, j) for sq in range(nseq) for h in range(HG_HEADS)] for j in range(cpi)]
        for stage in (load, operands, scores):
            stage([u for us in units for u in us], t)
        for us in units:
            assemble(us, t)
            finish(us, t)
        return carry

    lax.fori_loop(0, nchunk // cpi, chunk, 0)


def hgrn_pass(backward, q, k, logf, v, B, S, extra=()):
    T = B * S
    nseq = HG_SEQS if B % HG_SEQS == 0 else 1
    ts = HG_TS if nseq > 1 else TS
    nS = S // ts
    masks = jnp.asarray(_hgrn_constants(backward), F32)
    cum = masks.astype(BF16)

    tile = pl.BlockSpec((nseq, ts, D_MODEL), lambda b, i: (b, (nS - 1 - i) if backward else i, 0))
    seqs = lambda a: a.reshape(B, S, D_MODEL)
    in_specs = [tile] * 4 + [_const_spec(cum.shape), _const_spec(masks.shape)]
    args = [seqs(q), seqs(k), seqs(logf), seqs(v), cum, masks]
    if not backward:
        ob, og, nw = extra
        in_specs += [tile, tile, _const_spec(nw.shape)]
        args += [seqs(ob), seqs(og), nw]
    return pl.pallas_call(
        functools.partial(_hgrn_body, backward),
        grid=(B // nseq, nS),
        in_specs=in_specs,
        out_specs=tile,
        out_shape=jax.ShapeDtypeStruct((B, S, D_MODEL), BF16),
        scratch_shapes=[pltpu.VMEM((nseq, HG_HEADS, HG_V, HG_K), F32)],
        compiler_params=_cparams("parallel", "arbitrary"),
        name="hgrn_bwd" if backward else "hgrn_fwd",
    )(*args).reshape(T, D_MODEL)


def _t5_bucket(rel):
    half = T5_BUCKETS // 2
    max_exact = half // 2
    n = np.abs(rel)
    large = max_exact + (np.log(np.maximum(n, 1) / max_exact)
                         / np.log(T5_MAX_DIST / max_exact) * (half - max_exact)).astype(np.int32)
    large = np.minimum(large, half - 1)
    return ((rel > 0).astype(np.int32) * half + np.where(n < max_exact, n, large)).astype(np.int32)


def _attention_bias(t5_bias):
    qi = np.arange(A_BLOCK)[None, :]
    kj = np.arange(3 * A_BLOCK)[:, None] - A_BLOCK
    rel = kj - qi
    onehot = (_t5_bucket(rel)[None] == np.arange(T5_BUCKETS)[:, None, None]).astype(np.float32)
    bias = jnp.einsum("bh,bkq->hkq", t5_bias.astype(F32), onehot, precision=lax.Precision.HIGHEST)
    return jnp.where(np.abs(rel)[None] <= A_BLOCK, bias, NEG)


def _trunk(x, B, S, p):
    for l in range(DEPTH):
        j = l // 2
        g = p["gains"][l]
        if l % 2 == 0:
            e = p["even"][j]
            qkv, z, xbc, dt = in_proj_even(x, g[0:1], e["wqkv"], e["wz"], e["wxbc"], e["wdt"])
            a_out = windowed_attention(qkv, p["attn_bias"], e["sink"], B, S)
            common = (dt, e["a_log"], e["dt_bias"], B, S)
            yb, xact = ssd_pass(True, xbc, *common, extra=(e["conv_w"], e["conv_b"]))
            b_out = ssd_pass(False, xact, *common, extra=(yb, z, e["d_skip"], e["norm_w"]))
            mixes, wouts = [a_out, b_out], [e["wout_a"], e["wout_b"]]
        else:
            o = p["odd"][j]
            q, kf, kb, lf, lbw, v, og = in_proj_odd(x, g[0:1], o["lb"], o["w_in"])
            ob = hgrn_pass(True, q, kb, lbw, v, B, S)
            mix = hgrn_pass(False, q, kf, lf, v, B, S, extra=(ob, og, o["norm_w"]))
            mixes, wouts = [mix], [o["w_out"]]
        f = p["ffn"][l]
        x = out_proj_ffn(mixes, wouts, x, g[1:4], f["wg"], f["wu"], f["wd"])
    return x


def kernel(x_prompt, x_sample, norm_gains, t5_bias, ev_w_in, attn_sink, ssd_conv_w, ssd_conv_b, ssd_a_log,
           ssd_dt_bias, ssd_d, ssd_norm_w, ev_w_out, od_w_in, hg_lower_bounds, hg_norm_w, od_w_out,
           ffn_w_gate, ffn_w_up, ffn_w_down):
    nh2 = 2 * SSD_HEADS
    c_qkv = A_Q_DIM + 2 * A_KV_DIM
    c_z = c_qkv + SSD_INNER
    c_xbc = c_z + SSD_XBC
    dt_cols = np.array(SSD_HEAD_ORDER + [SSD_HEADS + h for h in SSD_HEAD_ORDER])
    lb_soft = jax.nn.softmax(hg_lower_bounds.astype(F32), axis=0)
    lb_all = jnp.cumsum(lb_soft, axis=0) - lb_soft[0]
    p = {"gains": norm_gains.astype(F32), "attn_bias": _attention_bias(t5_bias), "even": [], "odd": [], "ffn": []}
    for j in range(ev_w_in.shape[0]):
        w = ev_w_in[j]
        p["even"].append(dict(
            wqkv=w[:, :c_qkv].astype(BF16), wz=w[:, c_qkv:c_z].astype(BF16), wxbc=w[:, c_z:c_xbc].astype(BF16),
            wdt=w[:, c_xbc:][:, dt_cols].astype(BF16),
            sink=attn_sink[j].astype(F32).reshape(1, A_HEADS),
            conv_w=ssd_conv_w[j].astype(F32), conv_b=ssd_conv_b[j].astype(F32).reshape(1, SSD_XBC),
            a_log=ssd_a_log[j].astype(F32).reshape(1, nh2)[:, dt_cols],
            dt_bias=ssd_dt_bias[j].astype(F32).reshape(1, nh2)[:, dt_cols],
            d_skip=jnp.repeat(ssd_d[j].astype(F32), SSD_HEAD_DIM).reshape(1, SSD_INNER),
            norm_w=ssd_norm_w[j].astype(F32).reshape(1, SSD_INNER),
            wout_a=ev_w_out[j, :A_Q_DIM].astype(BF16), wout_b=ev_w_out[j, A_Q_DIM:].astype(BF16)))
    for j in range(od_w_in.shape[0]):
        p["odd"].append(dict(
            w_in=od_w_in[j].astype(BF16), lb=lb_all[j].reshape(1, HG_HEADS * HG_K),
            norm_w=hg_norm_w[j].astype(F32).reshape(1, HG_V), w_out=od_w_out[j].astype(BF16)))
    for l in range(DEPTH):
        p["ffn"].append(dict(wg=ffn_w_gate[l].astype(BF16), wu=ffn_w_up[l].astype(BF16),
                             wd=ffn_w_down[l].astype(BF16)))
    outs = []
    for x in (x_prompt, x_sample):
        B, S, _ = x.shape
        outs.append(_trunk(x.reshape(B * S, D_MODEL), B, S, p).reshape(B, S, D_MODEL))
    return tuple(outs)
```
